```python
import math
import jax, jax.numpy as jnp
from jax import lax
import numpy as np

D_MODEL = 1024
BATCH = 8
SEQ = 2048
DEPTH = 4
DEC_BATCH = 128
DEC_SEQ = 1
PAST_LEN = 8192
PAGE_SIZE = 128

N_HEADS = 8
N_KV_HEADS = 2
GROUP = N_HEADS // N_KV_HEADS
HEAD_DIM = 64
WINDOW = 128
ATTN_BLOCK = 128
ROPE_THETA = 10000.0
DN_HEADS = 4
DN_DK = 128
DN_DV = 128
CONV_W = 4
DN_CHUNK = 64
D_FF = 2816
EPS = 1e-6

ATTN_Q = N_HEADS * HEAD_DIM
ATTN_KV = N_KV_HEADS * HEAD_DIM
DN_QK = DN_HEADS * DN_DK
DN_V = DN_HEADS * DN_DV
CONV_DIM = 2 * DN_QK + DN_V
SPLIT_SIZES = (ATTN_Q, ATTN_KV, ATTN_KV, CONV_DIM, DN_V, DN_HEADS, DN_HEADS, D_MODEL, D_MODEL)
IN_COLS = sum(SPLIT_SIZES)

kernel_name = "hybrid_swa_sink_gated_deltanet_macaron_step"


def rms_norm(x, w):
    xf = x.astype(jnp.float32)
    y = xf * lax.rsqrt(jnp.mean(xf * xf, axis=-1, keepdims=True) + EPS)
    return (y * w.astype(jnp.float32)).astype(x.dtype)


def l2_norm(x):
    return x * lax.rsqrt(jnp.sum(x * x, axis=-1, keepdims=True) + EPS)


def swiglu_ffn(x, w_gate_up, w_down):
    g, u = jnp.split(x @ w_gate_up, 2, axis=-1)
    return (jax.nn.silu(g) * u) @ w_down


def rope(x, pos):
    half = HEAD_DIM // 2
    inv = 1.0 / (ROPE_THETA ** (jnp.arange(half, dtype=jnp.float32) / half))
    ang = pos.astype(jnp.float32)[:, None] * inv[None, :]
    cos = jnp.cos(ang)[None, :, None, :]
    sin = jnp.sin(ang)[None, :, None, :]
    xf = x.astype(jnp.float32)
    x1, x2 = xf[..., :half], xf[..., half:]
    return jnp.concatenate([x1 * cos - x2 * sin, x2 * cos + x1 * sin], axis=-1).astype(x.dtype)


def split_in(p):
    idx = [int(i) for i in np.cumsum(SPLIT_SIZES)[:-1]]
    return jnp.split(p, idx, axis=-1)


def sink_attention(q, k, v, mask, sinks):
    s = jnp.einsum('bnqhgd,bnkhd->bnhgqk', q, k).astype(jnp.float32) / math.sqrt(HEAD_DIM)
    s = jnp.where(mask[None, :, None, None], s, -jnp.inf)
    sink = jnp.broadcast_to(sinks.astype(jnp.float32).reshape(N_KV_HEADS, GROUP)[None, None, :, :, None, None],
                            s.shape[:-1] + (1,))
    p = jax.nn.softmax(jnp.concatenate([s, sink], axis=-1), axis=-1)[..., :-1]
    return jnp.einsum('bnhgqk,bnkhd->bnqhgd', p.astype(v.dtype), v)


def swa_prompt(q, k, v, sinks):
    B, T = q.shape[:2]
    nb = T // ATTN_BLOCK
    qb = q.reshape(B, nb, ATTN_BLOCK, N_KV_HEADS, GROUP, HEAD_DIM)

    def band(x):
        xp = jnp.concatenate([jnp.zeros_like(x[:, :ATTN_BLOCK]), x], axis=1)
        xp = xp.reshape(B, nb + 1, ATTN_BLOCK, N_KV_HEADS, HEAD_DIM)
        return jnp.concatenate([xp[:, :-1], xp[:, 1:]], axis=2)

    kb, vb = band(k), band(v)
    a = jnp.arange(ATTN_BLOCK)[:, None]
    b = jnp.arange(2 * ATTN_BLOCK)[None, :]
    rel = ATTN_BLOCK + a - b
    in_band = (rel >= 0) & (rel <= WINDOW)
    valid = (jnp.arange(nb)[:, None, None] > 0) | (b[None] >= ATTN_BLOCK)
    mask = in_band[None] & valid
    o = sink_attention(qb, kb, vb, mask, sinks)
    return o.reshape(B, T, ATTN_Q)


def swa_sample(q, k, v, buf_k, buf_v, sinks):
    B, T = q.shape[:2]
    kc = jnp.concatenate([buf_k.astype(k.dtype), k], axis=1)
    vc = jnp.concatenate([buf_v.astype(v.dtype), v], axis=1)
    j = jnp.arange(T)[:, None]
    m = jnp.arange(WINDOW + T)[None, :]
    rel = WINDOW + j - m
    mask = ((rel >= 0) & (rel <= WINDOW))[None]
    o = sink_attention(q.reshape(B, 1, T, N_KV_HEADS, GROUP, HEAD_DIM), kc[:, None], vc[:, None], mask, sinks)
    return o.reshape(B, T, ATTN_Q), kc[:, -WINDOW:], vc[:, -WINDOW:]


def short_conv(x_raw, conv_buf, conv_w):
    T = x_raw.shape[1]
    xp = jnp.concatenate([conv_buf.astype(x_raw.dtype), x_raw], axis=1)
    y = xp[:, 0:T] * conv_w[0]
    for i in range(1, CONV_W):
        y = y + xp[:, i:i + T] * conv_w[i]
    return jax.nn.silu(y), xp[:, -(CONV_W - 1):]


def gdn_chunked(q, k, v, g, beta, S0):
    B, T, H, dk = q.shape
    dv = v.shape[-1]
    C = DN_CHUNK
    n = T // C

    def chunks(x):
        return x.reshape((B, n, C, H) + x.shape[3:]).swapaxes(2, 3)

    qc, kc, vc, gc, bc = chunks(q), chunks(k), chunks(v), chunks(g), chunks(beta)
    G = jnp.cumsum(gc, axis=-1)
    i = jnp.arange(C)[:, None]
    j = jnp.arange(C)[None, :]
    decay = jnp.exp(jnp.where(j <= i, G[..., :, None] - G[..., None, :], -jnp.inf))
    kk = jnp.einsum('bnhid,bnhjd->bnhij', kc, kc)
    A = jnp.where(j < i, bc[..., :, None] * kk * decay, 0.0)
    lhs = jnp.eye(C, dtype=A.dtype) + A
    rhs = jnp.concatenate([vc * bc[..., None], kc * (bc * jnp.exp(G))[..., None]], axis=-1)
    sol = lax.linalg.triangular_solve(lhs, rhs, left_side=True, lower=True, unit_diagonal=True)
    u_base, w = sol[..., :dv], sol[..., dv:]
    qk = jnp.einsum('bnhid,bnhjd->bnhij', qc, kc) * decay
    q_dec = qc * jnp.exp(G)[..., None]
    k_dec = kc * jnp.exp(G[..., -1:] - G)[..., None]
    g_last = jnp.exp(G[..., -1])

    def step(S, xs):
        u_b, w_c, qk_c, qd_c, kd_c, gl_c = xs
        u = u_b - jnp.einsum('bhcd,bhde->bhce', w_c, S)
        o = jnp.einsum('bhcd,bhde->bhce', qd_c, S) + jnp.einsum('bhij,bhje->bhie', qk_c, u)
        S = S * gl_c[..., None, None] + jnp.einsum('bhcd,bhce->bhde', kd_c, u)
        return S, o

    xs = tuple(x.swapaxes(0, 1) for x in (u_base, w, qk, q_dec, k_dec, g_last))
    S, o = lax.scan(step, S0, xs)
    return o.transpose(1, 0, 3, 2, 4).reshape(B, T, H, dv), S


def gdn_recurrent(q, k, v, g, beta, S0):
    def step(S, xs):
        q_t, k_t, v_t, g_t, b_t = xs
        S = S * jnp.exp(g_t)[..., None, None]
        pred = jnp.einsum('bhd,bhde->bhe', k_t, S)
        S = S + jnp.einsum('bhd,bhe->bhde', k_t, b_t[..., None] * (v_t - pred))
        return S, jnp.einsum('bhd,bhde->bhe', q_t, S)

    xs = tuple(x.swapaxes(0, 1) for x in (q, k, v, g, beta))
    S, o = lax.scan(step, S0, xs)
    return o.swapaxes(0, 1), S


def hybrid_layer(x, pos, lw, buf_k, buf_v, dn_state, conv_buf, is_prompt):
    B, T, _ = x.shape
    f32 = jnp.float32
    h = x + 0.5 * swiglu_ffn(rms_norm(x, lw['ffn1_norm']), lw['ffn1_w_gate_up'], lw['ffn1_w_down'])
    u = rms_norm(h, lw['mix_norm'])
    aq, ak, av, dn_raw, dn_z, dn_b, dn_a, gate_a, gate_d = split_in(u @ lw['w_in'])

    q = rope(rms_norm(aq.reshape(B, T, N_HEADS, HEAD_DIM), lw['q_norm']), pos)
    k = rope(rms_norm(ak.reshape(B, T, N_KV_HEADS, HEAD_DIM), lw['k_norm']), pos)
    v = av.reshape(B, T, N_KV_HEADS, HEAD_DIM)
    if is_prompt:
        o_a = swa_prompt(q, k, v, lw['attn_sinks'])
        new_k, new_v = k[:, -WINDOW:], v[:, -WINDOW:]
    else:
        o_a, new_k, new_v = swa_sample(q, k, v, buf_k, buf_v, lw['attn_sinks'])

    if is_prompt:
        conv_buf = jnp.zeros((B, CONV_W - 1, CONV_DIM), dn_raw.dtype)
    qkv, new_conv = short_conv(dn_raw, conv_buf, lw['conv_w'])
    dq, dk, dvv = jnp.split(qkv, [DN_QK, 2 * DN_QK], axis=-1)
    dq = l2_norm(dq.reshape(B, T, DN_HEADS, DN_DK).astype(f32)) * (DN_DK ** -0.5)
    dk = l2_norm(dk.reshape(B, T, DN_HEADS, DN_DK).astype(f32))
    dvv = dvv.reshape(B, T, DN_HEADS, DN_DV).astype(f32)
    beta = jax.nn.sigmoid(dn_b.astype(f32))
    g = -jnp.exp(lw['dn_A_log'].astype(f32)) * jax.nn.softplus(dn_a.astype(f32) + lw['dn_dt_bias'].astype(f32))
    if is_prompt:
        S0 = jnp.zeros((B, DN_HEADS, DN_DK, DN_DV), f32)
        o_d, S = gdn_chunked(dq, dk, dvv, g, beta, S0)
    else:
        o_d, S = gdn_recurrent(dq, dk, dvv, g, beta, dn_state.astype(f32))
    o_d = rms_norm(o_d, lw['dn_out_norm']) * jax.nn.silu(dn_z.reshape(B, T, DN_HEADS, DN_DV).astype(f32))
    o_d = o_d.reshape(B, T, DN_V).astype(x.dtype)

    br_a = o_a @ lw['w_attn_o']
    br_d = o_d @ lw['w_dn_o']
    h = h + (jax.nn.sigmoid(gate_a) * br_a + jax.nn.sigmoid(gate_d) * br_d) @ lw['w_out']
    y = h + 0.5 * swiglu_ffn(rms_norm(h, lw['ffn2_norm']), lw['ffn2_w_gate_up'], lw['ffn2_w_down'])
    return y, new_k, new_v, S.astype(x.dtype), new_conv


def setup_inputs(seed: int = 0) -> dict:
    key = jax.random.key(seed)
    ks = jax.random.split(key, 24)
    nrm = jax.random.normal
    f32 = jnp.float32
    dt = jnp.exp(jax.random.uniform(ks[14], (DEPTH, DN_HEADS), minval=math.log(1e-3), maxval=math.log(1e-1)))
    return {
        "x_prompt": nrm(ks[0], (BATCH, SEQ, D_MODEL), f32),
        "x_sample": nrm(ks[1], (DEC_BATCH, DEC_SEQ, D_MODEL), f32),
        "cache_swa_k": nrm(ks[2], (DEPTH, DEC_BATCH, WINDOW, N_KV_HEADS, HEAD_DIM), f32),
        "cache_swa_v": nrm(ks[3], (DEPTH, DEC_BATCH, WINDOW, N_KV_HEADS, HEAD_DIM), f32),
        "state_dn": 0.05 * nrm(ks[4], (DEPTH, DEC_BATCH, DN_HEADS, DN_DK, DN_DV), f32),
        "state_conv": nrm(ks[5], (DEPTH, DEC_BATCH, CONV_W - 1, CONV_DIM), f32),
        "ffn1_norm": 1.0 + 0.02 * nrm(ks[6], (DEPTH, D_MODEL), f32),
        "ffn1_w_gate_up": nrm(ks[7], (DEPTH, D_MODEL, 2 * D_FF), f32) * D_MODEL ** -0.5,
        "ffn1_w_down": nrm(ks[8], (DEPTH, D_FF, D_MODEL), f32) * D_FF ** -0.5,
        "mix_norm": 1.0 + 0.02 * nrm(ks[9], (DEPTH, D_MODEL), f32),
        "w_in": nrm(ks[10], (DEPTH, D_MODEL, IN_COLS), f32) * D_MODEL ** -0.5,
        "q_norm": 1.0 + 0.02 * nrm(ks[11], (DEPTH, HEAD_DIM), f32),
        "k_norm": 1.0 + 0.02 * nrm(ks[12], (DEPTH, HEAD_DIM), f32),
        "attn_sinks": 0.5 * nrm(ks[13], (DEPTH, N_HEADS), f32),
        "conv_w": nrm(ks[15], (DEPTH, CONV_W, CONV_DIM), f32) * CONV_W ** -0.5,
        "dn_A_log": jnp.log(jax.random.uniform(ks[16], (DEPTH, DN_HEADS), minval=1.0, maxval=16.0)),
        "dn_dt_bias": dt + jnp.log(-jnp.expm1(-dt)),
        "dn_out_norm": 1.0 + 0.02 * nrm(ks[17], (DEPTH, DN_DV), f32),
        "w_attn_o": nrm(ks[18], (DEPTH, ATTN_Q, D_MODEL), f32) * ATTN_Q ** -0.5,
        "w_dn_o": nrm(ks[19], (DEPTH, DN_V, D_MODEL), f32) * DN_V ** -0.5,
        "w_out": nrm(ks[20], (DEPTH, D_MODEL, D_MODEL), f32) * D_MODEL ** -0.5,
        "ffn2_norm": 1.0 + 0.02 * nrm(ks[21], (DEPTH, D_MODEL), f32),
        "ffn2_w_gate_up": nrm(ks[22], (DEPTH, D_MODEL, 2 * D_FF), f32) * D_MODEL ** -0.5,
        "ffn2_w_down": nrm(ks[23], (DEPTH, D_FF, D_MODEL), f32) * D_FF ** -0.5,
    }


def reference(x_prompt, x_sample, cache_swa_k, cache_swa_v, state_dn, state_conv,
              ffn1_norm, ffn1_w_gate_up, ffn1_w_down, mix_norm, w_in, q_norm, k_norm, attn_sinks,
              conv_w, dn_A_log, dn_dt_bias, dn_out_norm, w_attn_o, w_dn_o, w_out,
              ffn2_norm, ffn2_w_gate_up, ffn2_w_down):
    pos_p = jnp.arange(x_prompt.shape[1])
    pos_s = PAST_LEN + jnp.arange(x_sample.shape[1])
    yp, ys = x_prompt, x_sample
    kp_l, vp_l, sp_l, cp_l = [], [], [], []
    ks_l, vs_l, ss_l, cs_l = [], [], [], []
    for l in range(DEPTH):
        lw = dict(ffn1_norm=ffn1_norm[l], ffn1_w_gate_up=ffn1_w_gate_up[l], ffn1_w_down=ffn1_w_down[l],
                  mix_norm=mix_norm[l], w_in=w_in[l], q_norm=q_norm[l], k_norm=k_norm[l],
                  attn_sinks=attn_sinks[l], conv_w=conv_w[l], dn_A_log=dn_A_log[l],
                  dn_dt_bias=dn_dt_bias[l], dn_out_norm=dn_out_norm[l], w_attn_o=w_attn_o[l],
                  w_dn_o=w_dn_o[l], w_out=w_out[l], ffn2_norm=ffn2_norm[l],
                  ffn2_w_gate_up=ffn2_w_gate_up[l], ffn2_w_down=ffn2_w_down[l])
        yp, kp, vp, sp, cp = hybrid_layer(yp, pos_p, lw, None, None, None, None, True)
        ys, kss, vss, sss, css = hybrid_layer(ys, pos_s, lw, cache_swa_k[l], cache_swa_v[l],
                                              state_dn[l], state_conv[l], False)
        kp_l.append(kp); vp_l.append(vp); sp_l.append(sp); cp_l.append(cp)
        ks_l.append(kss); vs_l.append(vss); ss_l.append(sss); cs_l.append(css)
    return (yp, ys,
            jnp.stack(kp_l), jnp.stack(vp_l), jnp.stack(sp_l), jnp.stack(cp_l),
            jnp.stack(ks_l), jnp.stack(vs_l), jnp.stack(ss_l), jnp.stack(cs_l))
```

```python
import functools
import math

import numpy as np
import jax
import jax.numpy as jnp
from jax import lax
from jax.experimental import pallas as pl
from jax.experimental.pallas import tpu as pltpu

F32 = jnp.float32
BF16 = jnp.bfloat16

D_MODEL = 1024
DEPTH = 4
PAST_LEN = 8192
N_HEADS = 8
N_KV_HEADS = 2
GROUP = N_HEADS // N_KV_HEADS
HEAD_DIM = 64
WINDOW = 128
ATTN_BLOCK = 128
ROPE_THETA = 10000.0
DN_HEADS = 4
DN_DK = 128
DN_DV = 128
CONV_W = 4
DN_CHUNK = 64
D_FF = 2816
EPS = 1e-6

ATTN_Q = N_HEADS * HEAD_DIM
ATTN_KV = N_KV_HEADS * HEAD_DIM
DN_QK = DN_HEADS * DN_DK
DN_V = DN_HEADS * DN_DV
CONV_DIM = 2 * DN_QK + DN_V
QKV_W = ATTN_Q + 2 * ATTN_KV
QK_W = ATTN_Q + ATTN_KV
LANES = 128
SUBLANES = 8
TILE = 128
VMEM_LIMIT = 56 * 1024 * 1024

HEAD_PERM = (0, 4, 1, 5, 2, 6, 3, 7)
FF_SPLIT = 2


def _row_tile(rows, cap=512):
    best = SUBLANES
    for t in range(SUBLANES, cap + 1, SUBLANES):
        if rows % t == 0:
            best = t
    return best


def _rms(x, w):
    return x * lax.rsqrt(jnp.mean(x * x, axis=-1, keepdims=True) + EPS) * w


def _dot(a, b):
    return jnp.dot(a, b, preferred_element_type=F32)


def _dot_nt(a, b):
    return lax.dot_general(a, b, (((1,), (1,)), ((), ())), preferred_element_type=F32)


def _split3(x):
    hi = x.astype(BF16)
    r = x - hi.astype(F32)
    mid = r.astype(BF16)
    lo = (r - mid.astype(F32)).astype(BF16)
    return hi, mid, lo


def _dot_exact_rhs(x, m):
    hi, mid, lo = _split3(x)
    return _dot(hi, m) + _dot(mid, m) + _dot(lo, m)


def _dot_exact_lhs(m, x):
    hi, mid, lo = _split3(x)
    return _dot(m, hi) + _dot(m, mid) + _dot(m, lo)


def _const_spec(shape):
    nd = len(shape)
    return pl.BlockSpec(shape, lambda *_: (0,) * nd, pipeline_mode=pl.Buffered(1))


def _layer_spec(l, shape):
    nd = len(shape)
    return pl.BlockSpec((None,) + shape, lambda *_: (l,) + (0,) * nd, pipeline_mode=pl.Buffered(1))


def _ffn_body(x_ref, nw_ref, wgu_ref, wd_ref, o_ref):
    x = x_ref[...]
    xn = _rms(x, nw_ref[...]).astype(BF16)
    tf = D_FF // FF_SPLIT
    acc = jnp.zeros_like(x)
    for c in range(FF_SPLIT):
        g = _dot(xn, wgu_ref[:, c * tf:(c + 1) * tf])
        u = _dot(xn, wgu_ref[:, D_FF + c * tf:D_FF + (c + 1) * tf])
        a = (g * jax.nn.sigmoid(g) * u).astype(BF16)
        acc = acc + _dot(a, wd_ref[c * tf:(c + 1) * tf, :])
    o_ref[...] = x + 0.5 * acc


def _ffn(x, norm_w, wgu, wd, l):
    rows = x.shape[0]
    tm = _row_tile(rows)
    return pl.pallas_call(
        _ffn_body,
        grid=(rows // tm,),
        in_specs=[
            pl.BlockSpec((tm, D_MODEL), lambda i: (i, 0)),
            _layer_spec(l, (1, D_MODEL)),
            _layer_spec(l, (D_MODEL, 2 * D_FF)),
            _layer_spec(l, (D_FF, D_MODEL)),
        ],
        out_specs=pl.BlockSpec((tm, D_MODEL), lambda i: (i, 0)),
        out_shape=jax.ShapeDtypeStruct((rows, D_MODEL), F32),
        compiler_params=pltpu.CompilerParams(
            dimension_semantics=("parallel",), vmem_limit_bytes=VMEM_LIMIT),
        name="ffn",
    )(x, norm_w, wgu, wd)


IN_SEGS = (QKV_W, CONV_DIM, DN_V, LANES, 2 * D_MODEL)
IN_COLS_PAD = sum(IN_SEGS)


def _inproj_body(h_ref, nw_ref, w_ref, qkv_ref, raw_ref, z_ref, ba_ref, gates_ref):
    u = _rms(h_ref[...], nw_ref[...]).astype(BF16)
    off = 0
    for ref, width in zip((qkv_ref, raw_ref, z_ref, ba_ref, gates_ref), IN_SEGS):
        ref[...] = _dot(u, w_ref[:, off:off + width])
        off += width


def _inproj(h, norm_w, w_in, l):
    rows = h.shape[0]
    tm = _row_tile(rows)
    return pl.pallas_call(
        _inproj_body,
        grid=(rows // tm,),
        in_specs=[
            pl.BlockSpec((tm, D_MODEL), lambda i: (i, 0)),
            _layer_spec(l, (1, D_MODEL)),
            _layer_spec(l, (D_MODEL, IN_COLS_PAD)),
        ],
        out_specs=[pl.BlockSpec((tm, w), lambda i: (i, 0)) for w in IN_SEGS],
        out_shape=[jax.ShapeDtypeStruct((rows, w), F32) for w in IN_SEGS],
        compiler_params=pltpu.CompilerParams(
            dimension_semantics=("parallel",), vmem_limit_bytes=VMEM_LIMIT),
        name="inproj",
    )(h, norm_w, w_in)


def _merge_body(h_ref, oa_ref, od_ref, gates_ref, wao_ref, wdo_ref, wout_ref, o_ref):
    br_a = _dot(oa_ref[...].astype(BF16), wao_ref[...])
    br_d = _dot(od_ref[...].astype(BF16), wdo_ref[...])
    ga = gates_ref[:, :D_MODEL]
    gd = gates_ref[:, D_MODEL:]
    m = jax.nn.sigmoid(ga) * br_a + jax.nn.sigmoid(gd) * br_d
    o_ref[...] = h_ref[...] + _dot(m.astype(BF16), wout_ref[...])


def _merge(h, oa, od, gates, wao, wdo, wout, l):
    rows = h.shape[0]
    tm = _row_tile(rows)
    return pl.pallas_call(
        _merge_body,
        grid=(rows // tm,),
        in_specs=[
            pl.BlockSpec((tm, D_MODEL), lambda i: (i, 0)),
            pl.BlockSpec((tm, ATTN_Q), lambda i: (i, 0)),
            pl.BlockSpec((tm, DN_V), lambda i: (i, 0)),
            pl.BlockSpec((tm, 2 * D_MODEL), lambda i: (i, 0)),
            _layer_spec(l, (ATTN_Q, D_MODEL)),
            _layer_spec(l, (DN_V, D_MODEL)),
            _layer_spec(l, (D_MODEL, D_MODEL)),
        ],
        out_specs=pl.BlockSpec((tm, D_MODEL), lambda i: (i, 0)),
        out_shape=jax.ShapeDtypeStruct((rows, D_MODEL), F32),
        compiler_params=pltpu.CompilerParams(
            dimension_semantics=("parallel",), vmem_limit_bytes=VMEM_LIMIT),
        name="merge",
    )(h, oa, od, gates, wao, wdo, wout)


def _norm_rope(x, ones_bd, nw, cos, sin):
    ssq = _dot_exact_rhs(x * x, ones_bd)
    xn = x * lax.rsqrt(ssq * (1.0 / HEAD_DIM) + EPS) * nw
    lane = lax.broadcasted_iota(jnp.int32, (x.shape[0], LANES), 1)
    first_half = (lane & (HEAD_DIM // 2)) == 0
    outs = []
    for g in range(x.shape[1] // LANES):
        xg = xn[:, g * LANES:(g + 1) * LANES]
        partner = jnp.where(first_half,
                            pltpu.roll(xg, LANES - HEAD_DIM // 2, 1),
                            pltpu.roll(xg, HEAD_DIM // 2, 1))
        outs.append(xg * cos + partner * sin)
    return jnp.concatenate(outs, axis=1)


def _attn_prompt_body(sinks_ref, x_ref, cos_ref, sin_ref, nw_ref, ones_ref,
                      o_ref, kout_ref, vout_ref, kprev, vprev):
    blk = pl.program_id(1)

    @pl.when(blk == 0)
    def _():
        kprev[...] = jnp.zeros_like(kprev)
        vprev[...] = jnp.zeros_like(vprev)

    x = x_ref[...]
    qk = _norm_rope(x[:, :QK_W], ones_ref[...], nw_ref[...], cos_ref[...], sin_ref[...])
    kcur = qk[:, ATTN_Q:QK_W]
    vcur = x[:, QK_W:QKV_W]
    kc = jnp.concatenate([kprev[...], kcur.astype(BF16)], axis=0)
    vc = jnp.concatenate([vprev[...], vcur.astype(BF16)], axis=0)

    r = lax.broadcasted_iota(jnp.int32, (ATTN_BLOCK, 2 * ATTN_BLOCK), 0)
    c = lax.broadcasted_iota(jnp.int32, (ATTN_BLOCK, 2 * ATTN_BLOCK), 1)
    first_col = jnp.where(blk == 0, ATTN_BLOCK, 0)
    vis = (c >= r) & (c <= r + WINDOW) & (c >= first_col)
    lo = lax.broadcasted_iota(jnp.int32, (ATTN_BLOCK, LANES), 1) < HEAD_DIM

    scale = 1.0 / math.sqrt(HEAD_DIM)
    outs = []
    for g in range(GROUP):
        qg = qk[:, g * LANES:(g + 1) * LANES] * scale
        res = []
        for half in range(2):
            head = HEAD_PERM[2 * g + half]
            qm = jnp.where(lo if half == 0 else jnp.logical_not(lo), qg, 0.0).astype(BF16)
            s = jnp.where(vis, _dot_nt(qm, kc), -jnp.inf)
            sink = sinks_ref[head]
            m = jnp.maximum(jnp.max(s, axis=1, keepdims=True), sink)
            p = jnp.exp(s - m)
            denom = jnp.sum(p, axis=1, keepdims=True) + jnp.exp(sink - m)
            res.append(_dot(p.astype(BF16), vc) / denom)
        outs.append(jnp.where(lo, res[0], res[1]))
    o_ref[...] = jnp.concatenate(outs, axis=1)

    kprev[...] = kcur.astype(BF16)
    vprev[...] = vcur.astype(BF16)

    @pl.when(blk == pl.num_programs(1) - 1)
    def _():
        kout_ref[...] = kcur
        vout_ref[...] = vcur


def _attn_prompt(qkv, sinks, cos, sin, nw, ones_bd, batch, seq, rows):
    nb = seq // ATTN_BLOCK
    return pl.pallas_call(
        _attn_prompt_body,
        grid=(batch, nb),
        in_specs=[
            pl.BlockSpec(memory_space=pltpu.SMEM),
            pl.BlockSpec((ATTN_BLOCK, QKV_W), lambda b, j: (b * nb + j, 0)),
            pl.BlockSpec((ATTN_BLOCK, LANES), lambda b, j: (j, 0)),
            pl.BlockSpec((ATTN_BLOCK, LANES), lambda b, j: (j, 0)),
            _const_spec((1, QK_W)),
            _const_spec((QK_W, QK_W)),
        ],
        out_specs=[
            pl.BlockSpec((ATTN_BLOCK, ATTN_Q), lambda b, j: (b * nb + j, 0)),
            pl.BlockSpec((None, WINDOW, ATTN_KV), lambda b, j: (b, 0, 0)),
            pl.BlockSpec((None, WINDOW, ATTN_KV), lambda b, j: (b, 0, 0)),
        ],
        out_shape=[
            jax.ShapeDtypeStruct((rows, ATTN_Q), F32),
            jax.ShapeDtypeStruct((batch, WINDOW, ATTN_KV), F32),
            jax.ShapeDtypeStruct((batch, WINDOW, ATTN_KV), F32),
        ],
        scratch_shapes=[pltpu.VMEM((ATTN_BLOCK, ATTN_KV), BF16),
                        pltpu.VMEM((ATTN_BLOCK, ATTN_KV), BF16)],
        compiler_params=pltpu.CompilerParams(
            dimension_semantics=("parallel", "arbitrary"), vmem_limit_bytes=VMEM_LIMIT),
        name="attn_prompt",
    )(sinks, qkv, cos, sin, nw, ones_bd)


SAMPLE_BB = 8


def _attn_sample_body(x_ref, kc_ref, vc_ref, cos_ref, sin_ref, nw_ref, ones_ref, sinks_ref,
                      oa_any, o_ref, kout_ref, vout_ref):
    del oa_any
    x = x_ref[...]
    qk = _norm_rope(x[:, :QK_W], ones_ref[...], nw_ref[...], cos_ref[...], sin_ref[...])
    knew = qk[:, ATTN_Q:QK_W]
    vnew = x[:, QK_W:QKV_W]
    scale = 1.0 / math.sqrt(HEAD_DIM)
    row = lax.broadcasted_iota(jnp.int32, (N_HEADS, LANES), 0)
    lane = lax.broadcasted_iota(jnp.int32, (N_HEADS, LANES), 1)
    own = (lane < HEAD_DIM) == ((row & 1) == 0)
    lo1 = lax.broadcasted_iota(jnp.int32, (1, LANES), 1) < HEAD_DIM
    sinks = sinks_ref[...]
    o_rows = []
    for b in range(SAMPLE_BB):
        qrows = [qk[b:b + 1, (i // 2) * LANES:(i // 2 + 1) * LANES] for i in range(N_HEADS)]
        qz = jnp.where(own, jnp.concatenate(qrows, axis=0) * scale, 0.0)
        kb = kc_ref[b]
        vb = vc_ref[b]
        s = _dot_nt(qz.astype(BF16), kb.astype(BF16))
        s_new = jnp.sum(qz * knew[b:b + 1, :], axis=1, keepdims=True)
        sink = sinks[:, 0:1]
        m = jnp.maximum(jnp.maximum(jnp.max(s, axis=1, keepdims=True), s_new), sink)
        p = jnp.exp(s - m)
        p_new = jnp.exp(s_new - m)
        denom = jnp.sum(p, axis=1, keepdims=True) + p_new + jnp.exp(sink - m)
        pv = (_dot(p.astype(BF16), vb.astype(BF16)) + p_new * vnew[b:b + 1, :]) / denom
        o_rows.append(jnp.concatenate(
            [jnp.where(lo1, pv[2 * g:2 * g + 1, :], pv[2 * g + 1:2 * g + 2, :]) for g in range(GROUP)],
            axis=1))
        kout_ref[b, 0:WINDOW - 1, :] = kc_ref[b, 1:WINDOW, :]
        kout_ref[b, WINDOW - 1:WINDOW, :] = knew[b:b + 1, :]
        vout_ref[b, 0:WINDOW - 1, :] = vc_ref[b, 1:WINDOW, :]
        vout_ref[b, WINDOW - 1:WINDOW, :] = vnew[b:b + 1, :]
    o_ref[...] = jnp.concatenate(o_rows, axis=0)


def _attn_sample(qkv, cache_k, cache_v, cos, sin, nw, ones_bd, sinks_b, oa, l, row0, dbatch):
    bb = SAMPLE_BB
    r0 = row0 // bb
    return pl.pallas_call(
        _attn_sample_body,
        grid=(dbatch // bb,),
        in_specs=[
            pl.BlockSpec((bb, QKV_W), lambda i: (r0 + i, 0)),
            pl.BlockSpec((None, bb, WINDOW, ATTN_KV), lambda i: (l, i, 0, 0)),
            pl.BlockSpec((None, bb, WINDOW, ATTN_KV), lambda i: (l, i, 0, 0)),
            _const_spec((1, LANES)),
            _const_spec((1, LANES)),
            _const_spec((1, QK_W)),
            _const_spec((QK_W, QK_W)),
            _const_spec((N_HEADS, LANES)),
            pl.BlockSpec(memory_space=pl.ANY),
        ],
        out_specs=[
            pl.BlockSpec((bb, ATTN_Q), lambda i: (r0 + i, 0)),
            pl.BlockSpec((bb, WINDOW, ATTN_KV), lambda i: (i, 0, 0)),
            pl.BlockSpec((bb, WINDOW, ATTN_KV), lambda i: (i, 0, 0)),
        ],
        out_shape=[
            jax.ShapeDtypeStruct(oa.shape, F32),
            jax.ShapeDtypeStruct((dbatch, WINDOW, ATTN_KV), F32),
            jax.ShapeDtypeStruct((dbatch, WINDOW, ATTN_KV), F32),
        ],
        input_output_aliases={8: 0},
        compiler_params=pltpu.CompilerParams(
            dimension_semantics=("parallel",), vmem_limit_bytes=VMEM_LIMIT),
        name="attn_sample",
    )(qkv, cache_k, cache_v, cos, sin, nw, ones_bd, sinks_b, oa)


def _softplus(x):
    return jnp.maximum(x, 0.0) + jnp.log(1.0 + jnp.exp(-jnp.abs(x)))


def _silu(x):
    return x * jax.nn.sigmoid(x)


def _l2n(x):
    return x * lax.rsqrt(jnp.sum(x * x, axis=-1, keepdims=True) + EPS)


def _gate_rows(ba, alog_row, dtb_row):
    beta = jax.nn.sigmoid(ba)
    g = -jnp.exp(alog_row) * _softplus(ba + dtb_row)
    return beta, g


def _gdn_prompt_body(raw_ref, z_ref, ba_ref, cw_ref, alog_ref, dtb_ref, onw_ref,
                     ltri_ref, lall_ref, lvl_ref,
                     od_ref, sout_ref, cout_ref, xp, s_scr):
    t = pl.program_id(1)
    nt = pl.num_programs(1)

    @pl.when(t == 0)
    def _():
        xp[0:SUBLANES, :] = jnp.zeros((SUBLANES, CONV_DIM), F32)
        s_scr[...] = jnp.zeros_like(s_scr)

    raw = raw_ref[...]
    xp[SUBLANES:SUBLANES + TILE, :] = raw
    y = raw * cw_ref[CONV_W - 1:CONV_W, :]
    for s in range(1, CONV_W):
        y = y + xp[SUBLANES - s:SUBLANES - s + TILE, :] * cw_ref[CONV_W - 1 - s:CONV_W - s, :]
    y = _silu(y)
    xp[0:SUBLANES, :] = raw[TILE - SUBLANES:TILE, :]

    @pl.when(t == nt - 1)
    def _():
        cout_ref[...] = raw[TILE - (CONV_W - 1):TILE, :]

    beta_f, g_f = _gate_rows(ba_ref[...], alog_ref[...], dtb_ref[...])
    ltri = ltri_ref[...]
    lall = lall_ref[...]
    gcum = _dot_exact_lhs(ltri, g_f)
    glast = _dot_exact_lhs(lall, g_f)
    gcum_t = gcum.T

    ri = lax.broadcasted_iota(jnp.int32, (TILE, TILE), 0)
    ci = lax.broadcasted_iota(jnp.int32, (TILE, TILE), 1)
    same = (ri // DN_CHUNK) == (ci // DN_CHUNK)
    causal = same & (ci <= ri)
    strict = same & (ci < ri)
    eye = (ri == ci).astype(F32)
    zeros_half = jnp.zeros((DN_CHUNK, DN_DV), F32)

    outs = []
    for h in range(DN_HEADS):
        q = _l2n(y[:, h * DN_DK:(h + 1) * DN_DK]) * (DN_DK ** -0.5)
        k = _l2n(y[:, DN_QK + h * DN_DK:DN_QK + (h + 1) * DN_DK])
        v = y[:, 2 * DN_QK + h * DN_DV:2 * DN_QK + (h + 1) * DN_DV]
        beta = beta_f[:, h:h + 1]
        gc = gcum[:, DN_HEADS + h:DN_HEADS + h + 1]
        gr = gcum_t[DN_HEADS + h:DN_HEADS + h + 1, :]
        gl = glast[:, DN_HEADS + h:DN_HEADS + h + 1]
        decay = jnp.exp(jnp.where(causal, gc - gr, -jnp.inf))
        kb = k.astype(BF16)
        kk = _dot_nt(kb, kb)
        a = jnp.where(strict, beta * kk * decay, 0.0)

        tinv = eye - a * lvl_ref[0]
        for lv in range(1, 6):
            tb = tinv.astype(BF16)
            aoff = (a * lvl_ref[lv]).astype(BF16)
            tinv = tinv - _dot(_dot(tb, aoff).astype(BF16), tb)

        eg = jnp.exp(gc)
        rhs = jnp.concatenate([v * beta, k * (beta * eg)], axis=1).astype(BF16)
        sol = _dot(tinv.astype(BF16), rhs)
        u_base = sol[:, :DN_DV]
        w = sol[:, DN_DV:].astype(BF16)
        qk = (_dot_nt(q.astype(BF16), kb) * decay).astype(BF16)
        qd = (q * eg).astype(BF16)
        kd_t = (k * jnp.exp(gl - gc)).T.astype(BF16)
        egl = jnp.exp(gl)

        st = s_scr[h]
        o_parts = []
        for c in range(TILE // DN_CHUNK):
            rows = slice(c * DN_CHUNK, (c + 1) * DN_CHUNK)
            sb = st.astype(BF16)
            u = u_base[rows] - _dot(w[rows], sb)
            parts = [zeros_half] * (TILE // DN_CHUNK)
            parts[c] = u
            u_full = jnp.concatenate(parts, axis=0).astype(BF16)
            o_parts.append(_dot(qd[rows], sb) + _dot(qk[rows], u_full))
            st = st * egl[c * DN_CHUNK:c * DN_CHUNK + 1, :] + _dot(kd_t, u_full)
        s_scr[h] = st
        o = jnp.concatenate(o_parts, axis=0)
        zh = z_ref[:, h * DN_DV:(h + 1) * DN_DV]
        outs.append(_rms(o, onw_ref[...]) * _silu(zh))
    od_ref[...] = jnp.concatenate(outs, axis=1)

    @pl.when(t == nt - 1)
    def _():
        sout_ref[...] = s_scr[...]


def _gdn_prompt(raw, z, ba, cw, alog_row, dtb_row, onw, consts, l, batch, seq, rows):
    nt = seq // TILE
    ltri, lall, lvl = consts
    return pl.pallas_call(
        _gdn_prompt_body,
        grid=(batch, nt),
        in_specs=[
            pl.BlockSpec((TILE, CONV_DIM), lambda b, t: (b * nt + t, 0)),
            pl.BlockSpec((TILE, DN_V), lambda b, t: (b * nt + t, 0)),
            pl.BlockSpec((TILE, LANES), lambda b, t: (b * nt + t, 0)),
            _layer_spec(l, (CONV_W, CONV_DIM)),
            _layer_spec(l, (1, LANES)),
            _layer_spec(l, (1, LANES)),
            _layer_spec(l, (1, DN_DV)),
            _const_spec((TILE, TILE)),
            _const_spec((TILE, TILE)),
            _const_spec((6, TILE, TILE)),
        ],
        out_specs=[
            pl.BlockSpec((TILE, DN_V), lambda b, t: (b * nt + t, 0)),
            pl.BlockSpec((None, DN_HEADS, DN_DK, DN_DV), lambda b, t: (b, 0, 0, 0)),
            pl.BlockSpec((None, CONV_W - 1, CONV_DIM), lambda b, t: (b, 0, 0)),
        ],
        out_shape=[
            jax.ShapeDtypeStruct((rows, DN_V), F32),
            jax.ShapeDtypeStruct((batch, DN_HEADS, DN_DK, DN_DV), F32),
            jax.ShapeDtypeStruct((batch, CONV_W - 1, CONV_DIM), F32),
        ],
        scratch_shapes=[pltpu.VMEM((SUBLANES + TILE, CONV_DIM), F32),
                        pltpu.VMEM((DN_HEADS, DN_DK, DN_DV), F32)],
        compiler_params=pltpu.CompilerParams(
            dimension_semantics=("parallel", "arbitrary"), vmem_limit_bytes=VMEM_LIMIT),
        name="gdn_prompt",
    )(raw, z, ba, cw, alog_row, dtb_row, onw, ltri, lall, lvl)


def _gdn_sample_body(raw_ref, z_ref, ba_ref, cs_ref, st_ref, cw_ref, alog_ref, dtb_ref, onw_ref,
                     od_any, od_ref, sout_ref, cout_ref):
    del od_any
    bb = SAMPLE_BB
    raw = raw_ref[...]
    y = raw * cw_ref[CONV_W - 1:CONV_W, :]
    for i in range(CONV_W - 1):
        y = y + cs_ref[:, i * CONV_DIM:(i + 1) * CONV_DIM] * cw_ref[i:i + 1, :]
    y = _silu(y)
    cout_ref[:, 0:(CONV_W - 2) * CONV_DIM] = cs_ref[:, CONV_DIM:(CONV_W - 1) * CONV_DIM]
    cout_ref[:, (CONV_W - 2) * CONV_DIM:] = raw

    beta_f, g_f = _gate_rows(ba_ref[...], alog_ref[...], dtb_ref[...])
    eg_f = jnp.exp(g_f)
    pad = jnp.zeros((LANES - bb, DN_DK), F32)
    outs = []
    for h in range(DN_HEADS):
        q = _l2n(y[:, h * DN_DK:(h + 1) * DN_DK]) * (DN_DK ** -0.5)
        k = _l2n(y[:, DN_QK + h * DN_DK:DN_QK + (h + 1) * DN_DK])
        v = y[:, 2 * DN_QK + h * DN_DV:2 * DN_QK + (h + 1) * DN_DV]
        k_t = jnp.concatenate([k, pad], axis=0).T
        qk = jnp.sum(q * k, axis=1, keepdims=True)
        o_rows = []
        for b in range(bb):
            s1 = st_ref[b, h] * eg_f[b:b + 1, DN_HEADS + h:DN_HEADS + h + 1]
            kq = jnp.concatenate([k[b:b + 1, :], q[b:b + 1, :]], axis=0).astype(BF16)
            r = _dot(kq, s1.astype(BF16))
            delta = beta_f[b:b + 1, h:h + 1] * (v[b:b + 1, :] - r[0:1, :])
            sout_ref[b, h] = s1 + k_t[:, b:b + 1] * delta
            o_rows.append(r[1:2, :] + qk[b:b + 1, :] * delta)
        o = jnp.concatenate(o_rows, axis=0)
        zh = z_ref[:, h * DN_DV:(h + 1) * DN_DV]
        outs.append(_rms(o, onw_ref[...]) * _silu(zh))
    od_ref[...] = jnp.concatenate(outs, axis=1)


def _gdn_sample(raw, z, ba, conv_state, dn_state, cw, alog_row, dtb_row, onw, od, l, row0, dbatch):
    bb = SAMPLE_BB
    r0 = row0 // bb
    cflat = (CONV_W - 1) * CONV_DIM
    return pl.pallas_call(
        _gdn_sample_body,
        grid=(dbatch // bb,),
        in_specs=[
            pl.BlockSpec((bb, CONV_DIM), lambda i: (r0 + i, 0)),
            pl.BlockSpec((bb, DN_V), lambda i: (r0 + i, 0)),
            pl.BlockSpec((bb, LANES), lambda i: (r0 + i, 0)),
            pl.BlockSpec((None, bb, cflat), lambda i: (l, i, 0)),
            pl.BlockSpec((None, bb, DN_HEADS, DN_DK, DN_DV), lambda i: (l, i, 0, 0, 0)),
            _layer_spec(l, (CONV_W, CONV_DIM)),
            _layer_spec(l, (1, LANES)),
            _layer_spec(l, (1, LANES)),
            _layer_spec(l, (1, DN_DV)),
            pl.BlockSpec(memory_space=pl.ANY),
        ],
        out_specs=[
            pl.BlockSpec((bb, DN_V), lambda i: (r0 + i, 0)),
            pl.BlockSpec((bb, DN_HEADS, DN_DK, DN_DV), lambda i: (i, 0, 0, 0)),
            pl.BlockSpec((bb, cflat), lambda i: (i, 0)),
        ],
        out_shape=[
            jax.ShapeDtypeStruct(od.shape, F32),
            jax.ShapeDtypeStruct((dbatch, DN_HEADS, DN_DK, DN_DV), F32),
            jax.ShapeDtypeStruct((dbatch, cflat), F32),
        ],
        input_output_aliases={9: 0},
        compiler_params=pltpu.CompilerParams(
            dimension_semantics=("parallel",), vmem_limit_bytes=VMEM_LIMIT),
        name="gdn_sample",
    )(raw, z, ba, conv_state, dn_state, cw, alog_row, dtb_row, onw, od)


def _np_consts():
    i = np.arange(TILE)[:, None]
    j = np.arange(TILE)[None, :]
    same = (i // DN_CHUNK) == (j // DN_CHUNK)
    ltri = (same & (j <= i)).astype(np.float32)
    lall = same.astype(np.float32)
    lvls = []
    b = 1
    while b < DN_CHUNK:
        lvls.append(((i // (2 * b)) == (j // (2 * b))) & ((i // b) != (j // b)))
        b *= 2
    lvl = np.stack(lvls).astype(np.float32)
    hi = np.arange(QK_W)
    ones_bd = (hi[:, None] // HEAD_DIM == hi[None, :] // HEAD_DIM).astype(np.float32)
    return ltri, lall, lvl, ones_bd


def _rope_tables(pos):
    half = HEAD_DIM // 2
    inv = 1.0 / (ROPE_THETA ** (jnp.arange(half, dtype=F32) / half))
    ang = pos.astype(F32)[:, None] * inv[None, :]
    cos, sin = jnp.cos(ang), jnp.sin(ang)
    cos_t = jnp.concatenate([cos, cos] * (LANES // HEAD_DIM), axis=1)
    sin_t = jnp.concatenate([-sin, sin] * (LANES // HEAD_DIM), axis=1)
    return cos_t, sin_t


def _pad_row(x, offset):
    out = jnp.zeros((x.shape[0], 1, LANES), F32)
    return out.at[:, 0, offset:offset + x.shape[1]].set(x.astype(F32))


def kernel(x_prompt, x_sample, cache_swa_k, cache_swa_v, state_dn, state_conv, ffn1_norm, ffn1_w_gate_up, ffn1_w_down, mix_norm, w_in, q_norm, k_norm, attn_sinks, conv_w, dn_A_log, dn_dt_bias, dn_out_norm, w_attn_o, w_dn_o, w_out, ffn2_norm, ffn2_w_gate_up, ffn2_w_down):
    batch, seq, _ = x_prompt.shape
    dbatch = x_sample.shape[0]
    prows = batch * seq
    rows = prows + dbatch
    perm = np.asarray(HEAD_PERM)

    o0 = 0
    segs = []
    for wdt in (ATTN_Q, ATTN_KV, ATTN_KV, CONV_DIM, DN_V, DN_HEADS, DN_HEADS, D_MODEL, D_MODEL):
        segs.append(w_in[:, :, o0:o0 + wdt])
        o0 += wdt
    wq, wk, wv, wraw, wz, wb, wa, wga, wgd = segs
    wq = wq.reshape(DEPTH, D_MODEL, N_HEADS, HEAD_DIM)[:, :, perm].reshape(DEPTH, D_MODEL, ATTN_Q)
    wpad = jnp.zeros((DEPTH, D_MODEL, LANES - 2 * DN_HEADS), w_in.dtype)
    w_in_r = jnp.concatenate([wq, wk, wv, wraw, wz, wb, wa, wpad, wga, wgd], axis=2).astype(BF16)
    wao = w_attn_o.reshape(DEPTH, N_HEADS, HEAD_DIM, D_MODEL)[:, perm].reshape(DEPTH, ATTN_Q, D_MODEL).astype(BF16)
    wdo = w_dn_o.astype(BF16)
    wout = w_out.astype(BF16)
    wgu1, wd1 = ffn1_w_gate_up.astype(BF16), ffn1_w_down.astype(BF16)
    wgu2, wd2 = ffn2_w_gate_up.astype(BF16), ffn2_w_down.astype(BF16)
    n1 = ffn1_norm.reshape(DEPTH, 1, D_MODEL)
    n2 = ffn2_norm.reshape(DEPTH, 1, D_MODEL)
    nm = mix_norm.reshape(DEPTH, 1, D_MODEL)
    qk_nw = jnp.concatenate([jnp.tile(q_norm, (1, N_HEADS)), jnp.tile(k_norm, (1, N_KV_HEADS))], axis=1)
    alog_row = _pad_row(dn_A_log, DN_HEADS)
    dtb_row = _pad_row(dn_dt_bias, DN_HEADS)
    onw = dn_out_norm.reshape(DEPTH, 1, DN_DV)
    sinks_perm = attn_sinks[:, perm]
    sinks_b = jnp.broadcast_to(sinks_perm[:, :, None], (DEPTH, N_HEADS, LANES))

    ltri, lall, lvl, ones_bd = _np_consts()
    gdn_consts = (jnp.asarray(ltri, BF16), jnp.asarray(lall, BF16), jnp.asarray(lvl, F32))
    ones_bd = jnp.asarray(ones_bd, BF16)
    cos_p, sin_p = _rope_tables(jnp.arange(seq))
    cos_s, sin_s = _rope_tables(PAST_LEN + jnp.arange(1))

    ck = cache_swa_k.reshape(DEPTH, dbatch, WINDOW, ATTN_KV)
    cv = cache_swa_v.reshape(DEPTH, dbatch, WINDOW, ATTN_KV)
    cs = state_conv.reshape(DEPTH, dbatch, (CONV_W - 1) * CONV_DIM)

    h = jnp.concatenate([x_prompt.reshape(prows, D_MODEL), x_sample.reshape(dbatch, D_MODEL)], axis=0)
    kp_l, vp_l, sp_l, cp_l, ks_l, vs_l, ss_l, cs_l = ([] for _ in range(8))
    for l in range(DEPTH):
        h = _ffn(h, n1, wgu1, wd1, l)
        qkv, raw, z, ba, gates = _inproj(h, nm, w_in_r, l)
        oa, kp, vp = _attn_prompt(qkv, attn_sinks[l], cos_p, sin_p, qk_nw[l:l + 1], ones_bd,
                                  batch, seq, rows)
        oa, ksn, vsn = _attn_sample(qkv, ck, cv, cos_s, sin_s, qk_nw[l:l + 1], ones_bd,
                                    sinks_b[l], oa, l, prows, dbatch)
        od, sp, cp = _gdn_prompt(raw, z, ba, conv_w, alog_row, dtb_row, onw, gdn_consts,
                                 l, batch, seq, rows)
        od, ssn, csn = _gdn_sample(raw, z, ba, cs, state_dn, conv_w, alog_row, dtb_row, onw,
                                   od, l, prows, dbatch)
        h = _merge(h, oa, od, gates, wao, wdo, wout, l)
        h = _ffn(h, n2, wgu2, wd2, l)
        kp_l.append(kp); vp_l.append(vp); sp_l.append(sp); cp_l.append(cp)
        ks_l.append(ksn); vs_l.append(vsn); ss_l.append(ssn); cs_l.append(csn)

    kv_shape = (DEPTH, -1, WINDOW, N_KV_HEADS, HEAD_DIM)
    return (h[:prows].reshape(batch, seq, D_MODEL),
            h[prows:].reshape(dbatch, 1, D_MODEL),
            jnp.stack(kp_l).reshape(kv_shape), jnp.stack(vp_l).reshape(kv_shape),
            jnp.stack(sp_l), jnp.stack(cp_l),
            jnp.stack(ks_l).reshape(kv_shape), jnp.stack(vs_l).reshape(kv_shape),
            jnp.stack(ss_l),
            jnp.stack(cs_l).reshape(DEPTH, dbatch, CONV_W - 1, CONV_DIM))
```

```python
import functools
import math

import numpy as np
import jax
import jax.numpy as jnp
from jax import lax
from jax.experimental import pallas as pl
from jax.experimental.pallas import tpu as pltpu

F32 = jnp.float32
BF16 = jnp.bfloat16

D_MODEL = 1024
DEPTH = 4
PAST_LEN = 8192
N_HEADS = 8
N_KV_HEADS = 2
GROUP = N_HEADS // N_KV_HEADS
HEAD_DIM = 64
WINDOW = 128
ATTN_BLOCK = 128
ROPE_THETA = 10000.0
DN_HEADS = 4
DN_DK = 128
DN_DV = 128
CONV_W = 4
DN_CHUNK = 64
D_FF = 2816
EPS = 1e-6

ATTN_Q = N_HEADS * HEAD_DIM
ATTN_KV = N_KV_HEADS * HEAD_DIM
DN_QK = DN_HEADS * DN_DK
DN_V = DN_HEADS * DN_DV
CONV_DIM = 2 * DN_QK + DN_V
QKV_W = ATTN_Q + 2 * ATTN_KV
QK_W = ATTN_Q + ATTN_KV
LANES = 128
SUBLANES = 8
TILE = 128
VMEM_LIMIT = 56 * 1024 * 1024

HEAD_PERM = (0, 4, 1, 5, 2, 6, 3, 7)
FF_SPLIT = 2


def _row_tile(rows, cap=512):
    best = SUBLANES
    for t in range(SUBLANES, cap + 1, SUBLANES):
        if rows % t == 0:
            best = t
    return best


def _rms(x, w):
    return x * lax.rsqrt(jnp.mean(x * x, axis=-1, keepdims=True) + EPS) * w


def _dot(a, b):
    return jnp.dot(a, b, preferred_element_type=F32)


def _dot_nt(a, b):
    return lax.dot_general(a, b, (((1,), (1,)), ((), ())), preferred_element_type=F32)


def _split2(x):
    hi = x.astype(BF16)
    return hi, (x - hi.astype(F32)).astype(BF16)


def _split3(x):
    hi = x.astype(BF16)
    r = x - hi.astype(F32)
    mid = r.astype(BF16)
    lo = (r - mid.astype(F32)).astype(BF16)
    return hi, mid, lo


def _dot_exact_lhs(m, x):
    hi, mid, lo = _split3(x)
    return _dot(m, hi) + _dot(m, mid) + _dot(m, lo)


def _const_spec(shape):
    nd = len(shape)
    return pl.BlockSpec(shape, lambda *_: (0,) * nd, pipeline_mode=pl.Buffered(1))


def _layer_spec(l, shape):
    nd = len(shape)
    return pl.BlockSpec((None,) + shape, lambda *_: (l,) + (0,) * nd, pipeline_mode=pl.Buffered(1))


def _stacked_out(l, prev, shape, block, index_map):
    spec = pl.BlockSpec((None,) + block, lambda *idx: (l,) + tuple(index_map(*idx)))
    sds = jax.ShapeDtypeStruct((DEPTH,) + shape, F32)
    if l == 0:
        return spec, sds, [], []
    return spec, sds, [prev], [pl.BlockSpec(memory_space=pl.ANY)]


def _swiglu_residual(x, nw, wgu_ref, wd_ref):
    xn = _rms(x, nw).astype(BF16)
    tf = D_FF // FF_SPLIT
    acc = jnp.zeros_like(x)
    for c in range(FF_SPLIT):
        g = _dot(xn, wgu_ref[:, c * tf:(c + 1) * tf])
        u = _dot(xn, wgu_ref[:, D_FF + c * tf:D_FF + (c + 1) * tf])
        a = (g * jax.nn.sigmoid(g) * u).astype(BF16)
        acc = acc + _dot(a, wd_ref[c * tf:(c + 1) * tf, :])
    return x + 0.5 * acc


def _ffn_body(x_ref, nw_ref, wgu_ref, wd_ref, o_ref):
    o_ref[...] = _swiglu_residual(x_ref[...], nw_ref[...], wgu_ref, wd_ref)


def _ffn(x, norm_w, wgu, wd, l):
    rows = x.shape[0]
    tm = _row_tile(rows)
    return pl.pallas_call(
        _ffn_body,
        grid=(rows // tm,),
        in_specs=[
            pl.BlockSpec((tm, D_MODEL), lambda i: (i, 0)),
            _layer_spec(l, (1, D_MODEL)),
            _layer_spec(l, (D_MODEL, 2 * D_FF)),
            _layer_spec(l, (D_FF, D_MODEL)),
        ],
        out_specs=pl.BlockSpec((tm, D_MODEL), lambda i: (i, 0)),
        out_shape=jax.ShapeDtypeStruct((rows, D_MODEL), F32),
        compiler_params=pltpu.CompilerParams(
            dimension_semantics=("parallel",), vmem_limit_bytes=VMEM_LIMIT),
        name="ffn",
    )(x, norm_w, wgu, wd)


IN_SEGS = (QKV_W, CONV_DIM, DN_V, LANES, 2 * D_MODEL)
IN_DTYPES = (F32, F32, F32, F32, BF16)
IN_COLS_PAD = sum(IN_SEGS)


def _inproj_body(h_ref, nw_ref, w_ref, qkv_ref, raw_ref, z_ref, ba_ref, gates_ref):
    u = _rms(h_ref[...], nw_ref[...]).astype(BF16)
    off = 0
    for ref, width in zip((qkv_ref, raw_ref, z_ref, ba_ref, gates_ref), IN_SEGS):
        ref[...] = _dot(u, w_ref[:, off:off + width]).astype(ref.dtype)
        off += width


def _inproj(h, norm_w, w_in, l):
    rows = h.shape[0]
    tm = _row_tile(rows)
    return pl.pallas_call(
        _inproj_body,
        grid=(rows // tm,),
        in_specs=[
            pl.BlockSpec((tm, D_MODEL), lambda i: (i, 0)),
            _layer_spec(l, (1, D_MODEL)),
            _layer_spec(l, (D_MODEL, IN_COLS_PAD)),
        ],
        out_specs=[pl.BlockSpec((tm, w), lambda i: (i, 0)) for w in IN_SEGS],
        out_shape=[jax.ShapeDtypeStruct((rows, w), dt) for w, dt in zip(IN_SEGS, IN_DTYPES)],
        compiler_params=pltpu.CompilerParams(
            dimension_semantics=("parallel",), vmem_limit_bytes=VMEM_LIMIT),
        name="inproj",
    )(h, norm_w, w_in)


def _merge_ffn_body(h_ref, oa_ref, od_ref, gates_ref, wao_ref, wdo_ref, wout_ref,
                    nw_ref, wgu_ref, wd_ref, o_ref):
    br_a = _dot(oa_ref[...], wao_ref[...])
    br_d = _dot(od_ref[...], wdo_ref[...])
    ga = gates_ref[:, :D_MODEL].astype(F32)
    gd = gates_ref[:, D_MODEL:].astype(F32)
    m = jax.nn.sigmoid(ga) * br_a + jax.nn.sigmoid(gd) * br_d
    h = h_ref[...] + _dot(m.astype(BF16), wout_ref[...])
    o_ref[...] = _swiglu_residual(h, nw_ref[...], wgu_ref, wd_ref)


def _merge_ffn(h, oa, od, gates, wao, wdo, wout, norm_w, wgu, wd, l):
    rows = h.shape[0]
    tm = _row_tile(rows)
    return pl.pallas_call(
        _merge_ffn_body,
        grid=(rows // tm,),
        in_specs=[
            pl.BlockSpec((tm, D_MODEL), lambda i: (i, 0)),
            pl.BlockSpec((tm, ATTN_Q), lambda i: (i, 0)),
            pl.BlockSpec((tm, DN_V), lambda i: (i, 0)),
            pl.BlockSpec((tm, 2 * D_MODEL), lambda i: (i, 0)),
            _layer_spec(l, (ATTN_Q, D_MODEL)),
            _layer_spec(l, (DN_V, D_MODEL)),
            _layer_spec(l, (D_MODEL, D_MODEL)),
            _layer_spec(l, (1, D_MODEL)),
            _layer_spec(l, (D_MODEL, 2 * D_FF)),
            _layer_spec(l, (D_FF, D_MODEL)),
        ],
        out_specs=pl.BlockSpec((tm, D_MODEL), lambda i: (i, 0)),
        out_shape=jax.ShapeDtypeStruct((rows, D_MODEL), F32),
        compiler_params=pltpu.CompilerParams(
            dimension_semantics=("parallel",), vmem_limit_bytes=VMEM_LIMIT),
        name="merge_ffn",
    )(h, oa, od, gates, wao, wdo, wout, norm_w, wgu, wd)


def _norm_rope(x, ones_bd, nw, cos, sin):
    lane = lax.broadcasted_iota(jnp.int32, (x.shape[0], LANES), 1)
    first_half = (lane & (HEAD_DIM // 2)) == 0
    outs = []
    for g in range(x.shape[1] // LANES):
        xg = x[:, g * LANES:(g + 1) * LANES]
        hi, lo = _split2(xg * xg)
        ssq = _dot(hi, ones_bd) + _dot(lo, ones_bd)
        xn = xg * lax.rsqrt(ssq * (1.0 / HEAD_DIM) + EPS) * nw[:, g * LANES:(g + 1) * LANES]
        partner = jnp.where(first_half,
                            pltpu.roll(xn, LANES - HEAD_DIM // 2, 1),
                            pltpu.roll(xn, HEAD_DIM // 2, 1))
        outs.append(xn * cos + partner * sin)
    return jnp.concatenate(outs, axis=1)


def _attn_prompt_body(sinks_ref, x_ref, cos_ref, sin_ref, nw_ref, ones_ref, *rest):
    o_ref, kout_ref, vout_ref, kprev, vprev = rest[-5:]
    blk = pl.program_id(1)

    @pl.when(blk == 0)
    def _():
        kprev[...] = jnp.zeros_like(kprev)
        vprev[...] = jnp.zeros_like(vprev)

    x = x_ref[...]
    qk = _norm_rope(x[:, :QK_W], ones_ref[...], nw_ref[...], cos_ref[...], sin_ref[...])
    kcur = qk[:, ATTN_Q:QK_W]
    vcur = x[:, QK_W:QKV_W]
    kc = jnp.concatenate([kprev[...], kcur.astype(BF16)], axis=0)
    vc = jnp.concatenate([vprev[...], vcur.astype(BF16)], axis=0)

    r = lax.broadcasted_iota(jnp.int32, (ATTN_BLOCK, 2 * ATTN_BLOCK), 0)
    c = lax.broadcasted_iota(jnp.int32, (ATTN_BLOCK, 2 * ATTN_BLOCK), 1)
    first_col = jnp.where(blk == 0, ATTN_BLOCK, 0)
    vis = (c >= r) & (c <= r + WINDOW) & (c >= first_col)
    lo = lax.broadcasted_iota(jnp.int32, (ATTN_BLOCK, LANES), 1) < HEAD_DIM

    scale = 1.0 / math.sqrt(HEAD_DIM)
    outs = []
    for g in range(GROUP):
        qg = qk[:, g * LANES:(g + 1) * LANES] * scale
        res = []
        for half in range(2):
            head = HEAD_PERM[2 * g + half]
            qm = jnp.where(lo if half == 0 else jnp.logical_not(lo), qg, 0.0).astype(BF16)
            s = jnp.where(vis, _dot_nt(qm, kc), -jnp.inf)
            sink = sinks_ref[head]
            m = jnp.maximum(jnp.max(s, axis=1, keepdims=True), sink)
            p = jnp.exp(s - m)
            denom = jnp.sum(p, axis=1, keepdims=True) + jnp.exp(sink - m)
            res.append(_dot(p.astype(BF16), vc) / denom)
        outs.append(jnp.where(lo, res[0], res[1]))
    o_ref[...] = jnp.concatenate(outs, axis=1).astype(o_ref.dtype)

    kprev[...] = kcur.astype(BF16)
    vprev[...] = vcur.astype(BF16)

    @pl.when(blk == pl.num_programs(1) - 1)
    def _():
        kout_ref[...] = kcur
        vout_ref[...] = vcur


def _attn_prompt(qkv, sinks, cos, sin, nw, ones_bd, kstack, vstack, l, batch, seq, rows):
    nb = seq // ATTN_BLOCK
    kv_block = (None, WINDOW, ATTN_KV)
    kv_idx = lambda b, j: (b, 0, 0)
    k_spec, k_sds, k_in, k_in_spec = _stacked_out(l, kstack, (batch, WINDOW, ATTN_KV), kv_block, kv_idx)
    v_spec, v_sds, v_in, v_in_spec = _stacked_out(l, vstack, (batch, WINDOW, ATTN_KV), kv_block, kv_idx)
    n_in = 6
    return pl.pallas_call(
        _attn_prompt_body,
        grid=(batch, nb),
        in_specs=[
            pl.BlockSpec(memory_space=pltpu.SMEM),
            pl.BlockSpec((ATTN_BLOCK, QKV_W), lambda b, j: (b * nb + j, 0)),
            pl.BlockSpec((ATTN_BLOCK, LANES), lambda b, j: (j, 0)),
            pl.BlockSpec((ATTN_BLOCK, LANES), lambda b, j: (j, 0)),
            _const_spec((1, QK_W)),
            _const_spec((LANES, LANES)),
        ] + k_in_spec + v_in_spec,
        out_specs=[pl.BlockSpec((ATTN_BLOCK, ATTN_Q), lambda b, j: (b * nb + j, 0)), k_spec, v_spec],
        out_shape=[jax.ShapeDtypeStruct((rows, ATTN_Q), BF16), k_sds, v_sds],
        input_output_aliases={n_in: 1, n_in + 1: 2} if l else {},
        scratch_shapes=[pltpu.VMEM((ATTN_BLOCK, ATTN_KV), BF16),
                        pltpu.VMEM((ATTN_BLOCK, ATTN_KV), BF16)],
        compiler_params=pltpu.CompilerParams(
            dimension_semantics=("parallel", "arbitrary"), vmem_limit_bytes=VMEM_LIMIT),
        name="attn_prompt",
    )(sinks, qkv, cos, sin, nw, ones_bd, *k_in, *v_in)


SAMPLE_BB = 16


def _attn_sample_body(x_ref, kc_ref, vc_ref, cos_ref, sin_ref, nw_ref, ones_ref, sinks_ref, *rest):
    o_ref, kout_ref, vout_ref = rest[-3:]
    x = x_ref[...]
    qk = _norm_rope(x[:, :QK_W], ones_ref[...], nw_ref[...], cos_ref[...], sin_ref[...])
    knew = qk[:, ATTN_Q:QK_W]
    vnew = x[:, QK_W:QKV_W]
    scale = 1.0 / math.sqrt(HEAD_DIM)
    row = lax.broadcasted_iota(jnp.int32, (N_HEADS, LANES), 0)
    lane = lax.broadcasted_iota(jnp.int32, (N_HEADS, LANES), 1)
    own = (lane < HEAD_DIM) == ((row & 1) == 0)
    lo1 = lax.broadcasted_iota(jnp.int32, (1, LANES), 1) < HEAD_DIM
    sinks = sinks_ref[...]
    o_rows = []
    for b in range(SAMPLE_BB):
        qrows = [qk[b:b + 1, (i // 2) * LANES:(i // 2 + 1) * LANES] for i in range(N_HEADS)]
        qz = jnp.where(own, jnp.concatenate(qrows, axis=0) * scale, 0.0)
        kb = kc_ref[b]
        vb = vc_ref[b]
        s = _dot_nt(qz.astype(BF16), kb.astype(BF16))
        s_new = jnp.sum(qz * knew[b:b + 1, :], axis=1, keepdims=True)
        sink = sinks[:, 0:1]
        m = jnp.maximum(jnp.maximum(jnp.max(s, axis=1, keepdims=True), s_new), sink)
        p = jnp.exp(s - m)
        p_new = jnp.exp(s_new - m)
        denom = jnp.sum(p, axis=1, keepdims=True) + p_new + jnp.exp(sink - m)
        pv = (_dot(p.astype(BF16), vb.astype(BF16)) + p_new * vnew[b:b + 1, :]) / denom
        o_rows.append(jnp.concatenate(
            [jnp.where(lo1, pv[2 * g:2 * g + 1, :], pv[2 * g + 1:2 * g + 2, :]) for g in range(GROUP)],
            axis=1))
        kout_ref[b, 0:WINDOW - 1, :] = kc_ref[b, 1:WINDOW, :]
        kout_ref[b, WINDOW - 1:WINDOW, :] = knew[b:b + 1, :]
        vout_ref[b, 0:WINDOW - 1, :] = vc_ref[b, 1:WINDOW, :]
        vout_ref[b, WINDOW - 1:WINDOW, :] = vnew[b:b + 1, :]
    o_ref[...] = jnp.concatenate(o_rows, axis=0).astype(o_ref.dtype)


def _attn_sample(qkv, cache_k, cache_v, cos, sin, nw, ones_bd, sinks_b, oa, kstack, vstack,
                 l, row0, dbatch):
    bb = SAMPLE_BB
    r0 = row0 // bb
    kv_block = (bb, WINDOW, ATTN_KV)
    kv_idx = lambda i: (i, 0, 0)
    k_spec, k_sds, k_in, k_in_spec = _stacked_out(l, kstack, (dbatch, WINDOW, ATTN_KV), kv_block, kv_idx)
    v_spec, v_sds, v_in, v_in_spec = _stacked_out(l, vstack, (dbatch, WINDOW, ATTN_KV), kv_block, kv_idx)
    n_in = 9
    return pl.pallas_call(
        _attn_sample_body,
        grid=(dbatch // bb,),
        in_specs=[
            pl.BlockSpec((bb, QKV_W), lambda i: (r0 + i, 0)),
            pl.BlockSpec((None, bb, WINDOW, ATTN_KV), lambda i: (l, i, 0, 0)),
            pl.BlockSpec((None, bb, WINDOW, ATTN_KV), lambda i: (l, i, 0, 0)),
            _const_spec((1, LANES)),
            _const_spec((1, LANES)),
            _const_spec((1, QK_W)),
            _const_spec((LANES, LANES)),
            _const_spec((N_HEADS, LANES)),
            pl.BlockSpec(memory_space=pl.ANY),
        ] + k_in_spec + v_in_spec,
        out_specs=[pl.BlockSpec((bb, ATTN_Q), lambda i: (r0 + i, 0)), k_spec, v_spec],
        out_shape=[jax.ShapeDtypeStruct(oa.shape, oa.dtype), k_sds, v_sds],
        input_output_aliases={8: 0, n_in: 1, n_in + 1: 2} if l else {8: 0},
        compiler_params=pltpu.CompilerParams(
            dimension_semantics=("parallel",), vmem_limit_bytes=VMEM_LIMIT),
        name="attn_sample",
    )(qkv, cache_k, cache_v, cos, sin, nw, ones_bd, sinks_b, oa, *k_in, *v_in)


def _softplus(x):
    return jnp.maximum(x, 0.0) + jnp.log(1.0 + jnp.exp(-jnp.abs(x)))


def _silu(x):
    return x * jax.nn.sigmoid(x)


def _l2n(x):
    return x * lax.rsqrt(jnp.sum(x * x, axis=-1, keepdims=True) + EPS)


def _gate_rows(ba, alog_row, dtb_row):
    beta = jax.nn.sigmoid(ba)
    g = -jnp.exp(alog_row) * _softplus(ba + dtb_row)
    return beta, g


GROUP_TILES = 4
GROUP_TOK = GROUP_TILES * TILE
CHUNKS = TILE // DN_CHUNK


def _gdn_prompt_body(raw_ref, z_ref, ba_ref, cw_ref, alog_ref, dtb_ref, onw_ref,
                     ltri_ref, lall_ref, lvl_ref, *rest, groups_per_seq):
    od_ref, sout_ref, cout_ref, xp, s_scr, ub_scr, wq_scr, kq_scr, egl_scr = rest[-9:]
    i = pl.program_id(0)
    n_groups = pl.num_programs(0) - 1
    ga = jnp.minimum(i, n_groups - 1)
    gb = jnp.maximum(i - 1, 0)
    slot_a = i % 2
    slot_b = 1 - slot_a

    @pl.when(i == 0)
    def _():
        ub_scr[1] = jnp.zeros(ub_scr.shape[1:], F32)
        wq_scr[1] = jnp.zeros(wq_scr.shape[1:], BF16)
        kq_scr[1] = jnp.zeros(kq_scr.shape[1:], BF16)
        egl_scr[1] = jnp.zeros(egl_scr.shape[1:], F32)

    @pl.when(ga % groups_per_seq == 0)
    def _():
        xp[0:SUBLANES, :] = jnp.zeros((SUBLANES, CONV_DIM), F32)

    @pl.when(gb % groups_per_seq == 0)
    def _():
        s_scr[...] = jnp.zeros_like(s_scr)

    @pl.when(ga % groups_per_seq == groups_per_seq - 1)
    def _():
        cout_ref[...] = raw_ref[GROUP_TOK - (CONV_W - 1):GROUP_TOK, :]

    ltri = ltri_ref[...]
    lall = lall_ref[...]
    ri = lax.broadcasted_iota(jnp.int32, (TILE, TILE), 0)
    ci = lax.broadcasted_iota(jnp.int32, (TILE, TILE), 1)
    same = (ri // DN_CHUNK) == (ci // DN_CHUNK)
    causal = same & (ci <= ri)
    strict = same & (ci < ri)
    eye = (ri == ci).astype(F32)
    zeros_half = jnp.zeros((DN_CHUNK, DN_DV), F32)
    onw = onw_ref[...]
    units = [(j, h) for j in range(GROUP_TILES) for h in range(DN_HEADS)]

    st = [s_scr[h] for h in range(DN_HEADS)]
    b_live = {}

    def b_stage1(j, c):
        for h in range(DN_HEADS):
            idx = j * DN_HEADS + h
            r1 = _dot(wq_scr[slot_b, idx, 2 * c * DN_CHUNK:(2 * c + 2) * DN_CHUNK, :], st[h].astype(BF16))
            u = ub_scr[slot_b, idx, c * DN_CHUNK:(c + 1) * DN_CHUNK, :] - r1[:DN_CHUNK]
            parts = [zeros_half] * CHUNKS
            parts[c] = u
            b_live[h] = (jnp.concatenate(parts, axis=0).astype(BF16), r1[DN_CHUNK:])

    def b_stage2(j, c):
        egl_f = egl_scr[slot_b, j]
        for h in range(DN_HEADS):
            idx = j * DN_HEADS + h
            u_full, oq = b_live[h]
            base = c * (DN_DK + DN_CHUNK)
            r2 = _dot(kq_scr[slot_b, idx, base:base + DN_DK + DN_CHUNK, :], u_full)
            o = oq + r2[DN_DK:]
            zh = z_ref[j * TILE + c * DN_CHUNK:j * TILE + (c + 1) * DN_CHUNK, h * DN_DV:(h + 1) * DN_DV]
            od_ref[j * TILE + c * DN_CHUNK:j * TILE + (c + 1) * DN_CHUNK, h * DN_DV:(h + 1) * DN_DV] = (
                _rms(o, onw) * _silu(zh)).astype(od_ref.dtype)
            st[h] = (st[h] * egl_f[c * DN_CHUNK:c * DN_CHUNK + 1, DN_HEADS + h:DN_HEADS + h + 1]
                     + r2[:DN_DK])

    b_stages = []
    for j in range(GROUP_TILES):
        for c in range(CHUNKS):
            b_stages.append(functools.partial(b_stage1, j, c))
            b_stages.append(functools.partial(b_stage2, j, c))
    b_iter = iter(b_stages)

    def b_step():
        f = next(b_iter, None)
        if f is not None:
            f()

    xp[SUBLANES:SUBLANES + GROUP_TOK, :] = raw_ref[...]
    ys = []
    for j in range(GROUP_TILES):
        r0 = SUBLANES + j * TILE
        y = xp[r0:r0 + TILE, :] * cw_ref[CONV_W - 1:CONV_W, :]
        for s in range(1, CONV_W):
            y = y + xp[r0 - s:r0 - s + TILE, :] * cw_ref[CONV_W - 1 - s:CONV_W - s, :]
        ys.append(_silu(y))
        b_step()
    xp[0:SUBLANES, :] = raw_ref[GROUP_TOK - SUBLANES:GROUP_TOK, :]

    tiles = []
    for j in range(GROUP_TILES):
        beta_f, g_f = _gate_rows(ba_ref[j * TILE:(j + 1) * TILE, :], alog_ref[...], dtb_ref[...])
        gcum = _dot_exact_lhs(ltri, g_f)
        glast = _dot_exact_lhs(lall, g_f)
        egl_scr[slot_a, j] = jnp.exp(glast)
        tiles.append((beta_f, gcum, glast, gcum.T))

    u = {}
    for (j, h) in units:
        beta_f, gcum, glast, gcum_t = tiles[j]
        y = ys[j]
        k = _l2n(y[:, DN_QK + h * DN_DK:DN_QK + (h + 1) * DN_DK])
        gc = gcum[:, DN_HEADS + h:DN_HEADS + h + 1]
        gr = gcum_t[DN_HEADS + h:DN_HEADS + h + 1, :]
        u[j, h] = dict(k=k, kb=k.astype(BF16), beta=beta_f[:, h:h + 1], gc=gc,
                       gl=glast[:, DN_HEADS + h:DN_HEADS + h + 1],
                       decay=jnp.exp(jnp.where(causal, gc - gr, -jnp.inf)))
    for un in units:
        d = u[un]
        d["kk"] = _dot_nt(d["kb"], d["kb"])
    b_step()
    for un in units:
        d = u[un]
        a = jnp.where(strict, d["beta"] * d.pop("kk") * d["decay"], 0.0)
        d["ab"] = a.astype(BF16)
        d["tinv"] = eye - a * lvl_ref[0].astype(F32)
    for lv in range(1, 6):
        for un in units:
            d = u[un]
            d["tb"] = d["tinv"].astype(BF16)
            d["p"] = _dot(d["tb"], d["ab"] * lvl_ref[lv]).astype(BF16)
        b_step()
        for un in units:
            d = u[un]
            d["tinv"] = d["tinv"] - _dot(d.pop("p"), d.pop("tb"))
        b_step()
    for (j, h) in units:
        d = u[j, h]
        y = ys[j]
        v = y[:, 2 * DN_QK + h * DN_DV:2 * DN_QK + (h + 1) * DN_DV]
        d["eg"] = jnp.exp(d["gc"])
        rhs = jnp.concatenate([v * d["beta"], d["k"] * (d["beta"] * d["eg"])], axis=1).astype(BF16)
        d["sol"] = _dot(d.pop("tinv").astype(BF16), rhs)
    b_step()
    for (j, h) in units:
        d = u[j, h]
        y = ys[j]
        q = _l2n(y[:, h * DN_DK:(h + 1) * DN_DK]) * (DN_DK ** -0.5)
        d["qk"] = (_dot_nt(q.astype(BF16), d["kb"]) * d["decay"]).astype(BF16)
        d["qd"] = (q * d["eg"]).astype(BF16)
    b_step()
    for _ in range(len(b_stages)):
        b_step()
    for (j, h) in units:
        d = u[j, h]
        idx = j * DN_HEADS + h
        sol = d["sol"]
        ub_scr[slot_a, idx] = sol[:, :DN_DV]
        w = sol[:, DN_DV:].astype(BF16)
        kd_t = (d["k"] * jnp.exp(d["gl"] - d["gc"])).T.astype(BF16)
        for c in range(CHUNKS):
            rows = slice(c * DN_CHUNK, (c + 1) * DN_CHUNK)
            wq_scr[slot_a, idx, 2 * c * DN_CHUNK:(2 * c + 1) * DN_CHUNK, :] = w[rows]
            wq_scr[slot_a, idx, (2 * c + 1) * DN_CHUNK:(2 * c + 2) * DN_CHUNK, :] = d["qd"][rows]
            base = c * (DN_DK + DN_CHUNK)
            kq_scr[slot_a, idx, base:base + DN_DK, :] = kd_t
            kq_scr[slot_a, idx, base + DN_DK:base + DN_DK + DN_CHUNK, :] = d["qk"][rows]

    for h in range(DN_HEADS):
        s_scr[h] = st[h]

    @pl.when(gb % groups_per_seq == groups_per_seq - 1)
    def _():
        for h in range(DN_HEADS):
            sout_ref[h] = st[h]


def _gdn_prompt(raw, z, ba, cw, alog_row, dtb_row, onw, consts, sstack, cstack, l, batch, seq, rows):
    gps = seq // GROUP_TOK
    ng = batch * gps
    s_spec, s_sds, s_in, s_in_spec = _stacked_out(
        l, sstack, (batch, DN_HEADS, DN_DK, DN_DV), (None, DN_HEADS, DN_DK, DN_DV),
        lambda i: (jnp.maximum(i - 1, 0) // gps, 0, 0, 0))
    c_spec, c_sds, c_in, c_in_spec = _stacked_out(
        l, cstack, (batch, CONV_W - 1, CONV_DIM), (None, CONV_W - 1, CONV_DIM),
        lambda i: (jnp.minimum(i, ng - 1) // gps, 0, 0))
    n_in = 10
    ltri, lall, lvl = consts
    nht = GROUP_TILES * DN_HEADS
    a_idx = lambda i: (jnp.minimum(i, ng - 1), 0)
    b_idx = lambda i: (jnp.maximum(i - 1, 0), 0)
    return pl.pallas_call(
        functools.partial(_gdn_prompt_body, groups_per_seq=gps),
        grid=(ng + 1,),
        in_specs=[
            pl.BlockSpec((GROUP_TOK, CONV_DIM), a_idx),
            pl.BlockSpec((GROUP_TOK, DN_V), b_idx),
            pl.BlockSpec((GROUP_TOK, LANES), a_idx),
            _layer_spec(l, (CONV_W, CONV_DIM)),
            _layer_spec(l, (1, LANES)),
            _layer_spec(l, (1, LANES)),
            _layer_spec(l, (1, DN_DV)),
            _const_spec((TILE, TILE)),
            _const_spec((TILE, TILE)),
            _const_spec((6, TILE, TILE)),
        ] + s_in_spec + c_in_spec,
        out_specs=[pl.BlockSpec((GROUP_TOK, DN_V), b_idx), s_spec, c_spec],
        out_shape=[jax.ShapeDtypeStruct((rows, DN_V), BF16), s_sds, c_sds],
        input_output_aliases={n_in: 1, n_in + 1: 2} if l else {},
        scratch_shapes=[
            pltpu.VMEM((SUBLANES + GROUP_TOK, CONV_DIM), F32),
            pltpu.VMEM((DN_HEADS, DN_DK, DN_DV), F32),
            pltpu.VMEM((2, nht, TILE, DN_DV), F32),
            pltpu.VMEM((2, nht, 2 * TILE, DN_DK), BF16),
            pltpu.VMEM((2, nht, CHUNKS * (DN_DK + DN_CHUNK), TILE), BF16),
            pltpu.VMEM((2, GROUP_TILES, TILE, LANES), F32),
        ],
        compiler_params=pltpu.CompilerParams(
            dimension_semantics=("arbitrary",), vmem_limit_bytes=VMEM_LIMIT),
        name="gdn_prompt",
    )(raw, z, ba, cw, alog_row, dtb_row, onw, ltri, lall, lvl, *s_in, *c_in)


def _gdn_sample_body(raw_ref, z_ref, ba_ref, cs_ref, st_ref, cw_ref, alog_ref, dtb_ref, onw_ref, *rest):
    od_ref, sout_ref, cout_ref = rest[-3:]
    bb = SAMPLE_BB
    raw = raw_ref[...]
    y = raw * cw_ref[CONV_W - 1:CONV_W, :]
    for i in range(CONV_W - 1):
        y = y + cs_ref[:, i * CONV_DIM:(i + 1) * CONV_DIM] * cw_ref[i:i + 1, :]
    y = _silu(y)
    cout_ref[:, 0:(CONV_W - 2) * CONV_DIM] = cs_ref[:, CONV_DIM:(CONV_W - 1) * CONV_DIM]
    cout_ref[:, (CONV_W - 2) * CONV_DIM:] = raw

    beta_f, g_f = _gate_rows(ba_ref[...], alog_ref[...], dtb_ref[...])
    eg_f = jnp.exp(g_f)
    pad = jnp.zeros((LANES - bb, DN_DK), F32)
    outs = []
    for h in range(DN_HEADS):
        q = _l2n(y[:, h * DN_DK:(h + 1) * DN_DK]) * (DN_DK ** -0.5)
        k = _l2n(y[:, DN_QK + h * DN_DK:DN_QK + (h + 1) * DN_DK])
        v = y[:, 2 * DN_QK + h * DN_DV:2 * DN_QK + (h + 1) * DN_DV]
        k_t = jnp.concatenate([k, pad], axis=0).T
        qk = jnp.sum(q * k, axis=1, keepdims=True)
        o_rows = []
        for b in range(bb):
            s1 = st_ref[b, h] * eg_f[b:b + 1, DN_HEADS + h:DN_HEADS + h + 1]
            kq = jnp.concatenate([k[b:b + 1, :], q[b:b + 1, :]], axis=0).astype(BF16)
            r = _dot(kq, s1.astype(BF16))
            delta = beta_f[b:b + 1, h:h + 1] * (v[b:b + 1, :] - r[0:1, :])
            sout_ref[b, h] = s1 + k_t[:, b:b + 1] * delta
            o_rows.append(r[1:2, :] + qk[b:b + 1, :] * delta)
        o = jnp.concatenate(o_rows, axis=0)
        zh = z_ref[:, h * DN_DV:(h + 1) * DN_DV]
        outs.append(_rms(o, onw_ref[...]) * _silu(zh))
    od_ref[...] = jnp.concatenate(outs, axis=1).astype(od_ref.dtype)


def _gdn_sample(raw, z, ba, conv_state, dn_state, cw, alog_row, dtb_row, onw, od, sstack, cstack,
                l, row0, dbatch):
    bb = SAMPLE_BB
    r0 = row0 // bb
    cflat = (CONV_W - 1) * CONV_DIM
    s_spec, s_sds, s_in, s_in_spec = _stacked_out(
        l, sstack, (dbatch, DN_HEADS, DN_DK, DN_DV), (bb, DN_HEADS, DN_DK, DN_DV), lambda i: (i, 0, 0, 0))
    c_spec, c_sds, c_in, c_in_spec = _stacked_out(l, cstack, (dbatch, cflat), (bb, cflat), lambda i: (i, 0))
    n_in = 10
    return pl.pallas_call(
        _gdn_sample_body,
        grid=(dbatch // bb,),
        in_specs=[
            pl.BlockSpec((bb, CONV_DIM), lambda i: (r0 + i, 0)),
            pl.BlockSpec((bb, DN_V), lambda i: (r0 + i, 0)),
            pl.BlockSpec((bb, LANES), lambda i: (r0 + i, 0)),
            pl.BlockSpec((None, bb, cflat), lambda i: (l, i, 0)),
            pl.BlockSpec((None, bb, DN_HEADS, DN_DK, DN_DV), lambda i: (l, i, 0, 0, 0)),
            _layer_spec(l, (CONV_W, CONV_DIM)),
            _layer_spec(l, (1, LANES)),
            _layer_spec(l, (1, LANES)),
            _layer_spec(l, (1, DN_DV)),
            pl.BlockSpec(memory_space=pl.ANY),
        ] + s_in_spec + c_in_spec,
        out_specs=[pl.BlockSpec((bb, DN_V), lambda i: (r0 + i, 0)), s_spec, c_spec],
        out_shape=[jax.ShapeDtypeStruct(od.shape, od.dtype), s_sds, c_sds],
        input_output_aliases={9: 0, n_in: 1, n_in + 1: 2} if l else {9: 0},
        compiler_params=pltpu.CompilerParams(
            dimension_semantics=("parallel",), vmem_limit_bytes=VMEM_LIMIT),
        name="gdn_sample",
    )(raw, z, ba, conv_state, dn_state, cw, alog_row, dtb_row, onw, od, *s_in, *c_in)


def _np_consts():
    i = np.arange(TILE)[:, None]
    j = np.arange(TILE)[None, :]
    same = (i // DN_CHUNK) == (j // DN_CHUNK)
    ltri = (same & (j <= i)).astype(np.float32)
    lall = same.astype(np.float32)
    lvls = []
    b = 1
    while b < DN_CHUNK:
        lvls.append(((i // (2 * b)) == (j // (2 * b))) & ((i // b) != (j // b)))
        b *= 2
    lvl = np.stack(lvls).astype(np.float32)
    hi = np.arange(LANES)
    ones_bd = (hi[:, None] // HEAD_DIM == hi[None, :] // HEAD_DIM).astype(np.float32)
    return ltri, lall, lvl, ones_bd


def _rope_tables(pos):
    half = HEAD_DIM // 2
    inv = 1.0 / (ROPE_THETA ** (jnp.arange(half, dtype=F32) / half))
    ang = pos.astype(F32)[:, None] * inv[None, :]
    cos, sin = jnp.cos(ang), jnp.sin(ang)
    cos_t = jnp.concatenate([cos, cos] * (LANES // HEAD_DIM), axis=1)
    sin_t = jnp.concatenate([-sin, sin] * (LANES // HEAD_DIM), axis=1)
    return cos_t, sin_t


def _pad_row(x, offset):
    out = jnp.zeros((x.shape[0], 1, LANES), F32)
    return out.at[:, 0, offset:offset + x.shape[1]].set(x.astype(F32))


def kernel(x_prompt, x_sample, cache_swa_k, cache_swa_v, state_dn, state_conv, ffn1_norm, ffn1_w_gate_up, ffn1_w_down, mix_norm, w_in, q_norm, k_norm, attn_sinks, conv_w, dn_A_log, dn_dt_bias, dn_out_norm, w_attn_o, w_dn_o, w_out, ffn2_norm, ffn2_w_gate_up, ffn2_w_down):
    batch, seq, _ = x_prompt.shape
    dbatch = x_sample.shape[0]
    prows = batch * seq
    rows = prows + dbatch
    perm = np.asarray(HEAD_PERM)

    o0 = 0
    segs = []
    for wdt in (ATTN_Q, ATTN_KV, ATTN_KV, CONV_DIM, DN_V, DN_HEADS, DN_HEADS, D_MODEL, D_MODEL):
        segs.append(w_in[:, :, o0:o0 + wdt])
        o0 += wdt
    wq, wk, wv, wraw, wz, wb, wa, wga, wgd = segs
    wq = wq.reshape(DEPTH, D_MODEL, N_HEADS, HEAD_DIM)[:, :, perm].reshape(DEPTH, D_MODEL, ATTN_Q)
    wpad = jnp.zeros((DEPTH, D_MODEL, LANES - 2 * DN_HEADS), w_in.dtype)
    w_in_r = jnp.concatenate([wq, wk, wv, wraw, wz, wb, wa, wpad, wga, wgd], axis=2).astype(BF16)
    wao = w_attn_o.reshape(DEPTH, N_HEADS, HEAD_DIM, D_MODEL)[:, perm].reshape(DEPTH, ATTN_Q, D_MODEL).astype(BF16)
    wdo = w_dn_o.astype(BF16)
    wout = w_out.astype(BF16)
    wgu1, wd1 = ffn1_w_gate_up.astype(BF16), ffn1_w_down.astype(BF16)
    wgu2, wd2 = ffn2_w_gate_up.astype(BF16), ffn2_w_down.astype(BF16)
    n1 = ffn1_norm.reshape(DEPTH, 1, D_MODEL)
    n2 = ffn2_norm.reshape(DEPTH, 1, D_MODEL)
    nm = mix_norm.reshape(DEPTH, 1, D_MODEL)
    qk_nw = jnp.concatenate([jnp.tile(q_norm, (1, N_HEADS)), jnp.tile(k_norm, (1, N_KV_HEADS))], axis=1)
    alog_row = _pad_row(dn_A_log, DN_HEADS)
    dtb_row = _pad_row(dn_dt_bias, DN_HEADS)
    onw = dn_out_norm.reshape(DEPTH, 1, DN_DV)
    sinks_perm = attn_sinks[:, perm]
    sinks_b = jnp.broadcast_to(sinks_perm[:, :, None], (DEPTH, N_HEADS, LANES))

    ltri, lall, lvl, ones_bd = _np_consts()
    gdn_consts = (jnp.asarray(ltri, BF16), jnp.asarray(lall, BF16), jnp.asarray(lvl, BF16))
    ones_bd = jnp.asarray(ones_bd, BF16)
    cos_p, sin_p = _rope_tables(jnp.arange(seq))
    cos_s, sin_s = _rope_tables(PAST_LEN + jnp.arange(1))

    ck = cache_swa_k.reshape(DEPTH, dbatch, WINDOW, ATTN_KV)
    cv = cache_swa_v.reshape(DEPTH, dbatch, WINDOW, ATTN_KV)
    cs = state_conv.reshape(DEPTH, dbatch, (CONV_W - 1) * CONV_DIM)

    h = jnp.concatenate([x_prompt.reshape(prows, D_MODEL), x_sample.reshape(dbatch, D_MODEL)], axis=0)
    kp = vp = sp = cp = ksn = vsn = ssn = csn = None
    for l in range(DEPTH):
        h = _ffn(h, n1, wgu1, wd1, l)
        qkv, raw, z, ba, gates = _inproj(h, nm, w_in_r, l)
        oa, kp, vp = _attn_prompt(qkv, attn_sinks[l], cos_p, sin_p, qk_nw[l:l + 1], ones_bd,
                                  kp, vp, l, batch, seq, rows)
        oa, ksn, vsn = _attn_sample(qkv, ck, cv, cos_s, sin_s, qk_nw[l:l + 1], ones_bd,
                                    sinks_b[l], oa, ksn, vsn, l, prows, dbatch)
        od, sp, cp = _gdn_prompt(raw, z, ba, conv_w, alog_row, dtb_row, onw, gdn_consts,
                                 sp, cp, l, batch, seq, rows)
        od, ssn, csn = _gdn_sample(raw, z, ba, cs, state_dn, conv_w, alog_row, dtb_row, onw,
                                   od, ssn, csn, l, prows, dbatch)
        h = _merge_ffn(h, oa, od, gates, wao, wdo, wout, n2, wgu2, wd2, l)

    kv_shape = (DEPTH, -1, WINDOW, N_KV_HEADS, HEAD_DIM)
    return (h[:prows].reshape(batch, seq, D_MODEL),
            h[prows:].reshape(dbatch, 1, D_MODEL),
            kp.reshape(kv_shape), vp.reshape(kv_shape), sp, cp,
            ksn.reshape(kv_shape), vsn.reshape(kv_shape), ssn,
            csn.reshape(DEPTH, dbatch, CONV_W - 1, CONV_DIM))
```

```python
import functools
import math

import numpy as np
import jax
import jax.numpy as jnp
from jax import lax
from jax.experimental import pallas as pl
from jax.experimental.pallas import tpu as pltpu

F32 = jnp.float32
BF16 = jnp.bfloat16

D_MODEL = 1024
DEPTH = 4
PAST_LEN = 8192
N_HEADS = 8
N_KV_HEADS = 2
GROUP = N_HEADS // N_KV_HEADS
HEAD_DIM = 64
WINDOW = 128
ATTN_BLOCK = 128
ROPE_THETA = 10000.0
DN_HEADS = 4
DN_DK = 128
DN_DV = 128
CONV_W = 4
DN_CHUNK = 64
D_FF = 2816
EPS = 1e-6

ATTN_Q = N_HEADS * HEAD_DIM
ATTN_KV = N_KV_HEADS * HEAD_DIM
DN_QK = DN_HEADS * DN_DK
DN_V = DN_HEADS * DN_DV
CONV_DIM = 2 * DN_QK + DN_V
QKV_W = ATTN_Q + 2 * ATTN_KV
QK_W = ATTN_Q + ATTN_KV
LANES = 128
SUBLANES = 8
TILE = 128
VMEM_LIMIT = 56 * 1024 * 1024

HEAD_PERM = (0, 4, 1, 5, 2, 6, 3, 7)
FF_SPLIT = 1


def _row_tile(rows, cap=512):
    best = SUBLANES
    for t in range(SUBLANES, cap + 1, SUBLANES):
        if rows % t == 0:
            best = t
    return best


def _rms(x, w):
    return x * lax.rsqrt(jnp.mean(x * x, axis=-1, keepdims=True) + EPS) * w


def _dot(a, b):
    return jnp.dot(a, b, preferred_element_type=F32)


def _dot_nt(a, b):
    return lax.dot_general(a, b, (((1,), (1,)), ((), ())), preferred_element_type=F32)


def _split2(x):
    hi = x.astype(BF16)
    return hi, (x - hi.astype(F32)).astype(BF16)


def _split3(x):
    hi = x.astype(BF16)
    r = x - hi.astype(F32)
    mid = r.astype(BF16)
    lo = (r - mid.astype(F32)).astype(BF16)
    return hi, mid, lo


def _dot_exact_lhs(m, x):
    hi, mid, lo = _split3(x)
    return _dot(m, hi) + _dot(m, mid) + _dot(m, lo)


def _const_spec(shape):
    nd = len(shape)
    return pl.BlockSpec(shape, lambda *_: (0,) * nd, pipeline_mode=pl.Buffered(1))


def _layer_spec(l, shape):
    nd = len(shape)
    return pl.BlockSpec((None,) + shape, lambda *_: (l,) + (0,) * nd, pipeline_mode=pl.Buffered(1))


def _stacked_out(l, prev, shape, block, index_map):
    spec = pl.BlockSpec((None,) + block, lambda *idx: (l,) + tuple(index_map(*idx)))
    sds = jax.ShapeDtypeStruct((DEPTH,) + shape, F32)
    if l == 0:
        return spec, sds, [], []
    return spec, sds, [prev], [pl.BlockSpec(memory_space=pl.ANY)]


def _swiglu_residual(x, nw, wgu_ref, wd_ref):
    xn = _rms(x, nw).astype(BF16)
    tf = D_FF // FF_SPLIT
    acc = jnp.zeros_like(x)
    for c in range(FF_SPLIT):
        g = _dot(xn, wgu_ref[:, c * tf:(c + 1) * tf])
        u = _dot(xn, wgu_ref[:, D_FF + c * tf:D_FF + (c + 1) * tf])
        a = (g * jax.nn.sigmoid(g) * u).astype(BF16)
        acc = acc + _dot(a, wd_ref[c * tf:(c + 1) * tf, :])
    return x + 0.5 * acc


def _ffn_body(x_ref, nw_ref, wgu_ref, wd_ref, o_ref):
    o_ref[...] = _swiglu_residual(x_ref[...], nw_ref[...], wgu_ref, wd_ref)


def _ffn(x, norm_w, wgu, wd, l):
    rows = x.shape[0]
    tm = _row_tile(rows)
    return pl.pallas_call(
        _ffn_body,
        grid=(rows // tm,),
        in_specs=[
            pl.BlockSpec((tm, D_MODEL), lambda i: (i, 0)),
            _layer_spec(l, (1, D_MODEL)),
            _layer_spec(l, (D_MODEL, 2 * D_FF)),
            _layer_spec(l, (D_FF, D_MODEL)),
        ],
        out_specs=pl.BlockSpec((tm, D_MODEL), lambda i: (i, 0)),
        out_shape=jax.ShapeDtypeStruct((rows, D_MODEL), F32),
        compiler_params=pltpu.CompilerParams(
            dimension_semantics=("parallel",), vmem_limit_bytes=VMEM_LIMIT),
        name="ffn",
    )(x, norm_w, wgu, wd)


IN_SEGS = (QKV_W, CONV_DIM, DN_V, LANES, 2 * D_MODEL)
IN_DTYPES = (F32, F32, F32, F32, BF16)
IN_COLS = QKV_W + CONV_DIM + DN_V + 2 * DN_HEADS + 2 * D_MODEL
IN_ALIGNED = QKV_W + CONV_DIM + DN_V
IN_TAIL = LANES + 2 * D_MODEL


def _inproj_body(h_ref, nw_ref, w_ref, wq_ref, wt_ref, qkv_ref, raw_ref, z_ref, ba_ref, gates_ref):
    u = _rms(h_ref[...], nw_ref[...]).astype(BF16)
    qkv_ref[:, :ATTN_Q] = _dot(u, wq_ref[...])
    qkv_ref[:, ATTN_Q:] = _dot(u, w_ref[:, ATTN_Q:QKV_W])
    raw_ref[...] = _dot(u, w_ref[:, QKV_W:QKV_W + CONV_DIM])
    z_ref[...] = _dot(u, w_ref[:, QKV_W + CONV_DIM:IN_ALIGNED])
    ba_ref[...] = _dot(u, wt_ref[:, :LANES])
    gates_ref[...] = _dot(u, wt_ref[:, LANES:]).astype(gates_ref.dtype)


def _inproj(h, norm_w, w_in, wq, wt, l):
    rows = h.shape[0]
    tm = _row_tile(rows)
    return pl.pallas_call(
        _inproj_body,
        grid=(rows // tm,),
        in_specs=[
            pl.BlockSpec((tm, D_MODEL), lambda i: (i, 0)),
            _layer_spec(l, (1, D_MODEL)),
            _layer_spec(l, (D_MODEL, IN_COLS)),
            _layer_spec(l, (D_MODEL, ATTN_Q)),
            _layer_spec(l, (D_MODEL, IN_TAIL)),
        ],
        out_specs=[pl.BlockSpec((tm, w), lambda i: (i, 0)) for w in IN_SEGS],
        out_shape=[jax.ShapeDtypeStruct((rows, w), dt) for w, dt in zip(IN_SEGS, IN_DTYPES)],
        compiler_params=pltpu.CompilerParams(
            dimension_semantics=("parallel",), vmem_limit_bytes=VMEM_LIMIT),
        name="inproj",
    )(h, norm_w, w_in, wq, wt)


def _merge_ffn_body(h_ref, oa_ref, od_ref, gates_ref, wao_ref, wdo_ref, wout_ref,
                    nw_ref, wgu_ref, wd_ref, o_ref):
    br_a = _dot(oa_ref[...], wao_ref[...])
    br_d = _dot(od_ref[...], wdo_ref[...])
    ga = gates_ref[:, :D_MODEL].astype(F32)
    gd = gates_ref[:, D_MODEL:].astype(F32)
    m = jax.nn.sigmoid(ga) * br_a + jax.nn.sigmoid(gd) * br_d
    h = h_ref[...] + _dot(m.astype(BF16), wout_ref[...])
    o_ref[...] = _swiglu_residual(h, nw_ref[...], wgu_ref, wd_ref)


def _merge_ffn(h, oa, od, gates, wao, wdo, wout, norm_w, wgu, wd, l):
    rows = h.shape[0]
    tm = _row_tile(rows)
    return pl.pallas_call(
        _merge_ffn_body,
        grid=(rows // tm,),
        in_specs=[
            pl.BlockSpec((tm, D_MODEL), lambda i: (i, 0)),
            pl.BlockSpec((tm, ATTN_Q), lambda i: (i, 0)),
            pl.BlockSpec((tm, DN_V), lambda i: (i, 0)),
            pl.BlockSpec((tm, 2 * D_MODEL), lambda i: (i, 0)),
            _layer_spec(l, (ATTN_Q, D_MODEL)),
            _layer_spec(l, (DN_V, D_MODEL)),
            _layer_spec(l, (D_MODEL, D_MODEL)),
            _layer_spec(l, (1, D_MODEL)),
            _layer_spec(l, (D_MODEL, 2 * D_FF)),
            _layer_spec(l, (D_FF, D_MODEL)),
        ],
        out_specs=pl.BlockSpec((tm, D_MODEL), lambda i: (i, 0)),
        out_shape=jax.ShapeDtypeStruct((rows, D_MODEL), F32),
        compiler_params=pltpu.CompilerParams(
            dimension_semantics=("parallel",), vmem_limit_bytes=VMEM_LIMIT),
        name="merge_ffn",
    )(h, oa, od, gates, wao, wdo, wout, norm_w, wgu, wd)


def _norm_rope(x, ones_bd, nw, cos, sin):
    lane = lax.broadcasted_iota(jnp.int32, (x.shape[0], LANES), 1)
    first_half = (lane & (HEAD_DIM // 2)) == 0
    outs = []
    for g in range(x.shape[1] // LANES):
        xg = x[:, g * LANES:(g + 1) * LANES]
        hi, lo = _split2(xg * xg)
        ssq = _dot(hi, ones_bd) + _dot(lo, ones_bd)
        xn = xg * lax.rsqrt(ssq * (1.0 / HEAD_DIM) + EPS) * nw[:, g * LANES:(g + 1) * LANES]
        partner = jnp.where(first_half,
                            pltpu.roll(xn, LANES - HEAD_DIM // 2, 1),
                            pltpu.roll(xn, HEAD_DIM // 2, 1))
        outs.append(xn * cos + partner * sin)
    return jnp.concatenate(outs, axis=1)


ATTN_STEP_BLOCKS = 2


def _attn_prompt_body(sinks_ref, x_ref, cos_ref, sin_ref, nw_ref, ones_ref, *rest):
    o_ref, kout_ref, vout_ref, kprev, vprev = rest[-5:]
    step = pl.program_id(1)

    @pl.when(step == 0)
    def _():
        kprev[...] = jnp.zeros_like(kprev)
        vprev[...] = jnp.zeros_like(vprev)

    x = x_ref[...]
    qk = _norm_rope(x[:, :QK_W], ones_ref[...], nw_ref[...], cos_ref[...], sin_ref[...])
    k_all = qk[:, ATTN_Q:QK_W]
    v_all = x[:, QK_W:QKV_W]
    k_bf = jnp.concatenate([kprev[...], k_all.astype(BF16)], axis=0)
    v_bf = jnp.concatenate([vprev[...], v_all.astype(BF16)], axis=0)

    r = lax.broadcasted_iota(jnp.int32, (ATTN_BLOCK, 2 * ATTN_BLOCK), 0)
    c = lax.broadcasted_iota(jnp.int32, (ATTN_BLOCK, 2 * ATTN_BLOCK), 1)
    band = (c >= r) & (c <= r + WINDOW)
    first_col = jnp.where(step == 0, ATTN_BLOCK, 0)
    lo = lax.broadcasted_iota(jnp.int32, (ATTN_BLOCK, LANES), 1) < HEAD_DIM
    scale = 1.0 / math.sqrt(HEAD_DIM)

    units = [(sb, g, half) for sb in range(ATTN_STEP_BLOCKS) for g in range(GROUP) for half in range(2)]
    s_raw = {}
    for (sb, g, half) in units:
        rows = slice(sb * ATTN_BLOCK, (sb + 1) * ATTN_BLOCK)
        qg = qk[rows, g * LANES:(g + 1) * LANES] * scale
        qm = jnp.where(lo if half == 0 else jnp.logical_not(lo), qg, 0.0).astype(BF16)
        s_raw[sb, g, half] = _dot_nt(qm, k_bf[sb * ATTN_BLOCK:(sb + 2) * ATTN_BLOCK])
    p_den = {}
    for (sb, g, half) in units:
        vis = band & (c >= first_col) if sb == 0 else band
        s = jnp.where(vis, s_raw.pop((sb, g, half)), -jnp.inf)
        sink = sinks_ref[HEAD_PERM[2 * g + half]]
        m = jnp.maximum(jnp.max(s, axis=1, keepdims=True), sink)
        p = jnp.exp(s - m)
        p_den[sb, g, half] = (p.astype(BF16), jnp.sum(p, axis=1, keepdims=True) + jnp.exp(sink - m))
    for sb in range(ATTN_STEP_BLOCKS):
        outs = []
        for g in range(GROUP):
            res = []
            for half in range(2):
                p, denom = p_den.pop((sb, g, half))
                res.append(_dot(p, v_bf[sb * ATTN_BLOCK:(sb + 2) * ATTN_BLOCK]) / denom)
            outs.append(jnp.where(lo, res[0], res[1]))
        o_ref[sb * ATTN_BLOCK:(sb + 1) * ATTN_BLOCK, :] = jnp.concatenate(outs, axis=1).astype(o_ref.dtype)

    last = slice((ATTN_STEP_BLOCKS - 1) * ATTN_BLOCK, ATTN_STEP_BLOCKS * ATTN_BLOCK)
    kprev[...] = k_all[last].astype(BF16)
    vprev[...] = v_all[last].astype(BF16)

    @pl.when(step == pl.num_programs(1) - 1)
    def _():
        kout_ref[...] = k_all[last]
        vout_ref[...] = v_all[last]


def _attn_prompt(qkv, sinks, cos, sin, nw, ones_bd, kstack, vstack, l, batch, seq, rows):
    rows_step = ATTN_STEP_BLOCKS * ATTN_BLOCK
    nb = seq // rows_step
    kv_block = (None, WINDOW, ATTN_KV)
    kv_idx = lambda b, j: (b, 0, 0)
    k_spec, k_sds, k_in, k_in_spec = _stacked_out(l, kstack, (batch, WINDOW, ATTN_KV), kv_block, kv_idx)
    v_spec, v_sds, v_in, v_in_spec = _stacked_out(l, vstack, (batch, WINDOW, ATTN_KV), kv_block, kv_idx)
    n_in = 6
    return pl.pallas_call(
        _attn_prompt_body,
        grid=(batch, nb),
        in_specs=[
            pl.BlockSpec(memory_space=pltpu.SMEM),
            pl.BlockSpec((rows_step, QKV_W), lambda b, j: (b * nb + j, 0)),
            pl.BlockSpec((rows_step, LANES), lambda b, j: (j, 0)),
            pl.BlockSpec((rows_step, LANES), lambda b, j: (j, 0)),
            _const_spec((1, QK_W)),
            _const_spec((LANES, LANES)),
        ] + k_in_spec + v_in_spec,
        out_specs=[pl.BlockSpec((rows_step, ATTN_Q), lambda b, j: (b * nb + j, 0)), k_spec, v_spec],
        out_shape=[jax.ShapeDtypeStruct((rows, ATTN_Q), BF16), k_sds, v_sds],
        input_output_aliases={n_in: 1, n_in + 1: 2} if l else {},
        scratch_shapes=[pltpu.VMEM((ATTN_BLOCK, ATTN_KV), BF16),
                        pltpu.VMEM((ATTN_BLOCK, ATTN_KV), BF16)],
        compiler_params=pltpu.CompilerParams(
            dimension_semantics=("parallel", "arbitrary"), vmem_limit_bytes=VMEM_LIMIT),
        name="attn_prompt",
    )(sinks, qkv, cos, sin, nw, ones_bd, *k_in, *v_in)


SAMPLE_BB = 16


def _attn_sample_body(x_ref, kc_ref, vc_ref, cos_ref, sin_ref, nw_ref, ones_ref, sinks_ref, *rest):
    o_ref, kout_ref, vout_ref = rest[-3:]
    x = x_ref[...]
    qk = _norm_rope(x[:, :QK_W], ones_ref[...], nw_ref[...], cos_ref[...], sin_ref[...])
    knew = qk[:, ATTN_Q:QK_W]
    vnew = x[:, QK_W:QKV_W]
    scale = 1.0 / math.sqrt(HEAD_DIM)
    row = lax.broadcasted_iota(jnp.int32, (N_HEADS, LANES), 0)
    lane = lax.broadcasted_iota(jnp.int32, (N_HEADS, LANES), 1)
    own = (lane < HEAD_DIM) == ((row & 1) == 0)
    lo1 = lax.broadcasted_iota(jnp.int32, (1, LANES), 1) < HEAD_DIM
    sinks = sinks_ref[...]
    o_rows = []
    for b in range(SAMPLE_BB):
        qrows = [qk[b:b + 1, (i // 2) * LANES:(i // 2 + 1) * LANES] for i in range(N_HEADS)]
        qz = jnp.where(own, jnp.concatenate(qrows, axis=0) * scale, 0.0)
        kb = kc_ref[b]
        vb = vc_ref[b]
        s = _dot_nt(qz.astype(BF16), kb.astype(BF16))
        s_new = jnp.sum(qz * knew[b:b + 1, :], axis=1, keepdims=True)
        sink = sinks[:, 0:1]
        m = jnp.maximum(jnp.maximum(jnp.max(s, axis=1, keepdims=True), s_new), sink)
        p = jnp.exp(s - m)
        p_new = jnp.exp(s_new - m)
        denom = jnp.sum(p, axis=1, keepdims=True) + p_new + jnp.exp(sink - m)
        pv = (_dot(p.astype(BF16), vb.astype(BF16)) + p_new * vnew[b:b + 1, :]) / denom
        o_rows.append(jnp.concatenate(
            [jnp.where(lo1, pv[2 * g:2 * g + 1, :], pv[2 * g + 1:2 * g + 2, :]) for g in range(GROUP)],
            axis=1))
        kout_ref[b, 0:WINDOW - 1, :] = kc_ref[b, 1:WINDOW, :]
        kout_ref[b, WINDOW - 1:WINDOW, :] = knew[b:b + 1, :]
        vout_ref[b, 0:WINDOW - 1, :] = vc_ref[b, 1:WINDOW, :]
        vout_ref[b, WINDOW - 1:WINDOW, :] = vnew[b:b + 1, :]
    o_ref[...] = jnp.concatenate(o_rows, axis=0).astype(o_ref.dtype)


def _attn_sample(qkv, cache_k, cache_v, cos, sin, nw, ones_bd, sinks_b, oa, kstack, vstack,
                 l, row0, dbatch):
    bb = SAMPLE_BB
    r0 = row0 // bb
    kv_block = (bb, WINDOW, ATTN_KV)
    kv_idx = lambda i: (i, 0, 0)
    k_spec, k_sds, k_in, k_in_spec = _stacked_out(l, kstack, (dbatch, WINDOW, ATTN_KV), kv_block, kv_idx)
    v_spec, v_sds, v_in, v_in_spec = _stacked_out(l, vstack, (dbatch, WINDOW, ATTN_KV), kv_block, kv_idx)
    n_in = 9
    return pl.pallas_call(
        _attn_sample_body,
        grid=(dbatch // bb,),
        in_specs=[
            pl.BlockSpec((bb, QKV_W), lambda i: (r0 + i, 0)),
            pl.BlockSpec((None, bb, WINDOW, ATTN_KV), lambda i: (l, i, 0, 0)),
            pl.BlockSpec((None, bb, WINDOW, ATTN_KV), lambda i: (l, i, 0, 0)),
            _const_spec((1, LANES)),
            _const_spec((1, LANES)),
            _const_spec((1, QK_W)),
            _const_spec((LANES, LANES)),
            _const_spec((N_HEADS, LANES)),
            pl.BlockSpec(memory_space=pl.ANY),
        ] + k_in_spec + v_in_spec,
        out_specs=[pl.BlockSpec((bb, ATTN_Q), lambda i: (r0 + i, 0)), k_spec, v_spec],
        out_shape=[jax.ShapeDtypeStruct(oa.shape, oa.dtype), k_sds, v_sds],
        input_output_aliases={8: 0, n_in: 1, n_in + 1: 2} if l else {8: 0},
        compiler_params=pltpu.CompilerParams(
            dimension_semantics=("parallel",), vmem_limit_bytes=VMEM_LIMIT),
        name="attn_sample",
    )(qkv, cache_k, cache_v, cos, sin, nw, ones_bd, sinks_b, oa, *k_in, *v_in)


def _softplus(x):
    return jnp.maximum(x, 0.0) + jnp.log(1.0 + jnp.exp(-jnp.abs(x)))


def _silu(x):
    return x * jax.nn.sigmoid(x)


def _l2n(x):
    return x * lax.rsqrt(jnp.sum(x * x, axis=-1, keepdims=True) + EPS)


def _l2n_mxu(x, ones):
    hi, lo = _split2(x * x)
    return x * lax.rsqrt(_dot(hi, ones) + _dot(lo, ones) + EPS)


def _gate_rows(ba, alog_row, dtb_row):
    beta = jax.nn.sigmoid(ba)
    g = -jnp.exp(alog_row) * _softplus(ba + dtb_row)
    return beta, g


GROUP_TILES = 4
GROUP_TOK = GROUP_TILES * TILE
CHUNKS = TILE // DN_CHUNK


def _gdn_prompt_body(raw_ref, z_ref, ba_ref, cw_ref, alog_ref, dtb_ref, onw_ref,
                     ltri_ref, lall_ref, lvl_ref, *rest, groups_per_seq):
    od_ref, sout_ref, cout_ref, xp, s_scr, ub_scr, wq_scr, kq_scr, egl_scr = rest[-9:]
    i = pl.program_id(0)
    n_groups = pl.num_programs(0) - 1
    ga = jnp.minimum(i, n_groups - 1)
    gb = jnp.maximum(i - 1, 0)
    slot_a = i % 2
    slot_b = 1 - slot_a

    @pl.when(i == 0)
    def _():
        ub_scr[1] = jnp.zeros(ub_scr.shape[1:], F32)
        wq_scr[1] = jnp.zeros(wq_scr.shape[1:], BF16)
        kq_scr[1] = jnp.zeros(kq_scr.shape[1:], BF16)
        egl_scr[1] = jnp.zeros(egl_scr.shape[1:], F32)

    @pl.when(ga % groups_per_seq == 0)
    def _():
        xp[0:SUBLANES, :] = jnp.zeros((SUBLANES, CONV_DIM), F32)

    @pl.when(gb % groups_per_seq == 0)
    def _():
        s_scr[...] = jnp.zeros_like(s_scr)

    @pl.when(ga % groups_per_seq == groups_per_seq - 1)
    def _():
        cout_ref[...] = raw_ref[GROUP_TOK - (CONV_W - 1):GROUP_TOK, :]

    ltri = ltri_ref[...]
    lall = lall_ref[...]
    ri = lax.broadcasted_iota(jnp.int32, (TILE, TILE), 0)
    ci = lax.broadcasted_iota(jnp.int32, (TILE, TILE), 1)
    same = (ri // DN_CHUNK) == (ci // DN_CHUNK)
    causal = same & (ci <= ri)
    strict = same & (ci < ri)
    eye = (ri == ci).astype(F32)
    zeros_half = jnp.zeros((DN_CHUNK, DN_DV), BF16)
    onw = onw_ref[...]
    ones_sq = jnp.ones((DN_DK, LANES), BF16)
    units = [(j, h) for j in range(GROUP_TILES) for h in range(DN_HEADS)]

    st = [s_scr[h] for h in range(DN_HEADS)]
    b_live = {}

    def b_stage1(j, c):
        for h in range(DN_HEADS):
            idx = j * DN_HEADS + h
            r1 = _dot(wq_scr[slot_b, idx, 2 * c * DN_CHUNK:(2 * c + 2) * DN_CHUNK, :], st[h].astype(BF16))
            u = ub_scr[slot_b, idx, c * DN_CHUNK:(c + 1) * DN_CHUNK, :] - r1[:DN_CHUNK]
            parts = [zeros_half] * CHUNKS
            parts[c] = u.astype(BF16)
            b_live[h] = (jnp.concatenate(parts, axis=0), r1[DN_CHUNK:])

    def b_stage2(j, c):
        egl_f = egl_scr[slot_b, j]
        for h in range(DN_HEADS):
            idx = j * DN_HEADS + h
            u_full, oq = b_live[h]
            base = c * (DN_DK + DN_CHUNK)
            r2 = _dot(kq_scr[slot_b, idx, base:base + DN_DK + DN_CHUNK, :], u_full)
            o = oq + r2[DN_DK:]
            zh = z_ref[j * TILE + c * DN_CHUNK:j * TILE + (c + 1) * DN_CHUNK, h * DN_DV:(h + 1) * DN_DV]
            od_ref[j * TILE + c * DN_CHUNK:j * TILE + (c + 1) * DN_CHUNK, h * DN_DV:(h + 1) * DN_DV] = (
                _rms(o, onw) * _silu(zh)).astype(od_ref.dtype)
            st[h] = (st[h] * egl_f[c * DN_CHUNK:c * DN_CHUNK + 1, DN_HEADS + h:DN_HEADS + h + 1]
                     + r2[:DN_DK])

    b_stages = []
    for j in range(GROUP_TILES):
        for c in range(CHUNKS):
            b_stages.append(functools.partial(b_stage1, j, c))
            b_stages.append(functools.partial(b_stage2, j, c))
    b_iter = iter(b_stages)

    def b_step():
        f = next(b_iter, None)
        if f is not None:
            f()

    xp[SUBLANES:SUBLANES + GROUP_TOK, :] = raw_ref[...]
    ys = []
    for j in range(GROUP_TILES):
        r0 = SUBLANES + j * TILE
        y = xp[r0:r0 + TILE, :] * cw_ref[CONV_W - 1:CONV_W, :]
        for s in range(1, CONV_W):
            y = y + xp[r0 - s:r0 - s + TILE, :] * cw_ref[CONV_W - 1 - s:CONV_W - s, :]
        ys.append(_silu(y))
        b_step()
    xp[0:SUBLANES, :] = raw_ref[GROUP_TOK - SUBLANES:GROUP_TOK, :]

    tiles = []
    for j in range(GROUP_TILES):
        beta_f, g_f = _gate_rows(ba_ref[j * TILE:(j + 1) * TILE, :], alog_ref[...], dtb_ref[...])
        gcum = _dot_exact_lhs(ltri, g_f)
        glast = _dot_exact_lhs(lall, g_f)
        egl_scr[slot_a, j] = jnp.exp(glast)
        tiles.append((beta_f, gcum, glast, gcum.T))

    u = {}
    for (j, h) in units:
        beta_f, gcum, glast, gcum_t = tiles[j]
        y = ys[j]
        k = _l2n_mxu(y[:, DN_QK + h * DN_DK:DN_QK + (h + 1) * DN_DK], ones_sq)
        gc = gcum[:, DN_HEADS + h:DN_HEADS + h + 1]
        gr = gcum_t[DN_HEADS + h:DN_HEADS + h + 1, :]
        u[j, h] = dict(k=k, kb=k.astype(BF16), beta=beta_f[:, h:h + 1], gc=gc,
                       gl=glast[:, DN_HEADS + h:DN_HEADS + h + 1],
                       decay=jnp.exp(jnp.where(causal, gc - gr, -jnp.inf)))
    for un in units:
        d = u[un]
        d["kk"] = _dot_nt(d["kb"], d["kb"])
    b_step()
    for un in units:
        d = u[un]
        a = jnp.where(strict, d["beta"] * d.pop("kk") * d["decay"], 0.0)
        d["ab"] = a.astype(BF16)
        d["tinv"] = eye - a * lvl_ref[0].astype(F32)
    for lv in range(1, 6):
        for un in units:
            d = u[un]
            d["tb"] = d["tinv"].astype(BF16)
            d["p"] = _dot(d["tb"], d["ab"] * lvl_ref[lv]).astype(BF16)
        b_step()
        for un in units:
            d = u[un]
            d["tinv"] = d["tinv"] - _dot(d.pop("p"), d.pop("tb"))
        b_step()
    for (j, h) in units:
        d = u[j, h]
        y = ys[j]
        v = y[:, 2 * DN_QK + h * DN_DV:2 * DN_QK + (h + 1) * DN_DV]
        d["eg"] = jnp.exp(d["gc"])
        rhs = jnp.concatenate([v * d["beta"], d["k"] * (d["beta"] * d["eg"])], axis=1).astype(BF16)
        d["sol"] = _dot(d.pop("tinv").astype(BF16), rhs)
    b_step()
    for (j, h) in units:
        d = u[j, h]
        y = ys[j]
        q = _l2n_mxu(y[:, h * DN_DK:(h + 1) * DN_DK], ones_sq) * (DN_DK ** -0.5)
        d["qk"] = (_dot_nt(q.astype(BF16), d["kb"]) * d["decay"]).astype(BF16)
        d["qd"] = (q * d["eg"]).astype(BF16)
    b_step()
    for _ in range(len(b_stages)):
        b_step()
    for (j, h) in units:
        d = u[j, h]
        idx = j * DN_HEADS + h
        sol = d["sol"]
        ub_scr[slot_a, idx] = sol[:, :DN_DV]
        w = sol[:, DN_DV:].astype(BF16)
        kd_t = (d["k"] * jnp.exp(d["gl"] - d["gc"])).T.astype(BF16)
        for c in range(CHUNKS):
            rows = slice(c * DN_CHUNK, (c + 1) * DN_CHUNK)
            wq_scr[slot_a, idx, 2 * c * DN_CHUNK:(2 * c + 1) * DN_CHUNK, :] = w[rows]
            wq_scr[slot_a, idx, (2 * c + 1) * DN_CHUNK:(2 * c + 2) * DN_CHUNK, :] = d["qd"][rows]
            base = c * (DN_DK + DN_CHUNK)
            kq_scr[slot_a, idx, base:base + DN_DK, :] = kd_t
            kq_scr[slot_a, idx, base + DN_DK:base + DN_DK + DN_CHUNK, :] = d["qk"][rows]

    for h in range(DN_HEADS):
        s_scr[h] = st[h]

    @pl.when(gb % groups_per_seq == groups_per_seq - 1)
    def _():
        for h in range(DN_HEADS):
            sout_ref[h] = st[h]


def _gdn_prompt(raw, z, ba, cw, alog_row, dtb_row, onw, consts, sstack, cstack, l, batch, seq, rows):
    gps = seq // GROUP_TOK
    ng = batch * gps
    s_spec, s_sds, s_in, s_in_spec = _stacked_out(
        l, sstack, (batch, DN_HEADS, DN_DK, DN_DV), (None, DN_HEADS, DN_DK, DN_DV),
        lambda i: (jnp.maximum(i - 1, 0) // gps, 0, 0, 0))
    c_spec, c_sds, c_in, c_in_spec = _stacked_out(
        l, cstack, (batch, CONV_W - 1, CONV_DIM), (None, CONV_W - 1, CONV_DIM),
        lambda i: (jnp.minimum(i, ng - 1) // gps, 0, 0))
    n_in = 10
    ltri, lall, lvl = consts
    nht = GROUP_TILES * DN_HEADS
    a_idx = lambda i: (jnp.minimum(i, ng - 1), 0)
    b_idx = lambda i: (jnp.maximum(i - 1, 0), 0)
    return pl.pallas_call(
        functools.partial(_gdn_prompt_body, groups_per_seq=gps),
        grid=(ng + 1,),
        in_specs=[
            pl.BlockSpec((GROUP_TOK, CONV_DIM), a_idx),
            pl.BlockSpec((GROUP_TOK, DN_V), b_idx),
            pl.BlockSpec((GROUP_TOK, LANES), a_idx),
            _layer_spec(l, (CONV_W, CONV_DIM)),
            _layer_spec(l, (1, LANES)),
            _layer_spec(l, (1, LANES)),
            _layer_spec(l, (1, DN_DV)),
            _const_spec((TILE, TILE)),
            _const_spec((TILE, TILE)),
            _const_spec((6, TILE, TILE)),
        ] + s_in_spec + c_in_spec,
        out_specs=[pl.BlockSpec((GROUP_TOK, DN_V), b_idx), s_spec, c_spec],
        out_shape=[jax.ShapeDtypeStruct((rows, DN_V), BF16), s_sds, c_sds],
        input_output_aliases={n_in: 1, n_in + 1: 2} if l else {},
        scratch_shapes=[
            pltpu.VMEM((SUBLANES + GROUP_TOK, CONV_DIM), F32),
            pltpu.VMEM((DN_HEADS, DN_DK, DN_DV), F32),
            pltpu.VMEM((2, nht, TILE, DN_DV), F32),
            pltpu.VMEM((2, nht, 2 * TILE, DN_DK), BF16),
            pltpu.VMEM((2, nht, CHUNKS * (DN_DK + DN_CHUNK), TILE), BF16),
            pltpu.VMEM((2, GROUP_TILES, TILE, LANES), F32),
        ],
        compiler_params=pltpu.CompilerParams(
            dimension_semantics=("arbitrary",), vmem_limit_bytes=VMEM_LIMIT),
        name="gdn_prompt",
    )(raw, z, ba, cw, alog_row, dtb_row, onw, ltri, lall, lvl, *s_in, *c_in)


def _gdn_sample_body(raw_ref, z_ref, ba_ref, cs_ref, st_ref, cw_ref, alog_ref, dtb_ref, onw_ref, *rest):
    od_ref, sout_ref, cout_ref = rest[-3:]
    bb = SAMPLE_BB
    raw = raw_ref[...]
    y = raw * cw_ref[CONV_W - 1:CONV_W, :]
    for i in range(CONV_W - 1):
        y = y + cs_ref[i] * cw_ref[i:i + 1, :]
    y = _silu(y)
    for i in range(CONV_W - 2):
        cout_ref[i] = cs_ref[i + 1]
    cout_ref[CONV_W - 2] = raw

    beta_f, g_f = _gate_rows(ba_ref[...], alog_ref[...], dtb_ref[...])
    eg_f = jnp.exp(g_f)
    pad = jnp.zeros((LANES - bb, DN_DK), F32)
    outs = []
    for h in range(DN_HEADS):
        q = _l2n(y[:, h * DN_DK:(h + 1) * DN_DK]) * (DN_DK ** -0.5)
        k = _l2n(y[:, DN_QK + h * DN_DK:DN_QK + (h + 1) * DN_DK])
        v = y[:, 2 * DN_QK + h * DN_DV:2 * DN_QK + (h + 1) * DN_DV]
        k_t = jnp.concatenate([k, pad], axis=0).T
        qk = jnp.sum(q * k, axis=1, keepdims=True)
        o_rows = []
        for b in range(bb):
            s1 = st_ref[b, h] * eg_f[b:b + 1, DN_HEADS + h:DN_HEADS + h + 1]
            kq = jnp.concatenate([k[b:b + 1, :], q[b:b + 1, :]], axis=0).astype(BF16)
            r = _dot(kq, s1.astype(BF16))
            delta = beta_f[b:b + 1, h:h + 1] * (v[b:b + 1, :] - r[0:1, :])
            sout_ref[b, h] = s1 + k_t[:, b:b + 1] * delta
            o_rows.append(r[1:2, :] + qk[b:b + 1, :] * delta)
        o = jnp.concatenate(o_rows, axis=0)
        zh = z_ref[:, h * DN_DV:(h + 1) * DN_DV]
        outs.append(_rms(o, onw_ref[...]) * _silu(zh))
    od_ref[...] = jnp.concatenate(outs, axis=1).astype(od_ref.dtype)


def _gdn_sample(raw, z, ba, conv_state, dn_state, cw, alog_row, dtb_row, onw, od, sstack, cstack,
                l, row0, dbatch):
    bb = SAMPLE_BB
    r0 = row0 // bb
    s_spec, s_sds, s_in, s_in_spec = _stacked_out(
        l, sstack, (dbatch, DN_HEADS, DN_DK, DN_DV), (bb, DN_HEADS, DN_DK, DN_DV), lambda i: (i, 0, 0, 0))
    c_spec, c_sds, c_in, c_in_spec = _stacked_out(
        l, cstack, (CONV_W - 1, dbatch, CONV_DIM), (CONV_W - 1, bb, CONV_DIM), lambda i: (0, i, 0))
    n_in = 10
    return pl.pallas_call(
        _gdn_sample_body,
        grid=(dbatch // bb,),
        in_specs=[
            pl.BlockSpec((bb, CONV_DIM), lambda i: (r0 + i, 0)),
            pl.BlockSpec((bb, DN_V), lambda i: (r0 + i, 0)),
            pl.BlockSpec((bb, LANES), lambda i: (r0 + i, 0)),
            pl.BlockSpec((None, CONV_W - 1, bb, CONV_DIM), lambda i: (l, 0, i, 0)),
            pl.BlockSpec((None, bb, DN_HEADS, DN_DK, DN_DV), lambda i: (l, i, 0, 0, 0)),
            _layer_spec(l, (CONV_W, CONV_DIM)),
            _layer_spec(l, (1, LANES)),
            _layer_spec(l, (1, LANES)),
            _layer_spec(l, (1, DN_DV)),
            pl.BlockSpec(memory_space=pl.ANY),
        ] + s_in_spec + c_in_spec,
        out_specs=[pl.BlockSpec((bb, DN_V), lambda i: (r0 + i, 0)), s_spec, c_spec],
        out_shape=[jax.ShapeDtypeStruct(od.shape, od.dtype), s_sds, c_sds],
        input_output_aliases={9: 0, n_in: 1, n_in + 1: 2} if l else {9: 0},
        compiler_params=pltpu.CompilerParams(
            dimension_semantics=("parallel",), vmem_limit_bytes=VMEM_LIMIT),
        name="gdn_sample",
    )(raw, z, ba, conv_state, dn_state, cw, alog_row, dtb_row, onw, od, *s_in, *c_in)


def _np_consts():
    i = np.arange(TILE)[:, None]
    j = np.arange(TILE)[None, :]
    same = (i // DN_CHUNK) == (j // DN_CHUNK)
    ltri = (same & (j <= i)).astype(np.float32)
    lall = same.astype(np.float32)
    lvls = []
    b = 1
    while b < DN_CHUNK:
        lvls.append(((i // (2 * b)) == (j // (2 * b))) & ((i // b) != (j // b)))
        b *= 2
    lvl = np.stack(lvls).astype(np.float32)
    hi = np.arange(LANES)
    ones_bd = (hi[:, None] // HEAD_DIM == hi[None, :] // HEAD_DIM).astype(np.float32)
    return ltri, lall, lvl, ones_bd


def _rope_tables(pos):
    half = HEAD_DIM // 2
    inv = 1.0 / (ROPE_THETA ** (jnp.arange(half, dtype=F32) / half))
    ang = pos.astype(F32)[:, None] * inv[None, :]
    cos, sin = jnp.cos(ang), jnp.sin(ang)
    cos_t = jnp.concatenate([cos, cos] * (LANES // HEAD_DIM), axis=1)
    sin_t = jnp.concatenate([-sin, sin] * (LANES // HEAD_DIM), axis=1)
    return cos_t, sin_t


def _pad_row(x, offset):
    out = jnp.zeros((x.shape[0], 1, LANES), F32)
    return out.at[:, 0, offset:offset + x.shape[1]].set(x.astype(F32))


def kernel(x_prompt, x_sample, cache_swa_k, cache_swa_v, state_dn, state_conv, ffn1_norm, ffn1_w_gate_up, ffn1_w_down, mix_norm, w_in, q_norm, k_norm, attn_sinks, conv_w, dn_A_log, dn_dt_bias, dn_out_norm, w_attn_o, w_dn_o, w_out, ffn2_norm, ffn2_w_gate_up, ffn2_w_down):
    batch, seq, _ = x_prompt.shape
    dbatch = x_sample.shape[0]
    prows = batch * seq
    rows = prows + dbatch
    perm = np.asarray(HEAD_PERM)

    w_in_b = w_in.astype(BF16)
    wq = w_in_b[:, :, :ATTN_Q].reshape(DEPTH, D_MODEL, N_HEADS, HEAD_DIM)[:, :, perm]
    wq = wq.reshape(DEPTH, D_MODEL, ATTN_Q)
    wpad = jnp.zeros((DEPTH, D_MODEL, LANES - 2 * DN_HEADS), BF16)
    w_tail = jnp.concatenate([w_in_b[:, :, IN_ALIGNED:IN_ALIGNED + 2 * DN_HEADS], wpad,
                              w_in_b[:, :, IN_ALIGNED + 2 * DN_HEADS:]], axis=2)
    wao = w_attn_o.reshape(DEPTH, N_HEADS, HEAD_DIM, D_MODEL)[:, perm].reshape(DEPTH, ATTN_Q, D_MODEL).astype(BF16)
    wdo = w_dn_o.astype(BF16)
    wout = w_out.astype(BF16)
    wgu1, wd1 = ffn1_w_gate_up.astype(BF16), ffn1_w_down.astype(BF16)
    wgu2, wd2 = ffn2_w_gate_up.astype(BF16), ffn2_w_down.astype(BF16)
    n1 = ffn1_norm.reshape(DEPTH, 1, D_MODEL)
    n2 = ffn2_norm.reshape(DEPTH, 1, D_MODEL)
    nm = mix_norm.reshape(DEPTH, 1, D_MODEL)
    qk_nw = jnp.concatenate([jnp.tile(q_norm, (1, N_HEADS)), jnp.tile(k_norm, (1, N_KV_HEADS))], axis=1)
    alog_row = _pad_row(dn_A_log, DN_HEADS)
    dtb_row = _pad_row(dn_dt_bias, DN_HEADS)
    onw = dn_out_norm.reshape(DEPTH, 1, DN_DV)
    sinks_perm = attn_sinks[:, perm]
    sinks_b = jnp.broadcast_to(sinks_perm[:, :, None], (DEPTH, N_HEADS, LANES))

    ltri, lall, lvl, ones_bd = _np_consts()
    gdn_consts = (jnp.asarray(ltri, BF16), jnp.asarray(lall, BF16), jnp.asarray(lvl, BF16))
    ones_bd = jnp.asarray(ones_bd, BF16)
    cos_p, sin_p = _rope_tables(jnp.arange(seq))
    cos_s, sin_s = _rope_tables(PAST_LEN + jnp.arange(1))

    ck = cache_swa_k.reshape(DEPTH, dbatch, WINDOW, ATTN_KV)
    cv = cache_swa_v.reshape(DEPTH, dbatch, WINDOW, ATTN_KV)
    cs = jnp.transpose(state_conv, (0, 2, 1, 3))

    h = jnp.concatenate([x_prompt.reshape(prows, D_MODEL), x_sample.reshape(dbatch, D_MODEL)], axis=0)
    kp = vp = sp = cp = ksn = vsn = ssn = csn = None
    for l in range(DEPTH):
        h = _ffn(h, n1, wgu1, wd1, l)
        qkv, raw, z, ba, gates = _inproj(h, nm, w_in_b, wq, w_tail, l)
        oa, kp, vp = _attn_prompt(qkv, attn_sinks[l], cos_p, sin_p, qk_nw[l:l + 1], ones_bd,
                                  kp, vp, l, batch, seq, rows)
        oa, ksn, vsn = _attn_sample(qkv, ck, cv, cos_s, sin_s, qk_nw[l:l + 1], ones_bd,
                                    sinks_b[l], oa, ksn, vsn, l, prows, dbatch)
        od, sp, cp = _gdn_prompt(raw, z, ba, conv_w, alog_row, dtb_row, onw, gdn_consts,
                                 sp, cp, l, batch, seq, rows)
        od, ssn, csn = _gdn_sample(raw, z, ba, cs, state_dn, conv_w, alog_row, dtb_row, onw,
                                   od, ssn, csn, l, prows, dbatch)
        h = _merge_ffn(h, oa, od, gates, wao, wdo, wout, n2, wgu2, wd2, l)

    kv_shape = (DEPTH, -1, WINDOW, N_KV_HEADS, HEAD_DIM)
    return (h[:prows].reshape(batch, seq, D_MODEL),
            h[prows:].reshape(dbatch, 1, D_MODEL),
            kp.reshape(kv_shape), vp.reshape(kv_shape), sp, cp,
            ksn.reshape(kv_shape), vsn.reshape(kv_shape), ssn,
            jnp.transpose(csn, (0, 2, 1, 3)))
```

```python
import functools
import math

import numpy as np
import jax
import jax.numpy as jnp
from jax import lax
from jax.experimental import pallas as pl
from jax.experimental.pallas import tpu as pltpu

F32 = jnp.float32
BF16 = jnp.bfloat16

D_MODEL = 1024
DEPTH = 4
PAST_LEN = 8192
N_HEADS = 8
N_KV_HEADS = 2
GROUP = N_HEADS // N_KV_HEADS
HEAD_DIM = 64
WINDOW = 128
ATTN_BLOCK = 128
ROPE_THETA = 10000.0
DN_HEADS = 4
DN_DK = 128
DN_DV = 128
CONV_W = 4
DN_CHUNK = 64
D_FF = 2816
EPS = 1e-6
LOG2E = math.log2(math.e)

ATTN_Q = N_HEADS * HEAD_DIM
ATTN_KV = N_KV_HEADS * HEAD_DIM
DN_QK = DN_HEADS * DN_DK
DN_V = DN_HEADS * DN_DV
CONV_DIM = 2 * DN_QK + DN_V
QKV_W = ATTN_Q + 2 * ATTN_KV
QK_W = ATTN_Q + ATTN_KV
LANES = 128
SUBLANES = 8
TILE = 128
VMEM_LIMIT = 56 * 1024 * 1024

HEAD_PERM = (0, 4, 1, 5, 2, 6, 3, 7)
FF_SPLIT = 1


def _row_tile(rows, cap=512):
    best = SUBLANES
    for t in range(SUBLANES, cap + 1, SUBLANES):
        if rows % t == 0:
            best = t
    return best


def _rms(x, w):
    return x * lax.rsqrt(jnp.mean(x * x, axis=-1, keepdims=True) + EPS) * w


def _dot(a, b):
    return jnp.dot(a, b, preferred_element_type=F32)


def _dot_nt(a, b):
    return lax.dot_general(a, b, (((1,), (1,)), ((), ())), preferred_element_type=F32)


def _split2(x):
    hi = x.astype(BF16)
    return hi, (x - hi.astype(F32)).astype(BF16)


def _split3(x):
    hi = x.astype(BF16)
    r = x - hi.astype(F32)
    mid = r.astype(BF16)
    lo = (r - mid.astype(F32)).astype(BF16)
    return hi, mid, lo


def _const_spec(shape):
    nd = len(shape)
    return pl.BlockSpec(shape, lambda *_: (0,) * nd, pipeline_mode=pl.Buffered(1))


def _layer_spec(l, shape):
    nd = len(shape)
    return pl.BlockSpec((None,) + shape, lambda *_: (l,) + (0,) * nd, pipeline_mode=pl.Buffered(1))


def _stacked_out(l, prev, shape, block, index_map):
    spec = pl.BlockSpec((None,) + block, lambda *idx: (l,) + tuple(index_map(*idx)))
    sds = jax.ShapeDtypeStruct((DEPTH,) + shape, F32)
    if l == 0:
        return spec, sds, [], []
    return spec, sds, [prev], [pl.BlockSpec(memory_space=pl.ANY)]


def _swiglu_residual(x, nw, wgu_ref, wd_ref):
    xn = _rms(x, nw).astype(BF16)
    tf = D_FF // FF_SPLIT
    acc = jnp.zeros_like(x)
    for c in range(FF_SPLIT):
        g = _dot(xn, wgu_ref[:, c * tf:(c + 1) * tf])
        u = _dot(xn, wgu_ref[:, D_FF + c * tf:D_FF + (c + 1) * tf])
        a = (g * jax.nn.sigmoid(g) * u).astype(BF16)
        acc = acc + _dot(a, wd_ref[c * tf:(c + 1) * tf, :])
    return x + 0.5 * acc


def _ffn_body(x_ref, nw_ref, wgu_ref, wd_ref, o_ref):
    o_ref[...] = _swiglu_residual(x_ref[...], nw_ref[...], wgu_ref, wd_ref)


def _ffn_first_body(*refs):
    *piece_refs, xs_ref, nw_ref, wgu_ref, wd_ref, o_ref = refs
    last = pl.program_id(0) == pl.num_programs(0) - 1
    pieces = [r[...] for r in piece_refs]
    pieces[-1] = jnp.where(last, xs_ref[...], pieces[-1])
    o_ref[...] = _swiglu_residual(jnp.concatenate(pieces, axis=0), nw_ref[...], wgu_ref, wd_ref)


def _ffn_first(xp, xs, norm_w, wgu, wd, l):
    prows, dbatch = xp.shape[0], xs.shape[0]
    rows = prows + dbatch
    tm = _row_tile(rows)
    n_piece = tm // dbatch
    last_piece = prows // dbatch - 1
    piece_spec = lambda p: pl.BlockSpec(
        (dbatch, D_MODEL), lambda i: (jnp.minimum(i * n_piece + p, last_piece), 0))
    return pl.pallas_call(
        _ffn_first_body,
        grid=(rows // tm,),
        in_specs=[piece_spec(p) for p in range(n_piece)] + [
            _const_spec((dbatch, D_MODEL)),
            _layer_spec(l, (1, D_MODEL)),
            _layer_spec(l, (D_MODEL, 2 * D_FF)),
            _layer_spec(l, (D_FF, D_MODEL)),
        ],
        out_specs=pl.BlockSpec((tm, D_MODEL), lambda i: (i, 0)),
        out_shape=jax.ShapeDtypeStruct((rows, D_MODEL), F32),
        compiler_params=pltpu.CompilerParams(
            dimension_semantics=("parallel",), vmem_limit_bytes=VMEM_LIMIT),
        name="ffn_first",
    )(*([xp] * n_piece), xs, norm_w, wgu, wd)


def _can_split_rows(prows, dbatch):
    tm = _row_tile(prows + dbatch)
    return tm % dbatch == 0 and prows % dbatch == 0 and dbatch % SUBLANES == 0


def _ffn(x, norm_w, wgu, wd, l):
    rows = x.shape[0]
    tm = _row_tile(rows)
    return pl.pallas_call(
        _ffn_body,
        grid=(rows // tm,),
        in_specs=[
            pl.BlockSpec((tm, D_MODEL), lambda i: (i, 0)),
            _layer_spec(l, (1, D_MODEL)),
            _layer_spec(l, (D_MODEL, 2 * D_FF)),
            _layer_spec(l, (D_FF, D_MODEL)),
        ],
        out_specs=pl.BlockSpec((tm, D_MODEL), lambda i: (i, 0)),
        out_shape=jax.ShapeDtypeStruct((rows, D_MODEL), F32),
        compiler_params=pltpu.CompilerParams(
            dimension_semantics=("parallel",), vmem_limit_bytes=VMEM_LIMIT),
        name="ffn",
    )(x, norm_w, wgu, wd)


IN_SEGS = (QKV_W, CONV_DIM, DN_V, LANES, 2 * D_MODEL)
IN_DTYPES = (F32, F32, F32, F32, BF16)
IN_COLS = QKV_W + CONV_DIM + DN_V + 2 * DN_HEADS + 2 * D_MODEL
IN_ALIGNED = QKV_W + CONV_DIM + DN_V


def _inproj_body(h_ref, nw_ref, w_ref, wq_ref, wba_ref, wg_ref, qkv_ref, raw_ref, z_ref, ba_ref, gates_ref):
    u = _rms(h_ref[...], nw_ref[...]).astype(BF16)
    qkv_ref[:, :ATTN_Q] = _dot(u, wq_ref[...])
    qkv_ref[:, ATTN_Q:] = _dot(u, w_ref[:, ATTN_Q:QKV_W])
    raw_ref[...] = _dot(u, w_ref[:, QKV_W:QKV_W + CONV_DIM])
    z_ref[...] = _dot(u, w_ref[:, QKV_W + CONV_DIM:IN_ALIGNED])
    ba_ref[...] = _dot(u, wba_ref[...])
    gates_ref[...] = _dot(u, wg_ref[...]).astype(gates_ref.dtype)


def _inproj(h, norm_w, w_in, wq, wba, wg, l):
    rows = h.shape[0]
    tm = _row_tile(rows)
    return pl.pallas_call(
        _inproj_body,
        grid=(rows // tm,),
        in_specs=[
            pl.BlockSpec((tm, D_MODEL), lambda i: (i, 0)),
            _layer_spec(l, (1, D_MODEL)),
            _layer_spec(l, (D_MODEL, IN_COLS)),
            _layer_spec(l, (D_MODEL, ATTN_Q)),
            _layer_spec(l, (D_MODEL, LANES)),
            _layer_spec(l, (D_MODEL, 2 * D_MODEL)),
        ],
        out_specs=[pl.BlockSpec((tm, w), lambda i: (i, 0)) for w in IN_SEGS],
        out_shape=[jax.ShapeDtypeStruct((rows, w), dt) for w, dt in zip(IN_SEGS, IN_DTYPES)],
        compiler_params=pltpu.CompilerParams(
            dimension_semantics=("parallel",), vmem_limit_bytes=VMEM_LIMIT),
        name="inproj",
    )(h, norm_w, w_in, wq, wba, wg)


def _merge_ffn_body(h_ref, oa_ref, od_ref, gates_ref, wao_ref, wdo_ref, wout_ref,
                    nw_ref, wgu_ref, wd_ref, o_ref, *sample_out):
    br_a = _dot(oa_ref[...], wao_ref[...])
    br_d = _dot(od_ref[...], wdo_ref[...])
    ga = gates_ref[:, :D_MODEL].astype(F32)
    gd = gates_ref[:, D_MODEL:].astype(F32)
    m = jax.nn.sigmoid(ga) * br_a + jax.nn.sigmoid(gd) * br_d
    h = h_ref[...] + _dot(m.astype(BF16), wout_ref[...])
    y = _swiglu_residual(h, nw_ref[...], wgu_ref, wd_ref)
    o_ref[...] = y
    if sample_out:
        (os_ref,) = sample_out

        @pl.when(pl.program_id(0) == pl.num_programs(0) - 1)
        def _():
            os_ref[...] = y[y.shape[0] - os_ref.shape[0]:, :]


def _merge_ffn(h, oa, od, gates, wao, wdo, wout, norm_w, wgu, wd, l, split=None):
    rows = h.shape[0]
    tm = _row_tile(rows)
    out_specs = pl.BlockSpec((tm, D_MODEL), lambda i: (i, 0))
    out_shape = jax.ShapeDtypeStruct((rows, D_MODEL), F32)
    if split is not None:
        prows, dbatch = split
        out_specs = [out_specs, pl.BlockSpec((dbatch, D_MODEL), lambda i: (0, 0))]
        out_shape = [jax.ShapeDtypeStruct((prows, D_MODEL), F32), jax.ShapeDtypeStruct((dbatch, D_MODEL), F32)]
    return pl.pallas_call(
        _merge_ffn_body,
        grid=(rows // tm,),
        in_specs=[
            pl.BlockSpec((tm, D_MODEL), lambda i: (i, 0)),
            pl.BlockSpec((tm, ATTN_Q), lambda i: (i, 0)),
            pl.BlockSpec((tm, DN_V), lambda i: (i, 0)),
            pl.BlockSpec((tm, 2 * D_MODEL), lambda i: (i, 0)),
            _layer_spec(l, (ATTN_Q, D_MODEL)),
            _layer_spec(l, (DN_V, D_MODEL)),
            _layer_spec(l, (D_MODEL, D_MODEL)),
            _layer_spec(l, (1, D_MODEL)),
            _layer_spec(l, (D_MODEL, 2 * D_FF)),
            _layer_spec(l, (D_FF, D_MODEL)),
        ],
        out_specs=out_specs,
        out_shape=out_shape,
        compiler_params=pltpu.CompilerParams(
            dimension_semantics=("arbitrary",), vmem_limit_bytes=VMEM_LIMIT),
        name="merge_ffn",
    )(h, oa, od, gates, wao, wdo, wout, norm_w, wgu, wd)


def _norm_rope(x, ones_bd, nw, cos, sin):
    lane = lax.broadcasted_iota(jnp.int32, (x.shape[0], LANES), 1)
    first_half = (lane & (HEAD_DIM // 2)) == 0
    outs = []
    for g in range(x.shape[1] // LANES):
        xg = x[:, g * LANES:(g + 1) * LANES]
        hi, lo = _split2(xg * xg)
        ssq = _dot(hi, ones_bd) + _dot(lo, ones_bd)
        xn = xg * lax.rsqrt(ssq * (1.0 / HEAD_DIM) + EPS) * nw[:, g * LANES:(g + 1) * LANES]
        partner = jnp.where(first_half,
                            pltpu.roll(xn, LANES - HEAD_DIM // 2, 1),
                            pltpu.roll(xn, HEAD_DIM // 2, 1))
        outs.append(xn * cos + partner * sin)
    return jnp.concatenate(outs, axis=1)


ATTN_STEP_BLOCKS = 2


def _attn_prompt_body(sinks_ref, x_ref, cos_ref, sin_ref, nw_ref, ones_ref, *rest):
    o_ref, kout_ref, vout_ref, kprev, vprev = rest[-5:]
    step = pl.program_id(1)

    @pl.when(step == 0)
    def _():
        kprev[...] = jnp.zeros_like(kprev)
        vprev[...] = jnp.zeros_like(vprev)

    x = x_ref[...]
    qk = _norm_rope(x[:, :QK_W], ones_ref[...], nw_ref[...], cos_ref[...], sin_ref[...])
    k_all = qk[:, ATTN_Q:QK_W]
    v_all = x[:, QK_W:QKV_W]
    k_bf = jnp.concatenate([kprev[...], k_all.astype(BF16)], axis=0)
    v_bf = jnp.concatenate([vprev[...], v_all.astype(BF16)], axis=0)

    r = lax.broadcasted_iota(jnp.int32, (ATTN_BLOCK, 2 * ATTN_BLOCK), 0)
    c = lax.broadcasted_iota(jnp.int32, (ATTN_BLOCK, 2 * ATTN_BLOCK), 1)
    band = (c >= r) & (c <= r + WINDOW)
    first_col = jnp.where(step == 0, ATTN_BLOCK, 0)
    lo = lax.broadcasted_iota(jnp.int32, (ATTN_BLOCK, LANES), 1) < HEAD_DIM
    scale = LOG2E / math.sqrt(HEAD_DIM)
    klane = lax.broadcasted_iota(jnp.int32, k_bf.shape, 1) < HEAD_DIM
    k_half = (jnp.where(klane, k_bf, jnp.zeros_like(k_bf)), jnp.where(klane, jnp.zeros_like(k_bf), k_bf))

    units = [(sb, g, half) for sb in range(ATTN_STEP_BLOCKS) for g in range(GROUP) for half in range(2)]
    q_bf = {(sb, g): (qk[sb * ATTN_BLOCK:(sb + 1) * ATTN_BLOCK, g * LANES:(g + 1) * LANES] * scale).astype(BF16)
            for sb in range(ATTN_STEP_BLOCKS) for g in range(GROUP)}
    s_raw = {}
    for (sb, g, half) in units:
        s_raw[sb, g, half] = _dot_nt(q_bf[sb, g], k_half[half][sb * ATTN_BLOCK:(sb + 2) * ATTN_BLOCK])
    p_den = {}
    for (sb, g, half) in units:
        vis = band & (c >= first_col) if sb == 0 else band
        s = jnp.where(vis, s_raw.pop((sb, g, half)), -jnp.inf)
        sink = sinks_ref[HEAD_PERM[2 * g + half]] * LOG2E
        m = jnp.maximum(jnp.max(s, axis=1, keepdims=True), sink)
        p = jnp.exp2(s - m)
        p_den[sb, g, half] = (p.astype(BF16), jnp.sum(p, axis=1, keepdims=True) + jnp.exp2(sink - m))
    for sb in range(ATTN_STEP_BLOCKS):
        outs = []
        for g in range(GROUP):
            res = []
            for half in range(2):
                p, denom = p_den.pop((sb, g, half))
                res.append(_dot(p, v_bf[sb * ATTN_BLOCK:(sb + 2) * ATTN_BLOCK]) / denom)
            outs.append(jnp.where(lo, res[0], res[1]))
        o_ref[sb * ATTN_BLOCK:(sb + 1) * ATTN_BLOCK, :] = jnp.concatenate(outs, axis=1).astype(o_ref.dtype)

    last = slice((ATTN_STEP_BLOCKS - 1) * ATTN_BLOCK, ATTN_STEP_BLOCKS * ATTN_BLOCK)
    kprev[...] = k_all[last].astype(BF16)
    vprev[...] = v_all[last].astype(BF16)

    @pl.when(step == pl.num_programs(1) - 1)
    def _():
        kout_ref[...] = k_all[last]
        vout_ref[...] = v_all[last]


def _attn_prompt(qkv, sinks, cos, sin, nw, ones_bd, kstack, vstack, l, batch, seq, rows):
    rows_step = ATTN_STEP_BLOCKS * ATTN_BLOCK
    nb = seq // rows_step
    kv_block = (None, WINDOW, ATTN_KV)
    kv_idx = lambda b, j: (b, 0, 0)
    k_spec, k_sds, k_in, k_in_spec = _stacked_out(l, kstack, (batch, WINDOW, ATTN_KV), kv_block, kv_idx)
    v_spec, v_sds, v_in, v_in_spec = _stacked_out(l, vstack, (batch, WINDOW, ATTN_KV), kv_block, kv_idx)
    n_in = 6
    return pl.pallas_call(
        _attn_prompt_body,
        grid=(batch, nb),
        in_specs=[
            pl.BlockSpec(memory_space=pltpu.SMEM),
            pl.BlockSpec((rows_step, QKV_W), lambda b, j: (b * nb + j, 0)),
            pl.BlockSpec((rows_step, LANES), lambda b, j: (j, 0)),
            pl.BlockSpec((rows_step, LANES), lambda b, j: (j, 0)),
            _const_spec((1, QK_W)),
            _const_spec((LANES, LANES)),
        ] + k_in_spec + v_in_spec,
        out_specs=[pl.BlockSpec((rows_step, ATTN_Q), lambda b, j: (b * nb + j, 0)), k_spec, v_spec],
        out_shape=[jax.ShapeDtypeStruct((rows, ATTN_Q), BF16), k_sds, v_sds],
        input_output_aliases={n_in: 1, n_in + 1: 2} if l else {},
        scratch_shapes=[pltpu.VMEM((ATTN_BLOCK, ATTN_KV), BF16),
                        pltpu.VMEM((ATTN_BLOCK, ATTN_KV), BF16)],
        compiler_params=pltpu.CompilerParams(
            dimension_semantics=("parallel", "arbitrary"), vmem_limit_bytes=VMEM_LIMIT),
        name="attn_prompt",
    )(sinks, qkv, cos, sin, nw, ones_bd, *k_in, *v_in)


SAMPLE_BB = 16


def _attn_sample_body(x_ref, kc_ref, vc_ref, cos_ref, sin_ref, nw_ref, ones_ref, sinks_ref, *rest):
    o_ref, kout_ref, vout_ref = rest[-3:]
    x = x_ref[...]
    qk = _norm_rope(x[:, :QK_W], ones_ref[...], nw_ref[...], cos_ref[...], sin_ref[...])
    knew = qk[:, ATTN_Q:QK_W]
    vnew = x[:, QK_W:QKV_W]
    scale = 1.0 / math.sqrt(HEAD_DIM)
    row = lax.broadcasted_iota(jnp.int32, (N_HEADS, LANES), 0)
    lane = lax.broadcasted_iota(jnp.int32, (N_HEADS, LANES), 1)
    own = (lane < HEAD_DIM) == ((row & 1) == 0)
    lo1 = lax.broadcasted_iota(jnp.int32, (1, LANES), 1) < HEAD_DIM
    sinks = sinks_ref[...]
    units = range(SAMPLE_BB)
    qz, s_all, soft = {}, {}, {}
    for b in units:
        qrows = [qk[b:b + 1, (i // 2) * LANES:(i // 2 + 1) * LANES] for i in range(N_HEADS)]
        qz[b] = jnp.where(own, jnp.concatenate(qrows, axis=0) * scale, 0.0)
        s_all[b] = _dot_nt(qz[b].astype(BF16), kc_ref[b].astype(BF16))
    sink = sinks[:, 0:1]
    for b in units:
        s = s_all.pop(b)
        s_new = jnp.sum(qz.pop(b) * knew[b:b + 1, :], axis=1, keepdims=True)
        m = jnp.maximum(jnp.maximum(jnp.max(s, axis=1, keepdims=True), s_new), sink)
        p = jnp.exp(s - m)
        p_new = jnp.exp(s_new - m)
        soft[b] = (p.astype(BF16), p_new, jnp.sum(p, axis=1, keepdims=True) + p_new + jnp.exp(sink - m))
    o_rows = []
    for b in units:
        p, p_new, denom = soft.pop(b)
        pv = (_dot(p, vc_ref[b].astype(BF16)) + p_new * vnew[b:b + 1, :]) / denom
        o_rows.append(jnp.concatenate(
            [jnp.where(lo1, pv[2 * g:2 * g + 1, :], pv[2 * g + 1:2 * g + 2, :]) for g in range(GROUP)],
            axis=1))
    for b in units:
        kout_ref[b, 0:WINDOW - 1, :] = kc_ref[b, 1:WINDOW, :]
        kout_ref[b, WINDOW - 1:WINDOW, :] = knew[b:b + 1, :]
        vout_ref[b, 0:WINDOW - 1, :] = vc_ref[b, 1:WINDOW, :]
        vout_ref[b, WINDOW - 1:WINDOW, :] = vnew[b:b + 1, :]
    o_ref[...] = jnp.concatenate(o_rows, axis=0).astype(o_ref.dtype)


def _attn_sample(qkv, cache_k, cache_v, cos, sin, nw, ones_bd, sinks_b, oa, kstack, vstack,
                 l, row0, dbatch):
    bb = SAMPLE_BB
    r0 = row0 // bb
    kv_block = (bb, WINDOW, ATTN_KV)
    kv_idx = lambda i: (i, 0, 0)
    k_spec, k_sds, k_in, k_in_spec = _stacked_out(l, kstack, (dbatch, WINDOW, ATTN_KV), kv_block, kv_idx)
    v_spec, v_sds, v_in, v_in_spec = _stacked_out(l, vstack, (dbatch, WINDOW, ATTN_KV), kv_block, kv_idx)
    n_in = 9
    return pl.pallas_call(
        _attn_sample_body,
        grid=(dbatch // bb,),
        in_specs=[
            pl.BlockSpec((bb, QKV_W), lambda i: (r0 + i, 0)),
            pl.BlockSpec((None, bb, WINDOW, ATTN_KV), lambda i: (l, i, 0, 0)),
            pl.BlockSpec((None, bb, WINDOW, ATTN_KV), lambda i: (l, i, 0, 0)),
            _const_spec((1, LANES)),
            _const_spec((1, LANES)),
            _const_spec((1, QK_W)),
            _const_spec((LANES, LANES)),
            _const_spec((N_HEADS, LANES)),
            pl.BlockSpec(memory_space=pl.ANY),
        ] + k_in_spec + v_in_spec,
        out_specs=[pl.BlockSpec((bb, ATTN_Q), lambda i: (r0 + i, 0)), k_spec, v_spec],
        out_shape=[jax.ShapeDtypeStruct(oa.shape, oa.dtype), k_sds, v_sds],
        input_output_aliases={8: 0, n_in: 1, n_in + 1: 2} if l else {8: 0},
        compiler_params=pltpu.CompilerParams(
            dimension_semantics=("parallel",), vmem_limit_bytes=VMEM_LIMIT),
        name="attn_sample",
    )(qkv, cache_k, cache_v, cos, sin, nw, ones_bd, sinks_b, oa, *k_in, *v_in)


def _softplus(x):
    return jnp.maximum(x, 0.0) + jnp.log(1.0 + jnp.exp(-jnp.abs(x)))


def _silu(x):
    return x * jax.nn.sigmoid(x)


def _l2n(x):
    return x * lax.rsqrt(jnp.sum(x * x, axis=-1, keepdims=True) + EPS)


def _l2n_mxu(x, ones):
    hi, lo = _split2(x * x)
    return x * lax.rsqrt(_dot(hi, ones) + _dot(lo, ones) + EPS)


def _gate_rows(ba, alog_row, dtb_row):
    beta = jax.nn.sigmoid(ba)
    g = -jnp.exp(alog_row) * _softplus(ba + dtb_row)
    return beta, g


GROUP_TILES = 4
GROUP_TOK = GROUP_TILES * TILE
CHUNKS = TILE // DN_CHUNK


def _gdn_prompt_body(raw_ref, z_ref, ba_ref, cw_ref, alog_ref, dtb_ref, onw_ref,
                     ltri_ref, lall_ref, lvl_ref, *rest, groups_per_seq):
    od_ref, sout_ref, cout_ref, xp, s_scr, ub_scr, wq_scr, kq_scr, egl_scr = rest[-9:]
    i = pl.program_id(0)
    n_groups = pl.num_programs(0) - 1
    ga = jnp.minimum(i, n_groups - 1)
    gb = jnp.maximum(i - 1, 0)
    slot_a = i % 2
    slot_b = 1 - slot_a

    @pl.when(i == 0)
    def _():
        ub_scr[1] = jnp.zeros(ub_scr.shape[1:], F32)
        wq_scr[1] = jnp.zeros(wq_scr.shape[1:], BF16)
        kq_scr[1] = jnp.zeros(kq_scr.shape[1:], BF16)
        egl_scr[1] = jnp.zeros(egl_scr.shape[1:], F32)

    @pl.when(ga % groups_per_seq == 0)
    def _():
        xp[0:SUBLANES, :] = jnp.zeros((SUBLANES, CONV_DIM), F32)

    @pl.when(gb % groups_per_seq == 0)
    def _():
        s_scr[...] = jnp.zeros_like(s_scr)

    @pl.when(ga % groups_per_seq == groups_per_seq - 1)
    def _():
        cout_ref[...] = raw_ref[GROUP_TOK - (CONV_W - 1):GROUP_TOK, :]

    ltri = ltri_ref[...]
    lall = lall_ref[...]
    ri = lax.broadcasted_iota(jnp.int32, (TILE, TILE), 0)
    ci = lax.broadcasted_iota(jnp.int32, (TILE, TILE), 1)
    same = (ri // DN_CHUNK) == (ci // DN_CHUNK)
    causal = same & (ci <= ri)
    strict = same & (ci < ri)
    eye_bf = (ri == ci).astype(F32).astype(BF16)
    zeros_half = jnp.zeros((DN_CHUNK, DN_DV), BF16)
    onw = onw_ref[...]
    ones_sq = jnp.ones((DN_DK, LANES), BF16)
    units = [(j, h) for j in range(GROUP_TILES) for h in range(DN_HEADS)]

    st = [s_scr[h] for h in range(DN_HEADS)]
    b_live = {}

    def b_stage1(j, c):
        for h in range(DN_HEADS):
            idx = j * DN_HEADS + h
            r1 = _dot(wq_scr[slot_b, idx, 2 * c * DN_CHUNK:(2 * c + 2) * DN_CHUNK, :], st[h].astype(BF16))
            u = ub_scr[slot_b, idx, c * DN_CHUNK:(c + 1) * DN_CHUNK, :] - r1[:DN_CHUNK]
            parts = [zeros_half] * CHUNKS
            parts[c] = u.astype(BF16)
            b_live[h] = (jnp.concatenate(parts, axis=0), r1[DN_CHUNK:])

    def b_stage2(j, c):
        egl_f = egl_scr[slot_b, j]
        for h in range(DN_HEADS):
            idx = j * DN_HEADS + h
            u_full, oq = b_live[h]
            base = c * (DN_DK + DN_CHUNK)
            r2 = _dot(kq_scr[slot_b, idx, base:base + DN_DK + DN_CHUNK, :], u_full)
            o = oq + r2[DN_DK:]
            zh = z_ref[j * TILE + c * DN_CHUNK:j * TILE + (c + 1) * DN_CHUNK, h * DN_DV:(h + 1) * DN_DV]
            od_ref[j * TILE + c * DN_CHUNK:j * TILE + (c + 1) * DN_CHUNK, h * DN_DV:(h + 1) * DN_DV] = (
                _rms(o, onw) * _silu(zh)).astype(od_ref.dtype)
            st[h] = (st[h] * egl_f[c * DN_CHUNK:c * DN_CHUNK + 1, DN_HEADS + h:DN_HEADS + h + 1]
                     + r2[:DN_DK])

    b_stages = []
    for j in range(GROUP_TILES):
        for c in range(CHUNKS):
            b_stages.append(functools.partial(b_stage1, j, c))
            b_stages.append(functools.partial(b_stage2, j, c))
    b_iter = iter(b_stages)

    def b_step():
        f = next(b_iter, None)
        if f is not None:
            f()

    xp[SUBLANES:SUBLANES + GROUP_TOK, :] = raw_ref[...]
    ys = []
    for j in range(GROUP_TILES):
        r0 = SUBLANES + j * TILE
        y = xp[r0:r0 + TILE, :] * cw_ref[CONV_W - 1:CONV_W, :]
        for s in range(1, CONV_W):
            y = y + xp[r0 - s:r0 - s + TILE, :] * cw_ref[CONV_W - 1 - s:CONV_W - s, :]
        ys.append(_silu(y))
        b_step()
    xp[0:SUBLANES, :] = raw_ref[GROUP_TOK - SUBLANES:GROUP_TOK, :]

    tiles = []
    for j in range(GROUP_TILES):
        beta_f, g_f = _gate_rows(ba_ref[j * TILE:(j + 1) * TILE, :], alog_ref[...], dtb_ref[...])
        g_parts = _split3(g_f)
        gcum = sum(_dot(ltri, part) for part in g_parts)
        glast = sum(_dot(lall, part) for part in g_parts)
        egl_scr[slot_a, j] = jnp.exp(glast)
        tiles.append((beta_f, gcum, glast, gcum.T))

    u = {}
    for (j, h) in units:
        beta_f, gcum, glast, gcum_t = tiles[j]
        y = ys[j]
        k = _l2n_mxu(y[:, DN_QK + h * DN_DK:DN_QK + (h + 1) * DN_DK], ones_sq)
        gc = gcum[:, DN_HEADS + h:DN_HEADS + h + 1]
        gr = gcum_t[DN_HEADS + h:DN_HEADS + h + 1, :]
        u[j, h] = dict(k=k, kb=k.astype(BF16), beta=beta_f[:, h:h + 1], gc=gc,
                       gl=glast[:, DN_HEADS + h:DN_HEADS + h + 1],
                       decay=jnp.exp(jnp.where(causal, gc - gr, -jnp.inf)))
    for un in units:
        d = u[un]
        d["kk"] = _dot_nt(d["kb"], d["kb"])
    b_step()
    for un in units:
        d = u[un]
        a = jnp.where(strict, d["beta"] * d.pop("kk") * d["decay"], 0.0)
        d["ab"] = a.astype(BF16)
        d["tb"] = eye_bf - d["ab"] * lvl_ref[0]
    for lv in range(1, 6):
        for un in units:
            d = u[un]
            d["p"] = _dot(d["tb"], d["ab"] * lvl_ref[lv]).astype(BF16)
        b_step()
        for un in units:
            d = u[un]
            d["tb"] = d["tb"] - _dot(d.pop("p"), d["tb"]).astype(BF16)
        b_step()
    for (j, h) in units:
        d = u[j, h]
        y = ys[j]
        v = y[:, 2 * DN_QK + h * DN_DV:2 * DN_QK + (h + 1) * DN_DV]
        d["eg"] = jnp.exp(d["gc"])
        rhs = jnp.concatenate([v * d["beta"], d["k"] * (d["beta"] * d["eg"])], axis=1).astype(BF16)
        d["sol"] = _dot(d.pop("tb"), rhs)
    b_step()
    for (j, h) in units:
        d = u[j, h]
        y = ys[j]
        q = _l2n_mxu(y[:, h * DN_DK:(h + 1) * DN_DK], ones_sq) * (DN_DK ** -0.5)
        d["qk"] = (_dot_nt(q.astype(BF16), d["kb"]) * d["decay"]).astype(BF16)
        d["qd"] = (q * d["eg"]).astype(BF16)
    b_step()
    for _ in range(len(b_stages)):
        b_step()
    for (j, h) in units:
        d = u[j, h]
        idx = j * DN_HEADS + h
        sol = d["sol"]
        ub_scr[slot_a, idx] = sol[:, :DN_DV]
        w = sol[:, DN_DV:].astype(BF16)
        kd_t = (d["k"] * jnp.exp(d["gl"] - d["gc"])).T.astype(BF16)
        for c in range(CHUNKS):
            rows = slice(c * DN_CHUNK, (c + 1) * DN_CHUNK)
            wq_scr[slot_a, idx, 2 * c * DN_CHUNK:(2 * c + 1) * DN_CHUNK, :] = w[rows]
            wq_scr[slot_a, idx, (2 * c + 1) * DN_CHUNK:(2 * c + 2) * DN_CHUNK, :] = d["qd"][rows]
            base = c * (DN_DK + DN_CHUNK)
            kq_scr[slot_a, idx, base:base + DN_DK, :] = kd_t
            kq_scr[slot_a, idx, base + DN_DK:base + DN_DK + DN_CHUNK, :] = d["qk"][rows]

    for h in range(DN_HEADS):
        s_scr[h] = st[h]

    @pl.when(gb % groups_per_seq == groups_per_seq - 1)
    def _():
        for h in range(DN_HEADS):
            sout_ref[h] = st[h]


def _gdn_prompt(raw, z, ba, cw, alog_row, dtb_row, onw, consts, sstack, cstack, l, batch, seq, rows):
    gps = seq // GROUP_TOK
    ng = batch * gps
    s_spec, s_sds, s_in, s_in_spec = _stacked_out(
        l, sstack, (batch, DN_HEADS, DN_DK, DN_DV), (None, DN_HEADS, DN_DK, DN_DV),
        lambda i: (jnp.maximum(i - 1, 0) // gps, 0, 0, 0))
    c_spec, c_sds, c_in, c_in_spec = _stacked_out(
        l, cstack, (batch, CONV_W - 1, CONV_DIM), (None, CONV_W - 1, CONV_DIM),
        lambda i: (jnp.minimum(i, ng - 1) // gps, 0, 0))
    n_in = 10
    ltri, lall, lvl = consts
    nht = GROUP_TILES * DN_HEADS
    a_idx = lambda i: (jnp.minimum(i, ng - 1), 0)
    b_idx = lambda i: (jnp.maximum(i - 1, 0), 0)
    return pl.pallas_call(
        functools.partial(_gdn_prompt_body, groups_per_seq=gps),
        grid=(ng + 1,),
        in_specs=[
            pl.BlockSpec((GROUP_TOK, CONV_DIM), a_idx),
            pl.BlockSpec((GROUP_TOK, DN_V), b_idx),
            pl.BlockSpec((GROUP_TOK, LANES), a_idx),
            _layer_spec(l, (CONV_W, CONV_DIM)),
            _layer_spec(l, (1, LANES)),
            _layer_spec(l, (1, LANES)),
            _layer_spec(l, (1, DN_DV)),
            _const_spec((TILE, TILE)),
            _const_spec((TILE, TILE)),
            _const_spec((6, TILE, TILE)),
        ] + s_in_spec + c_in_spec,
        out_specs=[pl.BlockSpec((GROUP_TOK, DN_V), b_idx), s_spec, c_spec],
        out_shape=[jax.ShapeDtypeStruct((rows, DN_V), BF16), s_sds, c_sds],
        input_output_aliases={n_in: 1, n_in + 1: 2} if l else {},
        scratch_shapes=[
            pltpu.VMEM((SUBLANES + GROUP_TOK, CONV_DIM), F32),
            pltpu.VMEM((DN_HEADS, DN_DK, DN_DV), F32),
            pltpu.VMEM((2, nht, TILE, DN_DV), F32),
            pltpu.VMEM((2, nht, 2 * TILE, DN_DK), BF16),
            pltpu.VMEM((2, nht, CHUNKS * (DN_DK + DN_CHUNK), TILE), BF16),
            pltpu.VMEM((2, GROUP_TILES, TILE, LANES), F32),
        ],
        compiler_params=pltpu.CompilerParams(
            dimension_semantics=("arbitrary",), vmem_limit_bytes=VMEM_LIMIT),
        name="gdn_prompt",
    )(raw, z, ba, cw, alog_row, dtb_row, onw, ltri, lall, lvl, *s_in, *c_in)


def _gdn_sample_body(raw_ref, z_ref, ba_ref, cs_ref, st_ref, cw_ref, alog_ref, dtb_ref, onw_ref, *rest):
    od_ref, sout_ref, cout_ref = rest[-3:]
    bb = SAMPLE_BB
    raw = raw_ref[...]
    y = raw * cw_ref[CONV_W - 1:CONV_W, :]
    for i in range(CONV_W - 1):
        y = y + cs_ref[i] * cw_ref[i:i + 1, :]
    y = _silu(y)
    for i in range(CONV_W - 2):
        cout_ref[i] = cs_ref[i + 1]
    cout_ref[CONV_W - 2] = raw

    beta_f, g_f = _gate_rows(ba_ref[...], alog_ref[...], dtb_ref[...])
    eg_f = jnp.exp(g_f)
    pad = jnp.zeros((LANES - bb, DN_DK), F32)
    outs = []
    for h in range(DN_HEADS):
        q = _l2n(y[:, h * DN_DK:(h + 1) * DN_DK]) * (DN_DK ** -0.5)
        k = _l2n(y[:, DN_QK + h * DN_DK:DN_QK + (h + 1) * DN_DK])
        v = y[:, 2 * DN_QK + h * DN_DV:2 * DN_QK + (h + 1) * DN_DV]
        k_t = jnp.concatenate([k, pad], axis=0).T
        qk = jnp.sum(q * k, axis=1, keepdims=True)
        o_rows = []
        for b in range(bb):
            s1 = st_ref[b, h] * eg_f[b:b + 1, DN_HEADS + h:DN_HEADS + h + 1]
            kq = jnp.concatenate([k[b:b + 1, :], q[b:b + 1, :]], axis=0).astype(BF16)
            r = _dot(kq, s1.astype(BF16))
            delta = beta_f[b:b + 1, h:h + 1] * (v[b:b + 1, :] - r[0:1, :])
            sout_ref[b, h] = s1 + k_t[:, b:b + 1] * delta
            o_rows.append(r[1:2, :] + qk[b:b + 1, :] * delta)
        o = jnp.concatenate(o_rows, axis=0)
        zh = z_ref[:, h * DN_DV:(h + 1) * DN_DV]
        outs.append(_rms(o, onw_ref[...]) * _silu(zh))
    od_ref[...] = jnp.concatenate(outs, axis=1).astype(od_ref.dtype)


def _gdn_sample(raw, z, ba, conv_state, dn_state, cw, alog_row, dtb_row, onw, od, sstack, cstack,
                l, row0, dbatch):
    bb = SAMPLE_BB
    r0 = row0 // bb
    s_spec, s_sds, s_in, s_in_spec = _stacked_out(
        l, sstack, (dbatch, DN_HEADS, DN_DK, DN_DV), (bb, DN_HEADS, DN_DK, DN_DV), lambda i: (i, 0, 0, 0))
    c_spec, c_sds, c_in, c_in_spec = _stacked_out(
        l, cstack, (CONV_W - 1, dbatch, CONV_DIM), (CONV_W - 1, bb, CONV_DIM), lambda i: (0, i, 0))
    n_in = 10
    return pl.pallas_call(
        _gdn_sample_body,
        grid=(dbatch // bb,),
        in_specs=[
            pl.BlockSpec((bb, CONV_DIM), lambda i: (r0 + i, 0)),
            pl.BlockSpec((bb, DN_V), lambda i: (r0 + i, 0)),
            pl.BlockSpec((bb, LANES), lambda i: (r0 + i, 0)),
            pl.BlockSpec((None, CONV_W - 1, bb, CONV_DIM), lambda i: (l, 0, i, 0)),
            pl.BlockSpec((None, bb, DN_HEADS, DN_DK, DN_DV), lambda i: (l, i, 0, 0, 0)),
            _layer_spec(l, (CONV_W, CONV_DIM)),
            _layer_spec(l, (1, LANES)),
            _layer_spec(l, (1, LANES)),
            _layer_spec(l, (1, DN_DV)),
            pl.BlockSpec(memory_space=pl.ANY),
        ] + s_in_spec + c_in_spec,
        out_specs=[pl.BlockSpec((bb, DN_V), lambda i: (r0 + i, 0)), s_spec, c_spec],
        out_shape=[jax.ShapeDtypeStruct(od.shape, od.dtype), s_sds, c_sds],
        input_output_aliases={9: 0, n_in: 1, n_in + 1: 2} if l else {9: 0},
        compiler_params=pltpu.CompilerParams(
            dimension_semantics=("parallel",), vmem_limit_bytes=VMEM_LIMIT),
        name="gdn_sample",
    )(raw, z, ba, conv_state, dn_state, cw, alog_row, dtb_row, onw, od, *s_in, *c_in)


def _np_consts():
    i = np.arange(TILE)[:, None]
    j = np.arange(TILE)[None, :]
    same = (i // DN_CHUNK) == (j // DN_CHUNK)
    ltri = (same & (j <= i)).astype(np.float32)
    lall = same.astype(np.float32)
    lvls = []
    b = 1
    while b < DN_CHUNK:
        lvls.append(((i // (2 * b)) == (j // (2 * b))) & ((i // b) != (j // b)))
        b *= 2
    lvl = np.stack(lvls).astype(np.float32)
    hi = np.arange(LANES)
    ones_bd = (hi[:, None] // HEAD_DIM == hi[None, :] // HEAD_DIM).astype(np.float32)
    return ltri, lall, lvl, ones_bd


def _rope_tables(pos):
    half = HEAD_DIM // 2
    inv = 1.0 / (ROPE_THETA ** (jnp.arange(half, dtype=F32) / half))
    ang = pos.astype(F32)[:, None] * inv[None, :]
    cos, sin = jnp.cos(ang), jnp.sin(ang)
    cos_t = jnp.concatenate([cos, cos] * (LANES // HEAD_DIM), axis=1)
    sin_t = jnp.concatenate([-sin, sin] * (LANES // HEAD_DIM), axis=1)
    return cos_t, sin_t


def _pad_row(x, offset):
    out = jnp.zeros((x.shape[0], 1, LANES), F32)
    return out.at[:, 0, offset:offset + x.shape[1]].set(x.astype(F32))


def kernel(x_prompt, x_sample, cache_swa_k, cache_swa_v, state_dn, state_conv, ffn1_norm, ffn1_w_gate_up, ffn1_w_down, mix_norm, w_in, q_norm, k_norm, attn_sinks, conv_w, dn_A_log, dn_dt_bias, dn_out_norm, w_attn_o, w_dn_o, w_out, ffn2_norm, ffn2_w_gate_up, ffn2_w_down):
    batch, seq, _ = x_prompt.shape
    dbatch = x_sample.shape[0]
    prows = batch * seq
    rows = prows + dbatch
    perm = np.asarray(HEAD_PERM)

    w_in_b = w_in.astype(BF16)
    wq = w_in_b[:, :, :ATTN_Q].reshape(DEPTH, D_MODEL, N_HEADS, HEAD_DIM)[:, :, perm]
    wq = wq.reshape(DEPTH, D_MODEL, ATTN_Q)
    wpad = jnp.zeros((DEPTH, D_MODEL, LANES - 2 * DN_HEADS), BF16)
    w_ba = jnp.concatenate([w_in_b[:, :, IN_ALIGNED:IN_ALIGNED + 2 * DN_HEADS], wpad], axis=2)
    w_gates = w_in_b[:, :, IN_ALIGNED + 2 * DN_HEADS:]
    wao = w_attn_o.reshape(DEPTH, N_HEADS, HEAD_DIM, D_MODEL)[:, perm].reshape(DEPTH, ATTN_Q, D_MODEL).astype(BF16)
    wdo = w_dn_o.astype(BF16)
    wout = w_out.astype(BF16)
    wgu1, wd1 = ffn1_w_gate_up.astype(BF16), ffn1_w_down.astype(BF16)
    wgu2, wd2 = ffn2_w_gate_up.astype(BF16), ffn2_w_down.astype(BF16)
    n1 = ffn1_norm.reshape(DEPTH, 1, D_MODEL)
    n2 = ffn2_norm.reshape(DEPTH, 1, D_MODEL)
    nm = mix_norm.reshape(DEPTH, 1, D_MODEL)
    qk_nw = jnp.concatenate([jnp.tile(q_norm, (1, N_HEADS)), jnp.tile(k_norm, (1, N_KV_HEADS))], axis=1)
    alog_row = _pad_row(dn_A_log, DN_HEADS)
    dtb_row = _pad_row(dn_dt_bias, DN_HEADS)
    onw = dn_out_norm.reshape(DEPTH, 1, DN_DV)
    sinks_perm = attn_sinks[:, perm]
    sinks_b = jnp.broadcast_to(sinks_perm[:, :, None], (DEPTH, N_HEADS, LANES))

    ltri, lall, lvl, ones_bd = _np_consts()
    gdn_consts = (jnp.asarray(ltri, BF16), jnp.asarray(lall, BF16), jnp.asarray(lvl, BF16))
    ones_bd = jnp.asarray(ones_bd, BF16)
    cos_p, sin_p = _rope_tables(jnp.arange(seq))
    cos_s, sin_s = _rope_tables(PAST_LEN + jnp.arange(1))

    ck = cache_swa_k.reshape(DEPTH, dbatch, WINDOW, ATTN_KV)
    cv = cache_swa_v.reshape(DEPTH, dbatch, WINDOW, ATTN_KV)
    cs = jnp.transpose(state_conv, (0, 2, 1, 3))

    split_rows = _can_split_rows(prows, dbatch)
    xp2, xs2 = x_prompt.reshape(prows, D_MODEL), x_sample.reshape(dbatch, D_MODEL)
    kp = vp = sp = cp = ksn = vsn = ssn = csn = None
    for l in range(DEPTH):
        if l > 0:
            h = _ffn(h, n1, wgu1, wd1, l)
        elif split_rows:
            h = _ffn_first(xp2, xs2, n1, wgu1, wd1, l)
        else:
            h = _ffn(jnp.concatenate([xp2, xs2], axis=0), n1, wgu1, wd1, l)
        qkv, raw, z, ba, gates = _inproj(h, nm, w_in_b, wq, w_ba, w_gates, l)
        oa, kp, vp = _attn_prompt(qkv, attn_sinks[l], cos_p, sin_p, qk_nw[l:l + 1], ones_bd,
                                  kp, vp, l, batch, seq, rows)
        oa, ksn, vsn = _attn_sample(qkv, ck, cv, cos_s, sin_s, qk_nw[l:l + 1], ones_bd,
                                    sinks_b[l], oa, ksn, vsn, l, prows, dbatch)
        od, sp, cp = _gdn_prompt(raw, z, ba, conv_w, alog_row, dtb_row, onw, gdn_consts,
                                 sp, cp, l, batch, seq, rows)
        od, ssn, csn = _gdn_sample(raw, z, ba, cs, state_dn, conv_w, alog_row, dtb_row, onw,
                                   od, ssn, csn, l, prows, dbatch)
        final_split = (prows, dbatch) if (split_rows and l == DEPTH - 1) else None
        h = _merge_ffn(h, oa, od, gates, wao, wdo, wout, n2, wgu2, wd2, l, split=final_split)

    yp, ys = h if split_rows else (h[:prows], h[prows:])
    kv_shape = (DEPTH, -1, WINDOW, N_KV_HEADS, HEAD_DIM)
    return (yp.reshape(batch, seq, D_MODEL),
            ys.reshape(dbatch, 1, D_MODEL),
            kp.reshape(kv_shape), vp.reshape(kv_shape), sp, cp,
            ksn.reshape(kv_shape), vsn.reshape(kv_shape), ssn,
            jnp.transpose(csn, (0, 2, 1, 3)))
```

```python
import functools
import math

import numpy as np
import jax
import jax.numpy as jnp
from jax import lax
from jax.experimental import pallas as pl
from jax.experimental.pallas import tpu as pltpu

F32 = jnp.float32
BF16 = jnp.bfloat16

D_MODEL = 1024
DEPTH = 4
PAST_LEN = 8192
N_HEADS = 8
N_KV_HEADS = 2
GROUP = N_HEADS // N_KV_HEADS
HEAD_DIM = 64
WINDOW = 128
ATTN_BLOCK = 128
ROPE_THETA = 10000.0
DN_HEADS = 4
DN_DK = 128
DN_DV = 128
CONV_W = 4
DN_CHUNK = 64
D_FF = 2816
EPS = 1e-6
LOG2E = math.log2(math.e)

ATTN_Q = N_HEADS * HEAD_DIM
ATTN_KV = N_KV_HEADS * HEAD_DIM
DN_QK = DN_HEADS * DN_DK
DN_V = DN_HEADS * DN_DV
CONV_DIM = 2 * DN_QK + DN_V
QKV_W = ATTN_Q + 2 * ATTN_KV
QK_W = ATTN_Q + ATTN_KV
LANES = 128
SUBLANES = 8
TILE = 128
VMEM_LIMIT = 60 * 1024 * 1024

HEAD_PERM = (0, 4, 1, 5, 2, 6, 3, 7)
MXU_COLS = 256
FF_CHUNKS = (6 * MXU_COLS, 5 * MXU_COLS)


def _row_tile(rows, cap=512):
    best = SUBLANES
    for t in range(SUBLANES, cap + 1, SUBLANES):
        if rows % t == 0:
            best = t
    return best


def _rms(x, w):
    return x * lax.rsqrt(jnp.mean(x * x, axis=-1, keepdims=True) + EPS) * w


def _dot(a, b):
    return jnp.dot(a, b, preferred_element_type=F32)


def _dot_nt(a, b):
    return lax.dot_general(a, b, (((1,), (1,)), ((), ())), preferred_element_type=F32)


def _split2(x):
    hi = x.astype(BF16)
    return hi, (x - hi.astype(F32)).astype(BF16)


def _split3(x):
    hi = x.astype(BF16)
    r = x - hi.astype(F32)
    mid = r.astype(BF16)
    lo = (r - mid.astype(F32)).astype(BF16)
    return hi, mid, lo


def _const_spec(shape):
    nd = len(shape)
    return pl.BlockSpec(shape, lambda *_: (0,) * nd, pipeline_mode=pl.Buffered(1))


def _layer_spec(l, shape):
    nd = len(shape)
    return pl.BlockSpec((None,) + shape, lambda *_: (l,) + (0,) * nd, pipeline_mode=pl.Buffered(1))


def _stacked_out(l, prev, shape, block, index_map):
    spec = pl.BlockSpec((None,) + block, lambda *idx: (l,) + tuple(index_map(*idx)))
    sds = jax.ShapeDtypeStruct((DEPTH,) + shape, F32)
    if l == 0:
        return spec, sds, [], []
    return spec, sds, [prev], [pl.BlockSpec(memory_space=pl.ANY)]


def _swiglu_residual(x, nw, wgu_ref, wd_ref):
    xn = _rms(x, nw).astype(BF16)
    acc = jnp.zeros_like(x)
    c0 = 0
    for tf in FF_CHUNKS:
        g = _dot(xn, wgu_ref[:, c0:c0 + tf].astype(BF16))
        u = _dot(xn, wgu_ref[:, D_FF + c0:D_FF + c0 + tf].astype(BF16))
        a = (g * jax.nn.sigmoid(g) * u).astype(BF16)
        acc = acc + _dot(a, wd_ref[c0:c0 + tf, :].astype(BF16))
        c0 += tf
    return x + 0.5 * acc


def _ffn_body(x_ref, nw_ref, wgu_ref, wd_ref, o_ref):
    o_ref[...] = _swiglu_residual(x_ref[...], nw_ref[...], wgu_ref, wd_ref)


def _ffn_first_body(*refs):
    *piece_refs, xs_ref, nw_ref, wgu_ref, wd_ref, o_ref = refs
    last = pl.program_id(0) == pl.num_programs(0) - 1
    pieces = [r[...] for r in piece_refs]
    pieces[-1] = jnp.where(last, xs_ref[...], pieces[-1])
    o_ref[...] = _swiglu_residual(jnp.concatenate(pieces, axis=0), nw_ref[...], wgu_ref, wd_ref)


def _ffn_first(xp, xs, norm_w, wgu, wd, l):
    prows, dbatch = xp.shape[0], xs.shape[0]
    rows = prows + dbatch
    tm = _row_tile(rows)
    n_piece = tm // dbatch
    last_piece = prows // dbatch - 1
    piece_spec = lambda p: pl.BlockSpec(
        (dbatch, D_MODEL), lambda i: (jnp.minimum(i * n_piece + p, last_piece), 0))
    return pl.pallas_call(
        _ffn_first_body,
        grid=(rows // tm,),
        in_specs=[piece_spec(p) for p in range(n_piece)] + [
            _const_spec((dbatch, D_MODEL)),
            _layer_spec(l, (1, D_MODEL)),
            _layer_spec(l, (D_MODEL, 2 * D_FF)),
            _layer_spec(l, (D_FF, D_MODEL)),
        ],
        out_specs=pl.BlockSpec((tm, D_MODEL), lambda i: (i, 0)),
        out_shape=jax.ShapeDtypeStruct((rows, D_MODEL), F32),
        compiler_params=pltpu.CompilerParams(
            dimension_semantics=("parallel",), vmem_limit_bytes=VMEM_LIMIT),
        name="ffn_first",
    )(*([xp] * n_piece), xs, norm_w, wgu, wd)


def _can_split_rows(prows, dbatch):
    tm = _row_tile(prows + dbatch)
    return tm % dbatch == 0 and prows % dbatch == 0 and dbatch % SUBLANES == 0


def _ffn(x, norm_w, wgu, wd, l):
    rows = x.shape[0]
    tm = _row_tile(rows)
    return pl.pallas_call(
        _ffn_body,
        grid=(rows // tm,),
        in_specs=[
            pl.BlockSpec((tm, D_MODEL), lambda i: (i, 0)),
            _layer_spec(l, (1, D_MODEL)),
            _layer_spec(l, (D_MODEL, 2 * D_FF)),
            _layer_spec(l, (D_FF, D_MODEL)),
        ],
        out_specs=pl.BlockSpec((tm, D_MODEL), lambda i: (i, 0)),
        out_shape=jax.ShapeDtypeStruct((rows, D_MODEL), F32),
        compiler_params=pltpu.CompilerParams(
            dimension_semantics=("parallel",), vmem_limit_bytes=VMEM_LIMIT),
        name="ffn",
    )(x, norm_w, wgu, wd)


IN_SEGS = (QKV_W, CONV_DIM, DN_V, LANES, 2 * D_MODEL)
IN_DTYPES = (F32, F32, F32, F32, BF16)
IN_COLS = QKV_W + CONV_DIM + DN_V + 2 * DN_HEADS + 2 * D_MODEL
IN_ALIGNED = QKV_W + CONV_DIM + DN_V


def _inproj_body(h_ref, nw_ref, w_ref, wq_ref, wba_ref, wg_ref, qkv_ref, raw_ref, z_ref, ba_ref, gates_ref):
    u = _rms(h_ref[...], nw_ref[...]).astype(BF16)
    qkv_ref[:, :ATTN_Q] = _dot(u, wq_ref[...])
    qkv_ref[:, ATTN_Q:] = _dot(u, w_ref[:, ATTN_Q:QKV_W])
    raw_ref[...] = _dot(u, w_ref[:, QKV_W:QKV_W + CONV_DIM])
    z_ref[...] = _dot(u, w_ref[:, QKV_W + CONV_DIM:IN_ALIGNED])
    ba_ref[...] = _dot(u, wba_ref[...])
    gates_ref[...] = _dot(u, wg_ref[...]).astype(gates_ref.dtype)


def _inproj(h, norm_w, w_in, wq, wba, wg, l):
    rows = h.shape[0]
    tm = _row_tile(rows)
    return pl.pallas_call(
        _inproj_body,
        grid=(rows // tm,),
        in_specs=[
            pl.BlockSpec((tm, D_MODEL), lambda i: (i, 0)),
            _layer_spec(l, (1, D_MODEL)),
            _layer_spec(l, (D_MODEL, IN_COLS)),
            _layer_spec(l, (D_MODEL, ATTN_Q)),
            _layer_spec(l, (D_MODEL, LANES)),
            _layer_spec(l, (D_MODEL, 2 * D_MODEL)),
        ],
        out_specs=[pl.BlockSpec((tm, w), lambda i: (i, 0)) for w in IN_SEGS],
        out_shape=[jax.ShapeDtypeStruct((rows, w), dt) for w, dt in zip(IN_SEGS, IN_DTYPES)],
        compiler_params=pltpu.CompilerParams(
            dimension_semantics=("parallel",), vmem_limit_bytes=VMEM_LIMIT),
        name="inproj",
    )(h, norm_w, w_in, wq, wba, wg)


def _merge_ffn_body(h_ref, oa_ref, od_ref, gates_ref, wao_ref, wdo_ref, wout_ref,
                    nw_ref, wgu_ref, wd_ref, o_ref, *sample_out):
    br_a = _dot(oa_ref[...], wao_ref[...])
    br_d = _dot(od_ref[...], wdo_ref[...])
    ga = gates_ref[:, :D_MODEL].astype(F32)
    gd = gates_ref[:, D_MODEL:].astype(F32)
    m = jax.nn.sigmoid(ga) * br_a + jax.nn.sigmoid(gd) * br_d
    h = h_ref[...] + _dot(m.astype(BF16), wout_ref[...])
    y = _swiglu_residual(h, nw_ref[...], wgu_ref, wd_ref)
    o_ref[...] = y
    if sample_out:
        (os_ref,) = sample_out

        @pl.when(pl.program_id(0) == pl.num_programs(0) - 1)
        def _():
            os_ref[...] = y[y.shape[0] - os_ref.shape[0]:, :]


def _merge_ffn(h, oa, od, gates, wao, wdo, wout, norm_w, wgu, wd, l, split=None):
    rows = h.shape[0]
    tm = _row_tile(rows)
    out_specs = pl.BlockSpec((tm, D_MODEL), lambda i: (i, 0))
    out_shape = jax.ShapeDtypeStruct((rows, D_MODEL), F32)
    if split is not None:
        prows, dbatch = split
        out_specs = [out_specs, pl.BlockSpec((dbatch, D_MODEL), lambda i: (0, 0))]
        out_shape = [jax.ShapeDtypeStruct((prows, D_MODEL), F32), jax.ShapeDtypeStruct((dbatch, D_MODEL), F32)]
    return pl.pallas_call(
        _merge_ffn_body,
        grid=(rows // tm,),
        in_specs=[
            pl.BlockSpec((tm, D_MODEL), lambda i: (i, 0)),
            pl.BlockSpec((tm, ATTN_Q), lambda i: (i, 0)),
            pl.BlockSpec((tm, DN_V), lambda i: (i, 0)),
            pl.BlockSpec((tm, 2 * D_MODEL), lambda i: (i, 0)),
            _layer_spec(l, (ATTN_Q, D_MODEL)),
            _layer_spec(l, (DN_V, D_MODEL)),
            _layer_spec(l, (D_MODEL, D_MODEL)),
            _layer_spec(l, (1, D_MODEL)),
            _layer_spec(l, (D_MODEL, 2 * D_FF)),
            _layer_spec(l, (D_FF, D_MODEL)),
        ],
        out_specs=out_specs,
        out_shape=out_shape,
        compiler_params=pltpu.CompilerParams(
            dimension_semantics=("arbitrary",), vmem_limit_bytes=VMEM_LIMIT),
        name="merge_ffn",
    )(h, oa, od, gates, wao, wdo, wout, norm_w, wgu, wd)


def _norm_rope(x, ones_bd, nw, cos, sin):
    lane = lax.broadcasted_iota(jnp.int32, (x.shape[0], LANES), 1)
    first_half = (lane & (HEAD_DIM // 2)) == 0
    outs = []
    for g in range(x.shape[1] // LANES):
        xg = x[:, g * LANES:(g + 1) * LANES]
        hi, lo = _split2(xg * xg)
        ssq = _dot(hi, ones_bd) + _dot(lo, ones_bd)
        xn = xg * lax.rsqrt(ssq * (1.0 / HEAD_DIM) + EPS) * nw[:, g * LANES:(g + 1) * LANES]
        partner = jnp.where(first_half,
                            pltpu.roll(xn, LANES - HEAD_DIM // 2, 1),
                            pltpu.roll(xn, HEAD_DIM // 2, 1))
        outs.append(xn * cos + partner * sin)
    return jnp.concatenate(outs, axis=1)


ATTN_STEP_BLOCKS = 2


def _attn_prompt_body(sinks_ref, x_ref, cos_ref, sin_ref, nw_ref, ones_ref, *rest):
    o_ref, kout_ref, vout_ref, kprev, vprev = rest[-5:]
    step = pl.program_id(1)

    @pl.when(step == 0)
    def _():
        kprev[...] = jnp.zeros_like(kprev)
        vprev[...] = jnp.zeros_like(vprev)

    x = x_ref[...]
    qk = _norm_rope(x[:, :QK_W], ones_ref[...], nw_ref[...], cos_ref[...], sin_ref[...])
    k_all = qk[:, ATTN_Q:QK_W]
    v_all = x[:, QK_W:QKV_W]
    k_bf = jnp.concatenate([kprev[...], k_all.astype(BF16)], axis=0)
    v_bf = jnp.concatenate([vprev[...], v_all.astype(BF16)], axis=0)

    r = lax.broadcasted_iota(jnp.int32, (ATTN_BLOCK, 2 * ATTN_BLOCK), 0)
    c = lax.broadcasted_iota(jnp.int32, (ATTN_BLOCK, 2 * ATTN_BLOCK), 1)
    band = (c >= r) & (c <= r + WINDOW)
    first_col = jnp.where(step == 0, ATTN_BLOCK, 0)
    lo = lax.broadcasted_iota(jnp.int32, (ATTN_BLOCK, LANES), 1) < HEAD_DIM
    scale = LOG2E / math.sqrt(HEAD_DIM)
    klane = lax.broadcasted_iota(jnp.int32, k_bf.shape, 1) < HEAD_DIM
    k_half = (jnp.where(klane, k_bf, jnp.zeros_like(k_bf)), jnp.where(klane, jnp.zeros_like(k_bf), k_bf))

    units = [(sb, g, half) for sb in range(ATTN_STEP_BLOCKS) for g in range(GROUP) for half in range(2)]
    q_bf = {(sb, g): (qk[sb * ATTN_BLOCK:(sb + 1) * ATTN_BLOCK, g * LANES:(g + 1) * LANES] * scale).astype(BF16)
            for sb in range(ATTN_STEP_BLOCKS) for g in range(GROUP)}
    s_raw = {}
    for (sb, g, half) in units:
        s_raw[sb, g, half] = _dot_nt(q_bf[sb, g], k_half[half][sb * ATTN_BLOCK:(sb + 2) * ATTN_BLOCK])
    p_den = {}
    for (sb, g, half) in units:
        vis = band & (c >= first_col) if sb == 0 else band
        s = jnp.where(vis, s_raw.pop((sb, g, half)), -jnp.inf)
        sink = sinks_ref[HEAD_PERM[2 * g + half]] * LOG2E
        m = jnp.maximum(jnp.max(s, axis=1, keepdims=True), sink)
        p = jnp.exp2(s - m)
        p_den[sb, g, half] = (p.astype(BF16), jnp.sum(p, axis=1, keepdims=True) + jnp.exp2(sink - m))
    for sb in range(ATTN_STEP_BLOCKS):
        outs = []
        for g in range(GROUP):
            res = []
            for half in range(2):
                p, denom = p_den.pop((sb, g, half))
                res.append(_dot(p, v_bf[sb * ATTN_BLOCK:(sb + 2) * ATTN_BLOCK]) / denom)
            outs.append(jnp.where(lo, res[0], res[1]))
        o_ref[sb * ATTN_BLOCK:(sb + 1) * ATTN_BLOCK, :] = jnp.concatenate(outs, axis=1).astype(o_ref.dtype)

    last = slice((ATTN_STEP_BLOCKS - 1) * ATTN_BLOCK, ATTN_STEP_BLOCKS * ATTN_BLOCK)
    kprev[...] = k_all[last].astype(BF16)
    vprev[...] = v_all[last].astype(BF16)

    @pl.when(step == pl.num_programs(1) - 1)
    def _():
        kout_ref[...] = k_all[last]
        vout_ref[...] = v_all[last]


def _attn_prompt(qkv, sinks, cos, sin, nw, ones_bd, kstack, vstack, l, batch, seq, rows):
    rows_step = ATTN_STEP_BLOCKS * ATTN_BLOCK
    nb = seq // rows_step
    kv_block = (None, WINDOW, ATTN_KV)
    kv_idx = lambda b, j: (b, 0, 0)
    k_spec, k_sds, k_in, k_in_spec = _stacked_out(l, kstack, (batch, WINDOW, ATTN_KV), kv_block, kv_idx)
    v_spec, v_sds, v_in, v_in_spec = _stacked_out(l, vstack, (batch, WINDOW, ATTN_KV), kv_block, kv_idx)
    n_in = 6
    return pl.pallas_call(
        _attn_prompt_body,
        grid=(batch, nb),
        in_specs=[
            pl.BlockSpec(memory_space=pltpu.SMEM),
            pl.BlockSpec((rows_step, QKV_W), lambda b, j: (b * nb + j, 0)),
            pl.BlockSpec((rows_step, LANES), lambda b, j: (j, 0)),
            pl.BlockSpec((rows_step, LANES), lambda b, j: (j, 0)),
            _const_spec((1, QK_W)),
            _const_spec((LANES, LANES)),
        ] + k_in_spec + v_in_spec,
        out_specs=[pl.BlockSpec((rows_step, ATTN_Q), lambda b, j: (b * nb + j, 0)), k_spec, v_spec],
        out_shape=[jax.ShapeDtypeStruct((rows, ATTN_Q), BF16), k_sds, v_sds],
        input_output_aliases={n_in: 1, n_in + 1: 2} if l else {},
        scratch_shapes=[pltpu.VMEM((ATTN_BLOCK, ATTN_KV), BF16),
                        pltpu.VMEM((ATTN_BLOCK, ATTN_KV), BF16)],
        compiler_params=pltpu.CompilerParams(
            dimension_semantics=("parallel", "arbitrary"), vmem_limit_bytes=VMEM_LIMIT),
        name="attn_prompt",
    )(sinks, qkv, cos, sin, nw, ones_bd, *k_in, *v_in)


SAMPLE_BB = 16


def _attn_sample_body(x_ref, kc_ref, vc_ref, cos_ref, sin_ref, nw_ref, ones_ref, sinks_ref, *rest):
    o_ref, kout_ref, vout_ref = rest[-3:]
    x = x_ref[...]
    qk = _norm_rope(x[:, :QK_W], ones_ref[...], nw_ref[...], cos_ref[...], sin_ref[...])
    knew = qk[:, ATTN_Q:QK_W]
    vnew = x[:, QK_W:QKV_W]
    scale = 1.0 / math.sqrt(HEAD_DIM)
    row = lax.broadcasted_iota(jnp.int32, (N_HEADS, LANES), 0)
    lane = lax.broadcasted_iota(jnp.int32, (N_HEADS, LANES), 1)
    own = (lane < HEAD_DIM) == ((row & 1) == 0)
    lo1 = lax.broadcasted_iota(jnp.int32, (1, LANES), 1) < HEAD_DIM
    sinks = sinks_ref[...]
    units = range(SAMPLE_BB)
    qz, s_all, soft = {}, {}, {}
    for b in units:
        qrows = [qk[b:b + 1, (i // 2) * LANES:(i // 2 + 1) * LANES] for i in range(N_HEADS)]
        qz[b] = jnp.where(own, jnp.concatenate(qrows, axis=0) * scale, 0.0)
        s_all[b] = _dot(qz[b].astype(BF16), kc_ref[b].reshape(ATTN_KV, WINDOW).astype(BF16))
    sink = sinks[:, 0:1]
    for b in units:
        s = s_all.pop(b)
        s_new = jnp.sum(qz.pop(b) * knew[b:b + 1, :], axis=1, keepdims=True)
        m = jnp.maximum(jnp.maximum(jnp.max(s, axis=1, keepdims=True), s_new), sink)
        p = jnp.exp(s - m)
        p_new = jnp.exp(s_new - m)
        soft[b] = (p.astype(BF16), p_new, jnp.sum(p, axis=1, keepdims=True) + p_new + jnp.exp(sink - m))
    o_rows = []
    for b in units:
        p, p_new, denom = soft.pop(b)
        pv = (_dot_nt(p, vc_ref[b].reshape(ATTN_KV, WINDOW).astype(BF16)) + p_new * vnew[b:b + 1, :]) / denom
        o_rows.append(jnp.concatenate(
            [jnp.where(lo1, pv[2 * g:2 * g + 1, :], pv[2 * g + 1:2 * g + 2, :]) for g in range(GROUP)],
            axis=1))
    pad = jnp.zeros((LANES - SAMPLE_BB, ATTN_KV), F32)
    knew_t = jnp.concatenate([knew, pad], axis=0).T
    vnew_t = jnp.concatenate([vnew, pad], axis=0).T
    newest = lax.broadcasted_iota(jnp.int32, (ATTN_KV, WINDOW), 1) == WINDOW - 1
    kv_block = (N_KV_HEADS, HEAD_DIM, WINDOW)
    for b in units:
        for src_ref, new_t, dst_ref in ((kc_ref, knew_t, kout_ref), (vc_ref, vnew_t, vout_ref)):
            shifted = pltpu.roll(src_ref[b].reshape(ATTN_KV, WINDOW), WINDOW - 1, 1)
            dst_ref[b] = jnp.where(newest, new_t[:, b:b + 1], shifted).reshape(kv_block)
    o_ref[...] = jnp.concatenate(o_rows, axis=0).astype(o_ref.dtype)


def _attn_sample(qkv, cache_k, cache_v, cos, sin, nw, ones_bd, sinks_b, oa, kstack, vstack,
                 l, row0, dbatch):
    bb = SAMPLE_BB
    r0 = row0 // bb
    kv_block = (bb, N_KV_HEADS, HEAD_DIM, WINDOW)
    kv_idx = lambda i: (i, 0, 0, 0)
    kv_shape = (dbatch, N_KV_HEADS, HEAD_DIM, WINDOW)
    k_spec, k_sds, k_in, k_in_spec = _stacked_out(l, kstack, kv_shape, kv_block, kv_idx)
    v_spec, v_sds, v_in, v_in_spec = _stacked_out(l, vstack, kv_shape, kv_block, kv_idx)
    n_in = 9
    return pl.pallas_call(
        _attn_sample_body,
        grid=(dbatch // bb,),
        in_specs=[
            pl.BlockSpec((bb, QKV_W), lambda i: (r0 + i, 0)),
            pl.BlockSpec((None,) + kv_block, lambda i: (l, i, 0, 0, 0)),
            pl.BlockSpec((None,) + kv_block, lambda i: (l, i, 0, 0, 0)),
            _const_spec((1, LANES)),
            _const_spec((1, LANES)),
            _const_spec((1, QK_W)),
            _const_spec((LANES, LANES)),
            _const_spec((N_HEADS, LANES)),
            pl.BlockSpec(memory_space=pl.ANY),
        ] + k_in_spec + v_in_spec,
        out_specs=[pl.BlockSpec((bb, ATTN_Q), lambda i: (r0 + i, 0)), k_spec, v_spec],
        out_shape=[jax.ShapeDtypeStruct(oa.shape, oa.dtype), k_sds, v_sds],
        input_output_aliases={8: 0, n_in: 1, n_in + 1: 2} if l else {8: 0},
        compiler_params=pltpu.CompilerParams(
            dimension_semantics=("parallel",), vmem_limit_bytes=VMEM_LIMIT),
        name="attn_sample",
    )(qkv, cache_k, cache_v, cos, sin, nw, ones_bd, sinks_b, oa, *k_in, *v_in)


def _softplus(x):
    return jnp.maximum(x, 0.0) + jnp.log(1.0 + jnp.exp(-jnp.abs(x)))


def _silu(x):
    return x * jax.nn.sigmoid(x)


def _l2n(x):
    return x * lax.rsqrt(jnp.sum(x * x, axis=-1, keepdims=True) + EPS)


def _l2n_mxu(x, ones):
    hi, lo = _split2(x * x)
    return x * lax.rsqrt(_dot(hi, ones) + _dot(lo, ones) + EPS)


def _gate_rows(ba, alog_row, dtb_row):
    beta = jax.nn.sigmoid(ba)
    g = -jnp.exp(alog_row) * _softplus(ba + dtb_row)
    return beta, g


GROUP_TILES = 4
GROUP_TOK = GROUP_TILES * TILE
CHUNKS = TILE // DN_CHUNK


def _gdn_prompt_body(raw_ref, z_ref, ba_ref, cw_ref, alog_ref, dtb_ref, onw_ref,
                     ltri_ref, lall_ref, lvl_ref, *rest, groups_per_seq):
    od_ref, sout_ref, cout_ref, xp, s_scr, ub_scr, wq_scr, kq_scr, egl_scr = rest[-9:]
    i = pl.program_id(0)
    n_groups = pl.num_programs(0) - 1
    ga = jnp.minimum(i, n_groups - 1)
    gb = jnp.maximum(i - 1, 0)
    slot_a = i % 2
    slot_b = 1 - slot_a

    @pl.when(i == 0)
    def _():
        ub_scr[1] = jnp.zeros(ub_scr.shape[1:], F32)
        wq_scr[1] = jnp.zeros(wq_scr.shape[1:], BF16)
        kq_scr[1] = jnp.zeros(kq_scr.shape[1:], BF16)
        egl_scr[1] = jnp.zeros(egl_scr.shape[1:], F32)

    @pl.when(ga % groups_per_seq == 0)
    def _():
        xp[0:SUBLANES, :] = jnp.zeros((SUBLANES, CONV_DIM), F32)

    @pl.when(gb % groups_per_seq == 0)
    def _():
        s_scr[...] = jnp.zeros_like(s_scr)

    @pl.when(ga % groups_per_seq == groups_per_seq - 1)
    def _():
        cout_ref[...] = raw_ref[GROUP_TOK - (CONV_W - 1):GROUP_TOK, :]

    ltri = ltri_ref[...]
    lall = lall_ref[...]
    ri = lax.broadcasted_iota(jnp.int32, (TILE, TILE), 0)
    ci = lax.broadcasted_iota(jnp.int32, (TILE, TILE), 1)
    same = (ri // DN_CHUNK) == (ci // DN_CHUNK)
    causal = same & (ci <= ri)
    strict = same & (ci < ri)
    eye_bf = (ri == ci).astype(F32).astype(BF16)
    zeros_half = jnp.zeros((DN_CHUNK, DN_DV), BF16)
    onw = onw_ref[...]
    ones_sq = jnp.ones((DN_DK, LANES), BF16)
    units = [(j, h) for j in range(GROUP_TILES) for h in range(DN_HEADS)]

    st = [s_scr[h] for h in range(DN_HEADS)]
    b_live = {}

    def b_stage1(j, c):
        for h in range(DN_HEADS):
            idx = j * DN_HEADS + h
            r1 = _dot(wq_scr[slot_b, idx, 2 * c * DN_CHUNK:(2 * c + 2) * DN_CHUNK, :], st[h].astype(BF16))
            u = ub_scr[slot_b, idx, c * DN_CHUNK:(c + 1) * DN_CHUNK, :] - r1[:DN_CHUNK]
            parts = [zeros_half] * CHUNKS
            parts[c] = u.astype(BF16)
            b_live[h] = (jnp.concatenate(parts, axis=0), r1[DN_CHUNK:])

    def b_stage2(j, c):
        egl_f = egl_scr[slot_b, j]
        for h in range(DN_HEADS):
            idx = j * DN_HEADS + h
            u_full, oq = b_live[h]
            base = c * (DN_DK + DN_CHUNK)
            r2 = _dot(kq_scr[slot_b, idx, base:base + DN_DK + DN_CHUNK, :], u_full)
            o = oq + r2[DN_DK:]
            zh = z_ref[j * TILE + c * DN_CHUNK:j * TILE + (c + 1) * DN_CHUNK, h * DN_DV:(h + 1) * DN_DV]
            od_ref[j * TILE + c * DN_CHUNK:j * TILE + (c + 1) * DN_CHUNK, h * DN_DV:(h + 1) * DN_DV] = (
                _rms(o, onw) * _silu(zh)).astype(od_ref.dtype)
            st[h] = (st[h] * egl_f[c * DN_CHUNK:c * DN_CHUNK + 1, DN_HEADS + h:DN_HEADS + h + 1]
                     + r2[:DN_DK])

    b_stages = []
    for j in range(GROUP_TILES):
        for c in range(CHUNKS):
            b_stages.append(functools.partial(b_stage1, j, c))
            b_stages.append(functools.partial(b_stage2, j, c))
    b_iter = iter(b_stages)

    def b_step():
        f = next(b_iter, None)
        if f is not None:
            f()

    xp[SUBLANES:SUBLANES + GROUP_TOK, :] = raw_ref[...]
    ys = []
    for j in range(GROUP_TILES):
        r0 = SUBLANES + j * TILE
        y = xp[r0:r0 + TILE, :] * cw_ref[CONV_W - 1:CONV_W, :]
        for s in range(1, CONV_W):
            y = y + xp[r0 - s:r0 - s + TILE, :] * cw_ref[CONV_W - 1 - s:CONV_W - s, :]
        ys.append(_silu(y))
        b_step()
    xp[0:SUBLANES, :] = raw_ref[GROUP_TOK - SUBLANES:GROUP_TOK, :]

    tiles = []
    for j in range(GROUP_TILES):
        beta_f, g_f = _gate_rows(ba_ref[j * TILE:(j + 1) * TILE, :], alog_ref[...], dtb_ref[...])
        g_parts = _split3(g_f)
        gcum = sum(_dot(ltri, part) for part in g_parts)
        glast = sum(_dot(lall, part) for part in g_parts)
        egl_scr[slot_a, j] = jnp.exp(glast)
        tiles.append((beta_f, gcum, glast, gcum.T))

    u = {}
    for (j, h) in units:
        beta_f, gcum, glast, gcum_t = tiles[j]
        y = ys[j]
        k = _l2n_mxu(y[:, DN_QK + h * DN_DK:DN_QK + (h + 1) * DN_DK], ones_sq)
        gc = gcum[:, DN_HEADS + h:DN_HEADS + h + 1]
        gr = gcum_t[DN_HEADS + h:DN_HEADS + h + 1, :]
        u[j, h] = dict(k=k, kb=k.astype(BF16), beta=beta_f[:, h:h + 1], gc=gc,
                       gl=glast[:, DN_HEADS + h:DN_HEADS + h + 1],
                       decay=jnp.exp(jnp.where(causal, gc - gr, -jnp.inf)))
    for un in units:
        d = u[un]
        d["kk"] = _dot_nt(d["kb"], d["kb"])
    b_step()
    for un in units:
        d = u[un]
        a = jnp.where(strict, d["beta"] * d.pop("kk") * d["decay"], 0.0)
        d["ab"] = a.astype(BF16)
        d["tb"] = eye_bf - d["ab"] * lvl_ref[0]
    for lv in range(1, 6):
        for un in units:
            d = u[un]
            d["p"] = _dot(d["tb"], d["ab"] * lvl_ref[lv]).astype(BF16)
        b_step()
        for un in units:
            d = u[un]
            d["tb"] = d["tb"] - _dot(d.pop("p"), d["tb"]).astype(BF16)
        b_step()
    for (j, h) in units:
        d = u[j, h]
        y = ys[j]
        v = y[:, 2 * DN_QK + h * DN_DV:2 * DN_QK + (h + 1) * DN_DV]
        d["eg"] = jnp.exp(d["gc"])
        rhs = jnp.concatenate([v * d["beta"], d["k"] * (d["beta"] * d["eg"])], axis=1).astype(BF16)
        d["sol"] = _dot(d.pop("tb"), rhs)
    b_step()
    for (j, h) in units:
        d = u[j, h]
        y = ys[j]
        q = _l2n_mxu(y[:, h * DN_DK:(h + 1) * DN_DK], ones_sq) * (DN_DK ** -0.5)
        d["qk"] = (_dot_nt(q.astype(BF16), d["kb"]) * d["decay"]).astype(BF16)
        d["qd"] = (q * d["eg"]).astype(BF16)
    b_step()
    for _ in range(len(b_stages)):
        b_step()
    for (j, h) in units:
        d = u[j, h]
        idx = j * DN_HEADS + h
        sol = d["sol"]
        ub_scr[slot_a, idx] = sol[:, :DN_DV]
        w = sol[:, DN_DV:].astype(BF16)
        kd_t = (d["k"] * jnp.exp(d["gl"] - d["gc"])).T.astype(BF16)
        for c in range(CHUNKS):
            rows = slice(c * DN_CHUNK, (c + 1) * DN_CHUNK)
            wq_scr[slot_a, idx, 2 * c * DN_CHUNK:(2 * c + 1) * DN_CHUNK, :] = w[rows]
            wq_scr[slot_a, idx, (2 * c + 1) * DN_CHUNK:(2 * c + 2) * DN_CHUNK, :] = d["qd"][rows]
            base = c * (DN_DK + DN_CHUNK)
            kq_scr[slot_a, idx, base:base + DN_DK, :] = kd_t
            kq_scr[slot_a, idx, base + DN_DK:base + DN_DK + DN_CHUNK, :] = d["qk"][rows]

    for h in range(DN_HEADS):
        s_scr[h] = st[h]

    @pl.when(gb % groups_per_seq == groups_per_seq - 1)
    def _():
        for h in range(DN_HEADS):
            sout_ref[h] = st[h]


def _gdn_prompt(raw, z, ba, cw, alog_row, dtb_row, onw, consts, sstack, cstack, l, batch, seq, rows):
    gps = seq // GROUP_TOK
    ng = batch * gps
    s_spec, s_sds, s_in, s_in_spec = _stacked_out(
        l, sstack, (batch, DN_HEADS, DN_DK, DN_DV), (None, DN_HEADS, DN_DK, DN_DV),
        lambda i: (jnp.maximum(i - 1, 0) // gps, 0, 0, 0))
    c_spec, c_sds, c_in, c_in_spec = _stacked_out(
        l, cstack, (batch, CONV_W - 1, CONV_DIM), (None, CONV_W - 1, CONV_DIM),
        lambda i: (jnp.minimum(i, ng - 1) // gps, 0, 0))
    n_in = 10
    ltri, lall, lvl = consts
    nht = GROUP_TILES * DN_HEADS
    a_idx = lambda i: (jnp.minimum(i, ng - 1), 0)
    b_idx = lambda i: (jnp.maximum(i - 1, 0), 0)
    return pl.pallas_call(
        functools.partial(_gdn_prompt_body, groups_per_seq=gps),
        grid=(ng + 1,),
        in_specs=[
            pl.BlockSpec((GROUP_TOK, CONV_DIM), a_idx),
            pl.BlockSpec((GROUP_TOK, DN_V), b_idx),
            pl.BlockSpec((GROUP_TOK, LANES), a_idx),
            _layer_spec(l, (CONV_W, CONV_DIM)),
            _layer_spec(l, (1, LANES)),
            _layer_spec(l, (1, LANES)),
            _layer_spec(l, (1, DN_DV)),
            _const_spec((TILE, TILE)),
            _const_spec((TILE, TILE)),
            _const_spec((6, TILE, TILE)),
        ] + s_in_spec + c_in_spec,
        out_specs=[pl.BlockSpec((GROUP_TOK, DN_V), b_idx), s_spec, c_spec],
        out_shape=[jax.ShapeDtypeStruct((rows, DN_V), BF16), s_sds, c_sds],
        input_output_aliases={n_in: 1, n_in + 1: 2} if l else {},
        scratch_shapes=[
            pltpu.VMEM((SUBLANES + GROUP_TOK, CONV_DIM), F32),
            pltpu.VMEM((DN_HEADS, DN_DK, DN_DV), F32),
            pltpu.VMEM((2, nht, TILE, DN_DV), F32),
            pltpu.VMEM((2, nht, 2 * TILE, DN_DK), BF16),
            pltpu.VMEM((2, nht, CHUNKS * (DN_DK + DN_CHUNK), TILE), BF16),
            pltpu.VMEM((2, GROUP_TILES, TILE, LANES), F32),
        ],
        compiler_params=pltpu.CompilerParams(
            dimension_semantics=("arbitrary",), vmem_limit_bytes=VMEM_LIMIT),
        name="gdn_prompt",
    )(raw, z, ba, cw, alog_row, dtb_row, onw, ltri, lall, lvl, *s_in, *c_in)


def _gdn_sample_body(raw_ref, z_ref, ba_ref, cs_ref, st_ref, cw_ref, alog_ref, dtb_ref, onw_ref, *rest):
    od_ref, sout_ref, cout_ref = rest[-3:]
    bb = SAMPLE_BB
    raw = raw_ref[...]
    y = raw * cw_ref[CONV_W - 1:CONV_W, :]
    for i in range(CONV_W - 1):
        y = y + cs_ref[i] * cw_ref[i:i + 1, :]
    y = _silu(y)
    for i in range(CONV_W - 2):
        cout_ref[i] = cs_ref[i + 1]
    cout_ref[CONV_W - 2] = raw

    beta_f, g_f = _gate_rows(ba_ref[...], alog_ref[...], dtb_ref[...])
    eg_f = jnp.exp(g_f)
    pad = jnp.zeros((LANES - bb, DN_DK), F32)
    outs = []
    for h in range(DN_HEADS):
        q = _l2n(y[:, h * DN_DK:(h + 1) * DN_DK]) * (DN_DK ** -0.5)
        k = _l2n(y[:, DN_QK + h * DN_DK:DN_QK + (h + 1) * DN_DK])
        v = y[:, 2 * DN_QK + h * DN_DV:2 * DN_QK + (h + 1) * DN_DV]
        k_t = jnp.concatenate([k, pad], axis=0).T
        qk = jnp.sum(q * k, axis=1, keepdims=True)
        o_rows = []
        for b in range(bb):
            s1 = st_ref[b, h] * eg_f[b:b + 1, DN_HEADS + h:DN_HEADS + h + 1]
            kq = jnp.concatenate([k[b:b + 1, :], q[b:b + 1, :]], axis=0).astype(BF16)
            r = _dot(kq, s1.astype(BF16))
            delta = beta_f[b:b + 1, h:h + 1] * (v[b:b + 1, :] - r[0:1, :])
            sout_ref[b, h] = s1 + k_t[:, b:b + 1] * delta
            o_rows.append(r[1:2, :] + qk[b:b + 1, :] * delta)
        o = jnp.concatenate(o_rows, axis=0)
        zh = z_ref[:, h * DN_DV:(h + 1) * DN_DV]
        outs.append(_rms(o, onw_ref[...]) * _silu(zh))
    od_ref[...] = jnp.concatenate(outs, axis=1).astype(od_ref.dtype)


def _gdn_sample(raw, z, ba, conv_state, dn_state, cw, alog_row, dtb_row, onw, od, sstack, cstack,
                l, row0, dbatch):
    bb = SAMPLE_BB
    r0 = row0 // bb
    s_spec, s_sds, s_in, s_in_spec = _stacked_out(
        l, sstack, (dbatch, DN_HEADS, DN_DK, DN_DV), (bb, DN_HEADS, DN_DK, DN_DV), lambda i: (i, 0, 0, 0))
    c_spec, c_sds, c_in, c_in_spec = _stacked_out(
        l, cstack, (CONV_W - 1, dbatch, CONV_DIM), (CONV_W - 1, bb, CONV_DIM), lambda i: (0, i, 0))
    n_in = 10
    return pl.pallas_call(
        _gdn_sample_body,
        grid=(dbatch // bb,),
        in_specs=[
            pl.BlockSpec((bb, CONV_DIM), lambda i: (r0 + i, 0)),
            pl.BlockSpec((bb, DN_V), lambda i: (r0 + i, 0)),
            pl.BlockSpec((bb, LANES), lambda i: (r0 + i, 0)),
            pl.BlockSpec((None, CONV_W - 1, bb, CONV_DIM), lambda i: (l, 0, i, 0)),
            pl.BlockSpec((None, bb, DN_HEADS, DN_DK, DN_DV), lambda i: (l, i, 0, 0, 0)),
            _layer_spec(l, (CONV_W, CONV_DIM)),
            _layer_spec(l, (1, LANES)),
            _layer_spec(l, (1, LANES)),
            _layer_spec(l, (1, DN_DV)),
            pl.BlockSpec(memory_space=pl.ANY),
        ] + s_in_spec + c_in_spec,
        out_specs=[pl.BlockSpec((bb, DN_V), lambda i: (r0 + i, 0)), s_spec, c_spec],
        out_shape=[jax.ShapeDtypeStruct(od.shape, od.dtype), s_sds, c_sds],
        input_output_aliases={9: 0, n_in: 1, n_in + 1: 2} if l else {9: 0},
        compiler_params=pltpu.CompilerParams(
            dimension_semantics=("parallel",), vmem_limit_bytes=VMEM_LIMIT),
        name="gdn_sample",
    )(raw, z, ba, conv_state, dn_state, cw, alog_row, dtb_row, onw, od, *s_in, *c_in)


def _np_consts():
    i = np.arange(TILE)[:, None]
    j = np.arange(TILE)[None, :]
    same = (i // DN_CHUNK) == (j // DN_CHUNK)
    ltri = (same & (j <= i)).astype(np.float32)
    lall = same.astype(np.float32)
    lvls = []
    b = 1
    while b < DN_CHUNK:
        lvls.append(((i // (2 * b)) == (j // (2 * b))) & ((i // b) != (j // b)))
        b *= 2
    lvl = np.stack(lvls).astype(np.float32)
    hi = np.arange(LANES)
    ones_bd = (hi[:, None] // HEAD_DIM == hi[None, :] // HEAD_DIM).astype(np.float32)
    return ltri, lall, lvl, ones_bd


def _rope_tables(pos):
    half = HEAD_DIM // 2
    inv = 1.0 / (ROPE_THETA ** (jnp.arange(half, dtype=F32) / half))
    ang = pos.astype(F32)[:, None] * inv[None, :]
    cos, sin = jnp.cos(ang), jnp.sin(ang)
    cos_t = jnp.concatenate([cos, cos] * (LANES // HEAD_DIM), axis=1)
    sin_t = jnp.concatenate([-sin, sin] * (LANES // HEAD_DIM), axis=1)
    return cos_t, sin_t


def _pad_row(x, offset):
    out = jnp.zeros((x.shape[0], 1, LANES), F32)
    return out.at[:, 0, offset:offset + x.shape[1]].set(x.astype(F32))


def kernel(x_prompt, x_sample, cache_swa_k, cache_swa_v, state_dn, state_conv, ffn1_norm, ffn1_w_gate_up, ffn1_w_down, mix_norm, w_in, q_norm, k_norm, attn_sinks, conv_w, dn_A_log, dn_dt_bias, dn_out_norm, w_attn_o, w_dn_o, w_out, ffn2_norm, ffn2_w_gate_up, ffn2_w_down):
    batch, seq, _ = x_prompt.shape
    dbatch = x_sample.shape[0]
    prows = batch * seq
    rows = prows + dbatch
    perm = np.asarray(HEAD_PERM)

    w_in_b = w_in.astype(BF16)
    wq = w_in_b[:, :, :ATTN_Q].reshape(DEPTH, D_MODEL, N_HEADS, HEAD_DIM)[:, :, perm]
    wq = wq.reshape(DEPTH, D_MODEL, ATTN_Q)
    wpad = jnp.zeros((DEPTH, D_MODEL, LANES - 2 * DN_HEADS), BF16)
    w_ba = jnp.concatenate([w_in_b[:, :, IN_ALIGNED:IN_ALIGNED + 2 * DN_HEADS], wpad], axis=2)
    w_gates = w_in_b[:, :, IN_ALIGNED + 2 * DN_HEADS:]
    wao = w_attn_o.reshape(DEPTH, N_HEADS, HEAD_DIM, D_MODEL)[:, perm].reshape(DEPTH, ATTN_Q, D_MODEL).astype(BF16)
    wdo = w_dn_o.astype(BF16)
    wout = w_out.astype(BF16)
    wgu1, wd1, wgu2, wd2 = ffn1_w_gate_up, ffn1_w_down, ffn2_w_gate_up, ffn2_w_down
    n1 = ffn1_norm.reshape(DEPTH, 1, D_MODEL)
    n2 = ffn2_norm.reshape(DEPTH, 1, D_MODEL)
    nm = mix_norm.reshape(DEPTH, 1, D_MODEL)
    qk_nw = jnp.concatenate([jnp.tile(q_norm, (1, N_HEADS)), jnp.tile(k_norm, (1, N_KV_HEADS))], axis=1)
    alog_row = _pad_row(dn_A_log, DN_HEADS)
    dtb_row = _pad_row(dn_dt_bias, DN_HEADS)
    onw = dn_out_norm.reshape(DEPTH, 1, DN_DV)
    sinks_perm = attn_sinks[:, perm]
    sinks_b = jnp.broadcast_to(sinks_perm[:, :, None], (DEPTH, N_HEADS, LANES))

    ltri, lall, lvl, ones_bd = _np_consts()
    gdn_consts = (jnp.asarray(ltri, BF16), jnp.asarray(lall, BF16), jnp.asarray(lvl, BF16))
    ones_bd = jnp.asarray(ones_bd, BF16)
    cos_p, sin_p = _rope_tables(jnp.arange(seq))
    cos_s, sin_s = _rope_tables(PAST_LEN + jnp.arange(1))

    ck = jnp.transpose(cache_swa_k, (0, 1, 3, 4, 2))
    cv = jnp.transpose(cache_swa_v, (0, 1, 3, 4, 2))
    cs = jnp.transpose(state_conv, (0, 2, 1, 3))

    split_rows = _can_split_rows(prows, dbatch)
    xp2, xs2 = x_prompt.reshape(prows, D_MODEL), x_sample.reshape(dbatch, D_MODEL)
    kp = vp = sp = cp = ksn = vsn = ssn = csn = None
    for l in range(DEPTH):
        if l > 0:
            h = _ffn(h, n1, wgu1, wd1, l)
        elif split_rows:
            h = _ffn_first(xp2, xs2, n1, wgu1, wd1, l)
        else:
            h = _ffn(jnp.concatenate([xp2, xs2], axis=0), n1, wgu1, wd1, l)
        qkv, raw, z, ba, gates = _inproj(h, nm, w_in_b, wq, w_ba, w_gates, l)
        oa, kp, vp = _attn_prompt(qkv, attn_sinks[l], cos_p, sin_p, qk_nw[l:l + 1], ones_bd,
                                  kp, vp, l, batch, seq, rows)
        oa, ksn, vsn = _attn_sample(qkv, ck, cv, cos_s, sin_s, qk_nw[l:l + 1], ones_bd,
                                    sinks_b[l], oa, ksn, vsn, l, prows, dbatch)
        od, sp, cp = _gdn_prompt(raw, z, ba, conv_w, alog_row, dtb_row, onw, gdn_consts,
                                 sp, cp, l, batch, seq, rows)
        od, ssn, csn = _gdn_sample(raw, z, ba, cs, state_dn, conv_w, alog_row, dtb_row, onw,
                                   od, ssn, csn, l, prows, dbatch)
        final_split = (prows, dbatch) if (split_rows and l == DEPTH - 1) else None
        h = _merge_ffn(h, oa, od, gates, wao, wdo, wout, n2, wgu2, wd2, l, split=final_split)

    yp, ys = h if split_rows else (h[:prows], h[prows:])
    kv_shape = (DEPTH, -1, WINDOW, N_KV_HEADS, HEAD_DIM)
    return (yp.reshape(batch, seq, D_MODEL),
            ys.reshape(dbatch, 1, D_MODEL),
            kp.reshape(kv_shape), vp.reshape(kv_shape), sp, cp,
            jnp.transpose(ksn, (0, 1, 4, 2, 3)), jnp.transpose(vsn, (0, 1, 4, 2, 3)), ssn,
            jnp.transpose(csn, (0, 2, 1, 3)))
```

```python
import functools
import math

import numpy as np
import jax
import jax.numpy as jnp
from jax import lax
from jax.experimental import pallas as pl
from jax.experimental.pallas import tpu as pltpu

F32 = jnp.float32
BF16 = jnp.bfloat16

D_MODEL = 1024
DEPTH = 4
PAST_LEN = 8192
N_HEADS = 8
N_KV_HEADS = 2
GROUP = N_HEADS // N_KV_HEADS
HEAD_DIM = 64
WINDOW = 128
ATTN_BLOCK = 128
ROPE_THETA = 10000.0
DN_HEADS = 4
DN_DK = 128
DN_DV = 128
CONV_W = 4
DN_CHUNK = 64
D_FF = 2816
EPS = 1e-6
LOG2E = math.log2(math.e)

ATTN_Q = N_HEADS * HEAD_DIM
ATTN_KV = N_KV_HEADS * HEAD_DIM
DN_QK = DN_HEADS * DN_DK
DN_V = DN_HEADS * DN_DV
CONV_DIM = 2 * DN_QK + DN_V
QKV_W = ATTN_Q + 2 * ATTN_KV
QK_W = ATTN_Q + ATTN_KV
LANES = 128
SUBLANES = 8
TILE = 128
VMEM_LIMIT = 60 * 1024 * 1024

HEAD_PERM = (0, 4, 1, 5, 2, 6, 3, 7)
MXU_COLS = 256
FF_CHUNKS = (6 * MXU_COLS, 5 * MXU_COLS)


def _row_tile(rows, cap=512):
    best = SUBLANES
    for t in range(SUBLANES, cap + 1, SUBLANES):
        if rows % t == 0:
            best = t
    return best


def _rms(x, w):
    return x * lax.rsqrt(jnp.mean(x * x, axis=-1, keepdims=True) + EPS) * w


def _dot(a, b):
    return jnp.dot(a, b, preferred_element_type=F32)


def _dot_nt(a, b):
    return lax.dot_general(a, b, (((1,), (1,)), ((), ())), preferred_element_type=F32)


def _split2(x):
    hi = x.astype(BF16)
    return hi, (x - hi.astype(F32)).astype(BF16)


def _split3(x):
    hi = x.astype(BF16)
    r = x - hi.astype(F32)
    mid = r.astype(BF16)
    lo = (r - mid.astype(F32)).astype(BF16)
    return hi, mid, lo


def _const_spec(shape):
    nd = len(shape)
    return pl.BlockSpec(shape, lambda *_: (0,) * nd, pipeline_mode=pl.Buffered(1))


def _layer_spec(l, shape):
    nd = len(shape)
    return pl.BlockSpec((None,) + shape, lambda *_: (l,) + (0,) * nd, pipeline_mode=pl.Buffered(1))


def _stacked_out(l, prev, shape, block, index_map):
    spec = pl.BlockSpec((None,) + block, lambda *idx: (l,) + tuple(index_map(*idx)))
    sds = jax.ShapeDtypeStruct((DEPTH,) + shape, F32)
    if l == 0:
        return spec, sds, [], []
    return spec, sds, [prev], [pl.BlockSpec(memory_space=pl.ANY)]


def _swiglu_residual(x, nw, wgu_ref, wd_ref):
    xn = _rms(x, nw).astype(BF16)
    acc = jnp.zeros_like(x)
    c0 = 0
    for tf in FF_CHUNKS:
        g = _dot(xn, wgu_ref[:, c0:c0 + tf].astype(BF16))
        u = _dot(xn, wgu_ref[:, D_FF + c0:D_FF + c0 + tf].astype(BF16))
        a = (g * jax.nn.sigmoid(g) * u).astype(BF16)
        acc = acc + _dot(a, wd_ref[c0:c0 + tf, :].astype(BF16))
        c0 += tf
    return x + 0.5 * acc


def _ffn_body(x_ref, nw_ref, wgu_ref, wd_ref, o_ref):
    o_ref[...] = _swiglu_residual(x_ref[...], nw_ref[...], wgu_ref, wd_ref)


def _ffn_first_body(*refs):
    *piece_refs, xs_ref, nw_ref, wgu_ref, wd_ref, o_ref = refs
    last = pl.program_id(0) == pl.num_programs(0) - 1
    pieces = [r[...] for r in piece_refs]
    pieces[-1] = jnp.where(last, xs_ref[...], pieces[-1])
    o_ref[...] = _swiglu_residual(jnp.concatenate(pieces, axis=0), nw_ref[...], wgu_ref, wd_ref)


def _ffn_first(xp, xs, norm_w, wgu, wd, l):
    prows, dbatch = xp.shape[0], xs.shape[0]
    rows = prows + dbatch
    tm = _row_tile(rows)
    n_piece = tm // dbatch
    last_piece = prows // dbatch - 1
    piece_spec = lambda p: pl.BlockSpec(
        (dbatch, D_MODEL), lambda i: (jnp.minimum(i * n_piece + p, last_piece), 0))
    return pl.pallas_call(
        _ffn_first_body,
        grid=(rows // tm,),
        in_specs=[piece_spec(p) for p in range(n_piece)] + [
            _const_spec((dbatch, D_MODEL)),
            _layer_spec(l, (1, D_MODEL)),
            _layer_spec(l, (D_MODEL, 2 * D_FF)),
            _layer_spec(l, (D_FF, D_MODEL)),
        ],
        out_specs=pl.BlockSpec((tm, D_MODEL), lambda i: (i, 0)),
        out_shape=jax.ShapeDtypeStruct((rows, D_MODEL), F32),
        compiler_params=pltpu.CompilerParams(
            dimension_semantics=("parallel",), vmem_limit_bytes=VMEM_LIMIT),
        name="ffn_first",
    )(*([xp] * n_piece), xs, norm_w, wgu, wd)


def _can_split_rows(prows, dbatch):
    tm = _row_tile(prows + dbatch)
    return tm % dbatch == 0 and prows % dbatch == 0 and dbatch % SUBLANES == 0


def _ffn(x, norm_w, wgu, wd, l):
    rows = x.shape[0]
    tm = _row_tile(rows)
    return pl.pallas_call(
        _ffn_body,
        grid=(rows // tm,),
        in_specs=[
            pl.BlockSpec((tm, D_MODEL), lambda i: (i, 0)),
            _layer_spec(l, (1, D_MODEL)),
            _layer_spec(l, (D_MODEL, 2 * D_FF)),
            _layer_spec(l, (D_FF, D_MODEL)),
        ],
        out_specs=pl.BlockSpec((tm, D_MODEL), lambda i: (i, 0)),
        out_shape=jax.ShapeDtypeStruct((rows, D_MODEL), F32),
        compiler_params=pltpu.CompilerParams(
            dimension_semantics=("parallel",), vmem_limit_bytes=VMEM_LIMIT),
        name="ffn",
    )(x, norm_w, wgu, wd)


IN_SEGS = (QKV_W, CONV_DIM, DN_V, LANES, 2 * D_MODEL)
IN_DTYPES = (F32, F32, F32, F32, BF16)
IN_COLS = QKV_W + CONV_DIM + DN_V + 2 * DN_HEADS + 2 * D_MODEL
IN_ALIGNED = QKV_W + CONV_DIM + DN_V


def _inproj_body(h_ref, nw_ref, w_ref, wba_ref, wg_ref, qkv_ref, raw_ref, z_ref, ba_ref, gates_ref):
    u = _rms(h_ref[...], nw_ref[...]).astype(BF16)
    qkv = _dot(u, w_ref[:, :QKV_W].astype(BF16))
    lo = lax.broadcasted_iota(jnp.int32, (qkv.shape[0], LANES), 1) < HEAD_DIM
    nat = [qkv[:, c * LANES:(c + 1) * LANES] for c in range(ATTN_Q // LANES)]
    for g in range(GROUP):
        first, second = nat[g // 2], nat[(GROUP + g) // 2]
        if g % 2 == 0:
            second = pltpu.roll(second, HEAD_DIM, 1)
        else:
            first = pltpu.roll(first, HEAD_DIM, 1)
        qkv_ref[:, g * LANES:(g + 1) * LANES] = jnp.where(lo, first, second)
    qkv_ref[:, ATTN_Q:] = qkv[:, ATTN_Q:]
    raw_ref[...] = _dot(u, w_ref[:, QKV_W:QKV_W + CONV_DIM].astype(BF16))
    z_ref[...] = _dot(u, w_ref[:, QKV_W + CONV_DIM:IN_ALIGNED].astype(BF16))
    ba_ref[...] = _dot(u, wba_ref[...])
    gates_ref[...] = _dot(u, wg_ref[...].astype(BF16)).astype(gates_ref.dtype)


def _inproj(h, norm_w, w_in, wba, wg, l):
    rows = h.shape[0]
    tm = _row_tile(rows)
    return pl.pallas_call(
        _inproj_body,
        grid=(rows // tm,),
        in_specs=[
            pl.BlockSpec((tm, D_MODEL), lambda i: (i, 0)),
            _layer_spec(l, (1, D_MODEL)),
            _layer_spec(l, (D_MODEL, IN_COLS)),
            _layer_spec(l, (D_MODEL, LANES)),
            _layer_spec(l, (D_MODEL, 2 * D_MODEL)),
        ],
        out_specs=[pl.BlockSpec((tm, w), lambda i: (i, 0)) for w in IN_SEGS],
        out_shape=[jax.ShapeDtypeStruct((rows, w), dt) for w, dt in zip(IN_SEGS, IN_DTYPES)],
        compiler_params=pltpu.CompilerParams(
            dimension_semantics=("parallel",), vmem_limit_bytes=VMEM_LIMIT),
        name="inproj",
    )(h, norm_w, w_in, wba, wg)


def _merge_ffn_body(h_ref, oa_ref, od_ref, gates_ref, wao_ref, wdo_ref, wout_ref,
                    nw_ref, wgu_ref, wd_ref, o_ref, *sample_out):
    br_a = _dot(oa_ref[...], wao_ref[...])
    br_d = _dot(od_ref[...], wdo_ref[...])
    ga = gates_ref[:, :D_MODEL].astype(F32)
    gd = gates_ref[:, D_MODEL:].astype(F32)
    m = jax.nn.sigmoid(ga) * br_a + jax.nn.sigmoid(gd) * br_d
    h = h_ref[...] + _dot(m.astype(BF16), wout_ref[...])
    y = _swiglu_residual(h, nw_ref[...], wgu_ref, wd_ref)
    o_ref[...] = y
    if sample_out:
        (os_ref,) = sample_out

        @pl.when(pl.program_id(0) == pl.num_programs(0) - 1)
        def _():
            os_ref[...] = y[y.shape[0] - os_ref.shape[0]:, :]


def _merge_ffn(h, oa, od, gates, wao, wdo, wout, norm_w, wgu, wd, l, split=None):
    rows = h.shape[0]
    tm = _row_tile(rows)
    out_specs = pl.BlockSpec((tm, D_MODEL), lambda i: (i, 0))
    out_shape = jax.ShapeDtypeStruct((rows, D_MODEL), F32)
    if split is not None:
        prows, dbatch = split
        out_specs = [out_specs, pl.BlockSpec((dbatch, D_MODEL), lambda i: (0, 0))]
        out_shape = [jax.ShapeDtypeStruct((prows, D_MODEL), F32), jax.ShapeDtypeStruct((dbatch, D_MODEL), F32)]
    return pl.pallas_call(
        _merge_ffn_body,
        grid=(rows // tm,),
        in_specs=[
            pl.BlockSpec((tm, D_MODEL), lambda i: (i, 0)),
            pl.BlockSpec((tm, ATTN_Q), lambda i: (i, 0)),
            pl.BlockSpec((tm, DN_V), lambda i: (i, 0)),
            pl.BlockSpec((tm, 2 * D_MODEL), lambda i: (i, 0)),
            _layer_spec(l, (ATTN_Q, D_MODEL)),
            _layer_spec(l, (DN_V, D_MODEL)),
            _layer_spec(l, (D_MODEL, D_MODEL)),
            _layer_spec(l, (1, D_MODEL)),
            _layer_spec(l, (D_MODEL, 2 * D_FF)),
            _layer_spec(l, (D_FF, D_MODEL)),
        ],
        out_specs=out_specs,
        out_shape=out_shape,
        compiler_params=pltpu.CompilerParams(
            dimension_semantics=("arbitrary",), vmem_limit_bytes=VMEM_LIMIT),
        name="merge_ffn",
    )(h, oa, od, gates, wao, wdo, wout, norm_w, wgu, wd)


def _norm_rope(x, ones_bd, nw, cos, sin):
    lane = lax.broadcasted_iota(jnp.int32, (x.shape[0], LANES), 1)
    first_half = (lane & (HEAD_DIM // 2)) == 0
    outs = []
    for g in range(x.shape[1] // LANES):
        xg = x[:, g * LANES:(g + 1) * LANES]
        hi, lo = _split2(xg * xg)
        ssq = _dot(hi, ones_bd) + _dot(lo, ones_bd)
        xn = xg * lax.rsqrt(ssq * (1.0 / HEAD_DIM) + EPS) * nw[:, g * LANES:(g + 1) * LANES]
        partner = jnp.where(first_half,
                            pltpu.roll(xn, LANES - HEAD_DIM // 2, 1),
                            pltpu.roll(xn, HEAD_DIM // 2, 1))
        outs.append(xn * cos + partner * sin)
    return jnp.concatenate(outs, axis=1)


ATTN_STEP_BLOCKS = 2


def _attn_prompt_body(sinks_ref, x_ref, cos_ref, sin_ref, nw_ref, ones_ref, *rest):
    o_ref, kout_ref, vout_ref, kprev, vprev = rest[-5:]
    step = pl.program_id(1)

    @pl.when(step == 0)
    def _():
        kprev[...] = jnp.zeros_like(kprev)
        vprev[...] = jnp.zeros_like(vprev)

    x = x_ref[...]
    qk = _norm_rope(x[:, :QK_W], ones_ref[...], nw_ref[...], cos_ref[...], sin_ref[...])
    k_all = qk[:, ATTN_Q:QK_W]
    v_all = x[:, QK_W:QKV_W]
    k_bf = jnp.concatenate([kprev[...], k_all.astype(BF16)], axis=0)
    v_bf = jnp.concatenate([vprev[...], v_all.astype(BF16)], axis=0)

    r = lax.broadcasted_iota(jnp.int32, (ATTN_BLOCK, 2 * ATTN_BLOCK), 0)
    c = lax.broadcasted_iota(jnp.int32, (ATTN_BLOCK, 2 * ATTN_BLOCK), 1)
    band = (c >= r) & (c <= r + WINDOW)
    first_col = jnp.where(step == 0, ATTN_BLOCK, 0)
    lo = lax.broadcasted_iota(jnp.int32, (ATTN_BLOCK, LANES), 1) < HEAD_DIM
    scale = LOG2E / math.sqrt(HEAD_DIM)
    klane = lax.broadcasted_iota(jnp.int32, k_bf.shape, 1) < HEAD_DIM
    k_half = (jnp.where(klane, k_bf, jnp.zeros_like(k_bf)), jnp.where(klane, jnp.zeros_like(k_bf), k_bf))

    units = [(sb, g, half) for sb in range(ATTN_STEP_BLOCKS) for g in range(GROUP) for half in range(2)]
    q_bf = {(sb, g): (qk[sb * ATTN_BLOCK:(sb + 1) * ATTN_BLOCK, g * LANES:(g + 1) * LANES] * scale).astype(BF16)
            for sb in range(ATTN_STEP_BLOCKS) for g in range(GROUP)}
    s_raw = {}
    for (sb, g, half) in units:
        s_raw[sb, g, half] = _dot_nt(q_bf[sb, g], k_half[half][sb * ATTN_BLOCK:(sb + 2) * ATTN_BLOCK])
    p_den = {}
    for (sb, g, half) in units:
        vis = band & (c >= first_col) if sb == 0 else band
        s = jnp.where(vis, s_raw.pop((sb, g, half)), -jnp.inf)
        sink = sinks_ref[HEAD_PERM[2 * g + half]] * LOG2E
        m = jnp.maximum(jnp.max(s, axis=1, keepdims=True), sink)
        p = jnp.exp2(s - m)
        p_den[sb, g, half] = (p.astype(BF16), jnp.sum(p, axis=1, keepdims=True) + jnp.exp2(sink - m))
    for sb in range(ATTN_STEP_BLOCKS):
        outs = []
        for g in range(GROUP):
            res = []
            for half in range(2):
                p, denom = p_den.pop((sb, g, half))
                res.append(_dot(p, v_bf[sb * ATTN_BLOCK:(sb + 2) * ATTN_BLOCK]) / denom)
            outs.append(jnp.where(lo, res[0], res[1]))
        o_ref[sb * ATTN_BLOCK:(sb + 1) * ATTN_BLOCK, :] = jnp.concatenate(outs, axis=1).astype(o_ref.dtype)

    last = slice((ATTN_STEP_BLOCKS - 1) * ATTN_BLOCK, ATTN_STEP_BLOCKS * ATTN_BLOCK)
    kprev[...] = k_all[last].astype(BF16)
    vprev[...] = v_all[last].astype(BF16)

    @pl.when(step == pl.num_programs(1) - 1)
    def _():
        kout_ref[...] = k_all[last]
        vout_ref[...] = v_all[last]


def _attn_prompt(qkv, sinks, cos, sin, nw, ones_bd, kstack, vstack, l, batch, seq, rows):
    rows_step = ATTN_STEP_BLOCKS * ATTN_BLOCK
    nb = seq // rows_step
    kv_block = (None, WINDOW, ATTN_KV)
    kv_idx = lambda b, j: (b, 0, 0)
    k_spec, k_sds, k_in, k_in_spec = _stacked_out(l, kstack, (batch, WINDOW, ATTN_KV), kv_block, kv_idx)
    v_spec, v_sds, v_in, v_in_spec = _stacked_out(l, vstack, (batch, WINDOW, ATTN_KV), kv_block, kv_idx)
    n_in = 6
    return pl.pallas_call(
        _attn_prompt_body,
        grid=(batch, nb),
        in_specs=[
            pl.BlockSpec(memory_space=pltpu.SMEM),
            pl.BlockSpec((rows_step, QKV_W), lambda b, j: (b * nb + j, 0)),
            pl.BlockSpec((rows_step, LANES), lambda b, j: (j, 0)),
            pl.BlockSpec((rows_step, LANES), lambda b, j: (j, 0)),
            _const_spec((1, QK_W)),
            _const_spec((LANES, LANES)),
        ] + k_in_spec + v_in_spec,
        out_specs=[pl.BlockSpec((rows_step, ATTN_Q), lambda b, j: (b * nb + j, 0)), k_spec, v_spec],
        out_shape=[jax.ShapeDtypeStruct((rows, ATTN_Q), BF16), k_sds, v_sds],
        input_output_aliases={n_in: 1, n_in + 1: 2} if l else {},
        scratch_shapes=[pltpu.VMEM((ATTN_BLOCK, ATTN_KV), BF16),
                        pltpu.VMEM((ATTN_BLOCK, ATTN_KV), BF16)],
        compiler_params=pltpu.CompilerParams(
            dimension_semantics=("parallel", "arbitrary"), vmem_limit_bytes=VMEM_LIMIT),
        name="attn_prompt",
    )(sinks, qkv, cos, sin, nw, ones_bd, *k_in, *v_in)


SAMPLE_BB = 16


def _attn_sample_body(x_ref, kc_ref, vc_ref, cos_ref, sin_ref, nw_ref, ones_ref, sinks_ref, *rest):
    o_ref, kout_ref, vout_ref = rest[-3:]
    x = x_ref[...]
    qk = _norm_rope(x[:, :QK_W], ones_ref[...], nw_ref[...], cos_ref[...], sin_ref[...])
    knew = qk[:, ATTN_Q:QK_W]
    vnew = x[:, QK_W:QKV_W]
    scale = 1.0 / math.sqrt(HEAD_DIM)
    row = lax.broadcasted_iota(jnp.int32, (N_HEADS, LANES), 0)
    lane = lax.broadcasted_iota(jnp.int32, (N_HEADS, LANES), 1)
    own = (lane < HEAD_DIM) == ((row & 1) == 0)
    lo1 = lax.broadcasted_iota(jnp.int32, (1, LANES), 1) < HEAD_DIM
    sinks = sinks_ref[...]
    units = range(SAMPLE_BB)
    qz, s_all, soft = {}, {}, {}
    for b in units:
        qrows = [qk[b:b + 1, (i // 2) * LANES:(i // 2 + 1) * LANES] for i in range(N_HEADS)]
        qz[b] = jnp.where(own, jnp.concatenate(qrows, axis=0) * scale, 0.0)
        s_all[b] = _dot(qz[b].astype(BF16), kc_ref[b].reshape(ATTN_KV, WINDOW).astype(BF16))
    sink = sinks[:, 0:1]
    for b in units:
        s = s_all.pop(b)
        s_new = jnp.sum(qz.pop(b) * knew[b:b + 1, :], axis=1, keepdims=True)
        m = jnp.maximum(jnp.maximum(jnp.max(s, axis=1, keepdims=True), s_new), sink)
        p = jnp.exp(s - m)
        p_new = jnp.exp(s_new - m)
        soft[b] = (p.astype(BF16), p_new, jnp.sum(p, axis=1, keepdims=True) + p_new + jnp.exp(sink - m))
    o_rows = []
    for b in units:
        p, p_new, denom = soft.pop(b)
        pv = (_dot_nt(p, vc_ref[b].reshape(ATTN_KV, WINDOW).astype(BF16)) + p_new * vnew[b:b + 1, :]) / denom
        o_rows.append(jnp.concatenate(
            [jnp.where(lo1, pv[2 * g:2 * g + 1, :], pv[2 * g + 1:2 * g + 2, :]) for g in range(GROUP)],
            axis=1))
    pad = jnp.zeros((LANES - SAMPLE_BB, ATTN_KV), F32)
    knew_t = jnp.concatenate([knew, pad], axis=0).T
    vnew_t = jnp.concatenate([vnew, pad], axis=0).T
    newest = lax.broadcasted_iota(jnp.int32, (ATTN_KV, WINDOW), 1) == WINDOW - 1
    kv_block = (N_KV_HEADS, HEAD_DIM, WINDOW)
    for b in units:
        for src_ref, new_t, dst_ref in ((kc_ref, knew_t, kout_ref), (vc_ref, vnew_t, vout_ref)):
            shifted = pltpu.roll(src_ref[b].reshape(ATTN_KV, WINDOW), WINDOW - 1, 1)
            dst_ref[b] = jnp.where(newest, new_t[:, b:b + 1], shifted).reshape(kv_block)
    o_ref[...] = jnp.concatenate(o_rows, axis=0).astype(o_ref.dtype)


def _attn_sample(qkv, cache_k, cache_v, cos, sin, nw, ones_bd, sinks_b, oa, kstack, vstack,
                 l, row0, dbatch):
    bb = SAMPLE_BB
    r0 = row0 // bb
    kv_block = (bb, N_KV_HEADS, HEAD_DIM, WINDOW)
    kv_idx = lambda i: (i, 0, 0, 0)
    kv_shape = (dbatch, N_KV_HEADS, HEAD_DIM, WINDOW)
    k_spec, k_sds, k_in, k_in_spec = _stacked_out(l, kstack, kv_shape, kv_block, kv_idx)
    v_spec, v_sds, v_in, v_in_spec = _stacked_out(l, vstack, kv_shape, kv_block, kv_idx)
    n_in = 9
    return pl.pallas_call(
        _attn_sample_body,
        grid=(dbatch // bb,),
        in_specs=[
            pl.BlockSpec((bb, QKV_W), lambda i: (r0 + i, 0)),
            pl.BlockSpec((None,) + kv_block, lambda i: (l, i, 0, 0, 0)),
            pl.BlockSpec((None,) + kv_block, lambda i: (l, i, 0, 0, 0)),
            _const_spec((1, LANES)),
            _const_spec((1, LANES)),
            _const_spec((1, QK_W)),
            _const_spec((LANES, LANES)),
            _const_spec((N_HEADS, LANES)),
            pl.BlockSpec(memory_space=pl.ANY),
        ] + k_in_spec + v_in_spec,
        out_specs=[pl.BlockSpec((bb, ATTN_Q), lambda i: (r0 + i, 0)), k_spec, v_spec],
        out_shape=[jax.ShapeDtypeStruct(oa.shape, oa.dtype), k_sds, v_sds],
        input_output_aliases={8: 0, n_in: 1, n_in + 1: 2} if l else {8: 0},
        compiler_params=pltpu.CompilerParams(
            dimension_semantics=("parallel",), vmem_limit_bytes=VMEM_LIMIT),
        name="attn_sample",
    )(qkv, cache_k, cache_v, cos, sin, nw, ones_bd, sinks_b, oa, *k_in, *v_in)


def _softplus(x):
    return jnp.maximum(x, 0.0) + jnp.log(1.0 + jnp.exp(-jnp.abs(x)))


def _silu(x):
    return x * jax.nn.sigmoid(x)


def _l2n(x):
    return x * lax.rsqrt(jnp.sum(x * x, axis=-1, keepdims=True) + EPS)


def _l2n_mxu(x, ones):
    hi, lo = _split2(x * x)
    return x * lax.rsqrt(_dot(hi, ones) + _dot(lo, ones) + EPS)


def _gate_rows(ba, alog_row, dtb_row):
    beta = jax.nn.sigmoid(ba)
    g = -jnp.exp(alog_row) * _softplus(ba + dtb_row)
    return beta, g


GROUP_TILES = 4
GROUP_TOK = GROUP_TILES * TILE
CHUNKS = TILE // DN_CHUNK


def _gdn_prompt_body(raw_ref, z_ref, ba_ref, cw_ref, alog_ref, dtb_ref, onw_ref,
                     ltri_ref, lall_ref, lvl_ref, *rest, groups_per_seq):
    od_ref, sout_ref, cout_ref, xp, s_scr, ub_scr, wq_scr, kq_scr, egl_scr = rest[-9:]
    i = pl.program_id(0)
    n_groups = pl.num_programs(0) - 1
    ga = jnp.minimum(i, n_groups - 1)
    gb = jnp.maximum(i - 1, 0)
    slot_a = i % 2
    slot_b = 1 - slot_a

    @pl.when(i == 0)
    def _():
        ub_scr[1] = jnp.zeros(ub_scr.shape[1:], F32)
        wq_scr[1] = jnp.zeros(wq_scr.shape[1:], BF16)
        kq_scr[1] = jnp.zeros(kq_scr.shape[1:], BF16)
        egl_scr[1] = jnp.zeros(egl_scr.shape[1:], F32)

    @pl.when(ga % groups_per_seq == 0)
    def _():
        xp[0:SUBLANES, :] = jnp.zeros((SUBLANES, CONV_DIM), F32)

    @pl.when(gb % groups_per_seq == 0)
    def _():
        s_scr[...] = jnp.zeros_like(s_scr)

    @pl.when(ga % groups_per_seq == groups_per_seq - 1)
    def _():
        cout_ref[...] = raw_ref[GROUP_TOK - (CONV_W - 1):GROUP_TOK, :]

    ltri = ltri_ref[...]
    lall = lall_ref[...]
    ri = lax.broadcasted_iota(jnp.int32, (TILE, TILE), 0)
    ci = lax.broadcasted_iota(jnp.int32, (TILE, TILE), 1)
    same = (ri // DN_CHUNK) == (ci // DN_CHUNK)
    causal = same & (ci <= ri)
    strict = same & (ci < ri)
    eye_bf = (ri == ci).astype(F32).astype(BF16)
    zeros_half = jnp.zeros((DN_CHUNK, DN_DV), BF16)
    onw = onw_ref[...]
    ones_sq = jnp.ones((DN_DK, LANES), BF16)
    units = [(j, h) for j in range(GROUP_TILES) for h in range(DN_HEADS)]

    st = [s_scr[h] for h in range(DN_HEADS)]
    b_live = {}

    def b_stage1(j, c):
        for h in range(DN_HEADS):
            idx = j * DN_HEADS + h
            r1 = _dot(wq_scr[slot_b, idx, 2 * c * DN_CHUNK:(2 * c + 2) * DN_CHUNK, :], st[h].astype(BF16))
            u = ub_scr[slot_b, idx, c * DN_CHUNK:(c + 1) * DN_CHUNK, :] - r1[:DN_CHUNK]
            parts = [zeros_half] * CHUNKS
            parts[c] = u.astype(BF16)
            b_live[h] = (jnp.concatenate(parts, axis=0), r1[DN_CHUNK:])

    def b_stage2(j, c):
        egl_f = egl_scr[slot_b, j]
        for h in range(DN_HEADS):
            idx = j * DN_HEADS + h
            u_full, oq = b_live[h]
            base = c * (DN_DK + DN_CHUNK)
            r2 = _dot(kq_scr[slot_b, idx, base:base + DN_DK + DN_CHUNK, :], u_full)
            o = oq + r2[DN_DK:]
            zh = z_ref[j * TILE + c * DN_CHUNK:j * TILE + (c + 1) * DN_CHUNK, h * DN_DV:(h + 1) * DN_DV]
            od_ref[j * TILE + c * DN_CHUNK:j * TILE + (c + 1) * DN_CHUNK, h * DN_DV:(h + 1) * DN_DV] = (
                _rms(o, onw) * _silu(zh)).astype(od_ref.dtype)
            st[h] = (st[h] * egl_f[c * DN_CHUNK:c * DN_CHUNK + 1, DN_HEADS + h:DN_HEADS + h + 1]
                     + r2[:DN_DK])

    b_stages = []
    for j in range(GROUP_TILES):
        for c in range(CHUNKS):
            b_stages.append(functools.partial(b_stage1, j, c))
            b_stages.append(functools.partial(b_stage2, j, c))
    b_iter = iter(b_stages)

    def b_step():
        f = next(b_iter, None)
        if f is not None:
            f()

    xp[SUBLANES:SUBLANES + GROUP_TOK, :] = raw_ref[...]
    ys = []
    for j in range(GROUP_TILES):
        r0 = SUBLANES + j * TILE
        y = xp[r0:r0 + TILE, :] * cw_ref[CONV_W - 1:CONV_W, :]
        for s in range(1, CONV_W):
            y = y + xp[r0 - s:r0 - s + TILE, :] * cw_ref[CONV_W - 1 - s:CONV_W - s, :]
        ys.append(_silu(y))
        b_step()
    xp[0:SUBLANES, :] = raw_ref[GROUP_TOK - SUBLANES:GROUP_TOK, :]

    tiles = []
    for j in range(GROUP_TILES):
        beta_f, g_f = _gate_rows(ba_ref[j * TILE:(j + 1) * TILE, :], alog_ref[...], dtb_ref[...])
        g_parts = _split3(g_f)
        gcum = sum(_dot(ltri, part) for part in g_parts)
        glast = sum(_dot(lall, part) for part in g_parts)
        egl_scr[slot_a, j] = jnp.exp(glast)
        tiles.append((beta_f, gcum, glast, gcum.T))

    u = {}
    for (j, h) in units:
        beta_f, gcum, glast, gcum_t = tiles[j]
        y = ys[j]
        k = _l2n_mxu(y[:, DN_QK + h * DN_DK:DN_QK + (h + 1) * DN_DK], ones_sq)
        gc = gcum[:, DN_HEADS + h:DN_HEADS + h + 1]
        gr = gcum_t[DN_HEADS + h:DN_HEADS + h + 1, :]
        u[j, h] = dict(k=k, kb=k.astype(BF16), beta=beta_f[:, h:h + 1], gc=gc,
                       gl=glast[:, DN_HEADS + h:DN_HEADS + h + 1],
                       decay=jnp.exp(jnp.where(causal, gc - gr, -jnp.inf)))
    for un in units:
        d = u[un]
        d["kk"] = _dot_nt(d["kb"], d["kb"])
    b_step()
    for un in units:
        d = u[un]
        a = jnp.where(strict, d["beta"] * d.pop("kk") * d["decay"], 0.0)
        d["ab"] = a.astype(BF16)
        d["tb"] = eye_bf - d["ab"] * lvl_ref[0]
    for lv in range(1, 6):
        for un in units:
            d = u[un]
            d["p"] = _dot(d["tb"], d["ab"] * lvl_ref[lv]).astype(BF16)
        b_step()
        for un in units:
            d = u[un]
            d["tb"] = d["tb"] - _dot(d.pop("p"), d["tb"]).astype(BF16)
        b_step()
    for (j, h) in units:
        d = u[j, h]
        y = ys[j]
        v = y[:, 2 * DN_QK + h * DN_DV:2 * DN_QK + (h + 1) * DN_DV]
        d["eg"] = jnp.exp(d["gc"])
        rhs = jnp.concatenate([v * d["beta"], d["k"] * (d["beta"] * d["eg"])], axis=1).astype(BF16)
        d["sol"] = _dot(d.pop("tb"), rhs)
    b_step()
    for (j, h) in units:
        d = u[j, h]
        y = ys[j]
        q = _l2n_mxu(y[:, h * DN_DK:(h + 1) * DN_DK], ones_sq) * (DN_DK ** -0.5)
        d["qk"] = (_dot_nt(q.astype(BF16), d["kb"]) * d["decay"]).astype(BF16)
        d["qd"] = (q * d["eg"]).astype(BF16)
    b_step()
    for _ in range(len(b_stages)):
        b_step()
    for (j, h) in units:
        d = u[j, h]
        idx = j * DN_HEADS + h
        sol = d["sol"]
        ub_scr[slot_a, idx] = sol[:, :DN_DV]
        w = sol[:, DN_DV:].astype(BF16)
        kd_t = (d["k"] * jnp.exp(d["gl"] - d["gc"])).T.astype(BF16)
        for c in range(CHUNKS):
            rows = slice(c * DN_CHUNK, (c + 1) * DN_CHUNK)
            wq_scr[slot_a, idx, 2 * c * DN_CHUNK:(2 * c + 1) * DN_CHUNK, :] = w[rows]
            wq_scr[slot_a, idx, (2 * c + 1) * DN_CHUNK:(2 * c + 2) * DN_CHUNK, :] = d["qd"][rows]
            base = c * (DN_DK + DN_CHUNK)
            kq_scr[slot_a, idx, base:base + DN_DK, :] = kd_t
            kq_scr[slot_a, idx, base + DN_DK:base + DN_DK + DN_CHUNK, :] = d["qk"][rows]

    for h in range(DN_HEADS):
        s_scr[h] = st[h]

    @pl.when(gb % groups_per_seq == groups_per_seq - 1)
    def _():
        for h in range(DN_HEADS):
            sout_ref[h] = st[h]


def _gdn_prompt(raw, z, ba, cw, alog_row, dtb_row, onw, consts, sstack, cstack, l, batch, seq, rows):
    gps = seq // GROUP_TOK
    ng = batch * gps
    s_spec, s_sds, s_in, s_in_spec = _stacked_out(
        l, sstack, (batch, DN_HEADS, DN_DK, DN_DV), (None, DN_HEADS, DN_DK, DN_DV),
        lambda i: (jnp.maximum(i - 1, 0) // gps, 0, 0, 0))
    c_spec, c_sds, c_in, c_in_spec = _stacked_out(
        l, cstack, (batch, CONV_W - 1, CONV_DIM), (None, CONV_W - 1, CONV_DIM),
        lambda i: (jnp.minimum(i, ng - 1) // gps, 0, 0))
    n_in = 10
    ltri, lall, lvl = consts
    nht = GROUP_TILES * DN_HEADS
    a_idx = lambda i: (jnp.minimum(i, ng - 1), 0)
    b_idx = lambda i: (jnp.maximum(i - 1, 0), 0)
    return pl.pallas_call(
        functools.partial(_gdn_prompt_body, groups_per_seq=gps),
        grid=(ng + 1,),
        in_specs=[
            pl.BlockSpec((GROUP_TOK, CONV_DIM), a_idx),
            pl.BlockSpec((GROUP_TOK, DN_V), b_idx),
            pl.BlockSpec((GROUP_TOK, LANES), a_idx),
            _layer_spec(l, (CONV_W, CONV_DIM)),
            _layer_spec(l, (1, LANES)),
            _layer_spec(l, (1, LANES)),
            _layer_spec(l, (1, DN_DV)),
            _const_spec((TILE, TILE)),
            _const_spec((TILE, TILE)),
            _const_spec((6, TILE, TILE)),
        ] + s_in_spec + c_in_spec,
        out_specs=[pl.BlockSpec((GROUP_TOK, DN_V), b_idx), s_spec, c_spec],
        out_shape=[jax.ShapeDtypeStruct((rows, DN_V), BF16), s_sds, c_sds],
        input_output_aliases={n_in: 1, n_in + 1: 2} if l else {},
        scratch_shapes=[
            pltpu.VMEM((SUBLANES + GROUP_TOK, CONV_DIM), F32),
            pltpu.VMEM((DN_HEADS, DN_DK, DN_DV), F32),
            pltpu.VMEM((2, nht, TILE, DN_DV), F32),
            pltpu.VMEM((2, nht, 2 * TILE, DN_DK), BF16),
            pltpu.VMEM((2, nht, CHUNKS * (DN_DK + DN_CHUNK), TILE), BF16),
            pltpu.VMEM((2, GROUP_TILES, TILE, LANES), F32),
        ],
        compiler_params=pltpu.CompilerParams(
            dimension_semantics=("arbitrary",), vmem_limit_bytes=VMEM_LIMIT),
        name="gdn_prompt",
    )(raw, z, ba, cw, alog_row, dtb_row, onw, ltri, lall, lvl, *s_in, *c_in)


def _gdn_sample_body(raw_ref, z_ref, ba_ref, cs_ref, st_ref, cw_ref, alog_ref, dtb_ref, onw_ref, *rest):
    od_ref, sout_ref, cout_ref = rest[-3:]
    bb = SAMPLE_BB
    raw = raw_ref[...]
    y = raw * cw_ref[CONV_W - 1:CONV_W, :]
    for i in range(CONV_W - 1):
        y = y + cs_ref[i] * cw_ref[i:i + 1, :]
    y = _silu(y)
    for i in range(CONV_W - 2):
        cout_ref[i] = cs_ref[i + 1]
    cout_ref[CONV_W - 2] = raw

    beta_f, g_f = _gate_rows(ba_ref[...], alog_ref[...], dtb_ref[...])
    eg_f = jnp.exp(g_f)
    pad = jnp.zeros((LANES - bb, DN_DK), F32)
    outs = []
    for h in range(DN_HEADS):
        q = _l2n(y[:, h * DN_DK:(h + 1) * DN_DK]) * (DN_DK ** -0.5)
        k = _l2n(y[:, DN_QK + h * DN_DK:DN_QK + (h + 1) * DN_DK])
        v = y[:, 2 * DN_QK + h * DN_DV:2 * DN_QK + (h + 1) * DN_DV]
        k_t = jnp.concatenate([k, pad], axis=0).T
        qk = jnp.sum(q * k, axis=1, keepdims=True)
        o_rows = []
        for b in range(bb):
            s1 = st_ref[b, h] * eg_f[b:b + 1, DN_HEADS + h:DN_HEADS + h + 1]
            kq = jnp.concatenate([k[b:b + 1, :], q[b:b + 1, :]], axis=0).astype(BF16)
            r = _dot(kq, s1.astype(BF16))
            delta = beta_f[b:b + 1, h:h + 1] * (v[b:b + 1, :] - r[0:1, :])
            sout_ref[b, h] = s1 + k_t[:, b:b + 1] * delta
            o_rows.append(r[1:2, :] + qk[b:b + 1, :] * delta)
        o = jnp.concatenate(o_rows, axis=0)
        zh = z_ref[:, h * DN_DV:(h + 1) * DN_DV]
        outs.append(_rms(o, onw_ref[...]) * _silu(zh))
    od_ref[...] = jnp.concatenate(outs, axis=1).astype(od_ref.dtype)


def _gdn_sample(raw, z, ba, conv_state, dn_state, cw, alog_row, dtb_row, onw, od, sstack, cstack,
                l, row0, dbatch):
    bb = SAMPLE_BB
    r0 = row0 // bb
    s_spec, s_sds, s_in, s_in_spec = _stacked_out(
        l, sstack, (dbatch, DN_HEADS, DN_DK, DN_DV), (bb, DN_HEADS, DN_DK, DN_DV), lambda i: (i, 0, 0, 0))
    c_spec, c_sds, c_in, c_in_spec = _stacked_out(
        l, cstack, (CONV_W - 1, dbatch, CONV_DIM), (CONV_W - 1, bb, CONV_DIM), lambda i: (0, i, 0))
    n_in = 10
    return pl.pallas_call(
        _gdn_sample_body,
        grid=(dbatch // bb,),
        in_specs=[
            pl.BlockSpec((bb, CONV_DIM), lambda i: (r0 + i, 0)),
            pl.BlockSpec((bb, DN_V), lambda i: (r0 + i, 0)),
            pl.BlockSpec((bb, LANES), lambda i: (r0 + i, 0)),
            pl.BlockSpec((None, CONV_W - 1, bb, CONV_DIM), lambda i: (l, 0, i, 0)),
            pl.BlockSpec((None, bb, DN_HEADS, DN_DK, DN_DV), lambda i: (l, i, 0, 0, 0)),
            _layer_spec(l, (CONV_W, CONV_DIM)),
            _layer_spec(l, (1, LANES)),
            _layer_spec(l, (1, LANES)),
            _layer_spec(l, (1, DN_DV)),
            pl.BlockSpec(memory_space=pl.ANY),
        ] + s_in_spec + c_in_spec,
        out_specs=[pl.BlockSpec((bb, DN_V), lambda i: (r0 + i, 0)), s_spec, c_spec],
        out_shape=[jax.ShapeDtypeStruct(od.shape, od.dtype), s_sds, c_sds],
        input_output_aliases={9: 0, n_in: 1, n_in + 1: 2} if l else {9: 0},
        compiler_params=pltpu.CompilerParams(
            dimension_semantics=("parallel",), vmem_limit_bytes=VMEM_LIMIT),
        name="gdn_sample",
    )(raw, z, ba, conv_state, dn_state, cw, alog_row, dtb_row, onw, od, *s_in, *c_in)


def _np_consts():
    i = np.arange(TILE)[:, None]
    j = np.arange(TILE)[None, :]
    same = (i // DN_CHUNK) == (j // DN_CHUNK)
    ltri = (same & (j <= i)).astype(np.float32)
    lall = same.astype(np.float32)
    lvls = []
    b = 1
    while b < DN_CHUNK:
        lvls.append(((i // (2 * b)) == (j // (2 * b))) & ((i // b) != (j // b)))
        b *= 2
    lvl = np.stack(lvls).astype(np.float32)
    hi = np.arange(LANES)
    ones_bd = (hi[:, None] // HEAD_DIM == hi[None, :] // HEAD_DIM).astype(np.float32)
    return ltri, lall, lvl, ones_bd


def _rope_tables(pos):
    half = HEAD_DIM // 2
    inv = 1.0 / (ROPE_THETA ** (jnp.arange(half, dtype=F32) / half))
    ang = pos.astype(F32)[:, None] * inv[None, :]
    cos, sin = jnp.cos(ang), jnp.sin(ang)
    cos_t = jnp.concatenate([cos, cos] * (LANES // HEAD_DIM), axis=1)
    sin_t = jnp.concatenate([-sin, sin] * (LANES // HEAD_DIM), axis=1)
    return cos_t, sin_t


def _pad_row(x, offset):
    out = jnp.zeros((x.shape[0], 1, LANES), F32)
    return out.at[:, 0, offset:offset + x.shape[1]].set(x.astype(F32))


def kernel(x_prompt, x_sample, cache_swa_k, cache_swa_v, state_dn, state_conv, ffn1_norm, ffn1_w_gate_up, ffn1_w_down, mix_norm, w_in, q_norm, k_norm, attn_sinks, conv_w, dn_A_log, dn_dt_bias, dn_out_norm, w_attn_o, w_dn_o, w_out, ffn2_norm, ffn2_w_gate_up, ffn2_w_down):
    batch, seq, _ = x_prompt.shape
    dbatch = x_sample.shape[0]
    prows = batch * seq
    rows = prows + dbatch
    perm = np.asarray(HEAD_PERM)

    wpad = jnp.zeros((DEPTH, D_MODEL, LANES - 2 * DN_HEADS), BF16)
    w_ba = jnp.concatenate([w_in[:, :, IN_ALIGNED:IN_ALIGNED + 2 * DN_HEADS].astype(BF16), wpad], axis=2)
    w_gates = w_in[:, :, IN_ALIGNED + 2 * DN_HEADS:]
    wao = w_attn_o.reshape(DEPTH, N_HEADS, HEAD_DIM, D_MODEL)[:, perm].reshape(DEPTH, ATTN_Q, D_MODEL).astype(BF16)
    wdo = w_dn_o.astype(BF16)
    wout = w_out.astype(BF16)
    wgu1, wd1, wgu2, wd2 = ffn1_w_gate_up, ffn1_w_down, ffn2_w_gate_up, ffn2_w_down
    n1 = ffn1_norm.reshape(DEPTH, 1, D_MODEL)
    n2 = ffn2_norm.reshape(DEPTH, 1, D_MODEL)
    nm = mix_norm.reshape(DEPTH, 1, D_MODEL)
    qk_nw = jnp.concatenate([jnp.tile(q_norm, (1, N_HEADS)), jnp.tile(k_norm, (1, N_KV_HEADS))], axis=1)
    alog_row = _pad_row(dn_A_log, DN_HEADS)
    dtb_row = _pad_row(dn_dt_bias, DN_HEADS)
    onw = dn_out_norm.reshape(DEPTH, 1, DN_DV)
    sinks_perm = attn_sinks[:, perm]
    sinks_b = jnp.broadcast_to(sinks_perm[:, :, None], (DEPTH, N_HEADS, LANES))

    ltri, lall, lvl, ones_bd = _np_consts()
    gdn_consts = (jnp.asarray(ltri, BF16), jnp.asarray(lall, BF16), jnp.asarray(lvl, BF16))
    ones_bd = jnp.asarray(ones_bd, BF16)
    cos_p, sin_p = _rope_tables(jnp.arange(seq))
    cos_s, sin_s = _rope_tables(PAST_LEN + jnp.arange(1))

    ck = jnp.transpose(cache_swa_k, (0, 1, 3, 4, 2))
    cv = jnp.transpose(cache_swa_v, (0, 1, 3, 4, 2))
    cs = jnp.transpose(state_conv, (0, 2, 1, 3))

    split_rows = _can_split_rows(prows, dbatch)
    xp2, xs2 = x_prompt.reshape(prows, D_MODEL), x_sample.reshape(dbatch, D_MODEL)
    kp = vp = sp = cp = ksn = vsn = ssn = csn = None
    for l in range(DEPTH):
        if l > 0:
            h = _ffn(h, n1, wgu1, wd1, l)
        elif split_rows:
            h = _ffn_first(xp2, xs2, n1, wgu1, wd1, l)
        else:
            h = _ffn(jnp.concatenate([xp2, xs2], axis=0), n1, wgu1, wd1, l)
        qkv, raw, z, ba, gates = _inproj(h, nm, w_in, w_ba, w_gates, l)
        oa, kp, vp = _attn_prompt(qkv, attn_sinks[l], cos_p, sin_p, qk_nw[l:l + 1], ones_bd,
                                  kp, vp, l, batch, seq, rows)
        oa, ksn, vsn = _attn_sample(qkv, ck, cv, cos_s, sin_s, qk_nw[l:l + 1], ones_bd,
                                    sinks_b[l], oa, ksn, vsn, l, prows, dbatch)
        od, sp, cp = _gdn_prompt(raw, z, ba, conv_w, alog_row, dtb_row, onw, gdn_consts,
                                 sp, cp, l, batch, seq, rows)
        od, ssn, csn = _gdn_sample(raw, z, ba, cs, state_dn, conv_w, alog_row, dtb_row, onw,
                                   od, ssn, csn, l, prows, dbatch)
        final_split = (prows, dbatch) if (split_rows and l == DEPTH - 1) else None
        h = _merge_ffn(h, oa, od, gates, wao, wdo, wout, n2, wgu2, wd2, l, split=final_split)

    yp, ys = h if split_rows else (h[:prows], h[prows:])
    kv_shape = (DEPTH, -1, WINDOW, N_KV_HEADS, HEAD_DIM)
    return (yp.reshape(batch, seq, D_MODEL),
            ys.reshape(dbatch, 1, D_MODEL),
            kp.reshape(kv_shape), vp.reshape(kv_shape), sp, cp,
            jnp.transpose(ksn, (0, 1, 4, 2, 3)), jnp.transpose(vsn, (0, 1, 4, 2, 3)), ssn,
            jnp.transpose(csn, (0, 2, 1, 3)))
```

```python
import functools
import math

import numpy as np
import jax
import jax.numpy as jnp
from jax import lax
from jax.experimental import pallas as pl
from jax.experimental.pallas import tpu as pltpu

F32 = jnp.float32
BF16 = jnp.bfloat16

D_MODEL = 1024
DEPTH = 4
PAST_LEN = 8192
N_HEADS = 8
N_KV_HEADS = 2
GROUP = N_HEADS // N_KV_HEADS
HEAD_DIM = 64
WINDOW = 128
ATTN_BLOCK = 128
ROPE_THETA = 10000.0
DN_HEADS = 4
DN_DK = 128
DN_DV = 128
CONV_W = 4
DN_CHUNK = 64
D_FF = 2816
EPS = 1e-6
LOG2E = math.log2(math.e)

ATTN_Q = N_HEADS * HEAD_DIM
ATTN_KV = N_KV_HEADS * HEAD_DIM
DN_QK = DN_HEADS * DN_DK
DN_V = DN_HEADS * DN_DV
CONV_DIM = 2 * DN_QK + DN_V
QKV_W = ATTN_Q + 2 * ATTN_KV
QK_W = ATTN_Q + ATTN_KV
LANES = 128
SUBLANES = 8
TILE = 128
VMEM_LIMIT = 60 * 1024 * 1024

HEAD_PERM = (0, 4, 1, 5, 2, 6, 3, 7)
MXU_COLS = 256
FF_CHUNKS = (6 * MXU_COLS, 5 * MXU_COLS)


def _row_tile(rows, cap=512):
    best = SUBLANES
    for t in range(SUBLANES, cap + 1, SUBLANES):
        if rows % t == 0:
            best = t
    return best


def _rms(x, w):
    return x * lax.rsqrt(jnp.mean(x * x, axis=-1, keepdims=True) + EPS) * w


def _dot(a, b):
    return jnp.dot(a, b, preferred_element_type=F32)


def _dot_nt(a, b):
    return lax.dot_general(a, b, (((1,), (1,)), ((), ())), preferred_element_type=F32)


def _split2(x):
    hi = x.astype(BF16)
    return hi, (x - hi.astype(F32)).astype(BF16)


def _split3(x):
    hi = x.astype(BF16)
    r = x - hi.astype(F32)
    mid = r.astype(BF16)
    lo = (r - mid.astype(F32)).astype(BF16)
    return hi, mid, lo


def _const_spec(shape):
    nd = len(shape)
    return pl.BlockSpec(shape, lambda *_: (0,) * nd, pipeline_mode=pl.Buffered(1))


def _layer_spec(l, shape):
    nd = len(shape)
    return pl.BlockSpec((None,) + shape, lambda *_: (l,) + (0,) * nd, pipeline_mode=pl.Buffered(1))


def _stacked_out(l, prev, shape, block, index_map):
    spec = pl.BlockSpec((None,) + block, lambda *idx: (l,) + tuple(index_map(*idx)))
    sds = jax.ShapeDtypeStruct((DEPTH,) + shape, F32)
    if l == 0:
        return spec, sds, [], []
    return spec, sds, [prev], [pl.BlockSpec(memory_space=pl.ANY)]


def _swiglu_residual(x, nw, wgu_ref, wd_ref):
    xn = _rms(x, nw).astype(BF16)
    acc = jnp.zeros_like(x)
    c0 = 0
    for tf in FF_CHUNKS:
        g = _dot(xn, wgu_ref[:, c0:c0 + tf].astype(BF16))
        u = _dot(xn, wgu_ref[:, D_FF + c0:D_FF + c0 + tf].astype(BF16))
        a = (g * jax.nn.sigmoid(g) * u).astype(BF16)
        acc = acc + _dot(a, wd_ref[c0:c0 + tf, :].astype(BF16))
        c0 += tf
    return x + 0.5 * acc


def _ffn_body(x_ref, nw_ref, wgu_ref, wd_ref, o_ref):
    o_ref[...] = _swiglu_residual(x_ref[...], nw_ref[...], wgu_ref, wd_ref)


def _ffn_first_body(*refs):
    *piece_refs, xs_ref, nw_ref, wgu_ref, wd_ref, o_ref = refs
    last = pl.program_id(0) == pl.num_programs(0) - 1
    pieces = [r[...] for r in piece_refs]
    pieces[-1] = jnp.where(last, xs_ref[...], pieces[-1])
    o_ref[...] = _swiglu_residual(jnp.concatenate(pieces, axis=0), nw_ref[...], wgu_ref, wd_ref)


def _ffn_first(xp, xs, norm_w, wgu, wd, l):
    prows, dbatch = xp.shape[0], xs.shape[0]
    rows = prows + dbatch
    tm = _row_tile(rows)
    n_piece = tm // dbatch
    last_piece = prows // dbatch - 1
    piece_spec = lambda p: pl.BlockSpec(
        (dbatch, D_MODEL), lambda i: (jnp.minimum(i * n_piece + p, last_piece), 0))
    return pl.pallas_call(
        _ffn_first_body,
        grid=(rows // tm,),
        in_specs=[piece_spec(p) for p in range(n_piece)] + [
            _const_spec((dbatch, D_MODEL)),
            _layer_spec(l, (1, D_MODEL)),
            _layer_spec(l, (D_MODEL, 2 * D_FF)),
            _layer_spec(l, (D_FF, D_MODEL)),
        ],
        out_specs=pl.BlockSpec((tm, D_MODEL), lambda i: (i, 0)),
        out_shape=jax.ShapeDtypeStruct((rows, D_MODEL), F32),
        compiler_params=pltpu.CompilerParams(
            dimension_semantics=("parallel",), vmem_limit_bytes=VMEM_LIMIT),
        name="ffn_first",
    )(*([xp] * n_piece), xs, norm_w, wgu, wd)


def _can_split_rows(prows, dbatch):
    tm = _row_tile(prows + dbatch)
    return tm % dbatch == 0 and prows % dbatch == 0 and dbatch % SUBLANES == 0


def _ffn(x, norm_w, wgu, wd, l):
    rows = x.shape[0]
    tm = _row_tile(rows)
    return pl.pallas_call(
        _ffn_body,
        grid=(rows // tm,),
        in_specs=[
            pl.BlockSpec((tm, D_MODEL), lambda i: (i, 0)),
            _layer_spec(l, (1, D_MODEL)),
            _layer_spec(l, (D_MODEL, 2 * D_FF)),
            _layer_spec(l, (D_FF, D_MODEL)),
        ],
        out_specs=pl.BlockSpec((tm, D_MODEL), lambda i: (i, 0)),
        out_shape=jax.ShapeDtypeStruct((rows, D_MODEL), F32),
        compiler_params=pltpu.CompilerParams(
            dimension_semantics=("parallel",), vmem_limit_bytes=VMEM_LIMIT),
        name="ffn",
    )(x, norm_w, wgu, wd)


IN_SEGS = (QKV_W, CONV_DIM, DN_V, LANES, 2 * D_MODEL)
IN_DTYPES = (F32, F32, F32, F32, BF16)
IN_COLS = QKV_W + CONV_DIM + DN_V + 2 * DN_HEADS + 2 * D_MODEL
IN_ALIGNED = QKV_W + CONV_DIM + DN_V


def _inproj_body(h_ref, nw_ref, wt_ref, qkv_ref, raw_ref, z_ref, ba_ref, gates_ref):
    u = _rms(h_ref[...], nw_ref[...]).astype(BF16)

    def proj(r0, r1):
        return _dot_nt(u, wt_ref[r0:r1, :].astype(BF16))

    qkv = proj(0, QKV_W)
    lo = lax.broadcasted_iota(jnp.int32, (qkv.shape[0], LANES), 1) < HEAD_DIM
    nat = [qkv[:, c * LANES:(c + 1) * LANES] for c in range(ATTN_Q // LANES)]
    for g in range(GROUP):
        first, second = nat[g // 2], nat[(GROUP + g) // 2]
        if g % 2 == 0:
            second = pltpu.roll(second, HEAD_DIM, 1)
        else:
            first = pltpu.roll(first, HEAD_DIM, 1)
        qkv_ref[:, g * LANES:(g + 1) * LANES] = jnp.where(lo, first, second)
    qkv_ref[:, ATTN_Q:] = qkv[:, ATTN_Q:]
    raw_ref[...] = proj(QKV_W, QKV_W + CONV_DIM)
    z_ref[...] = proj(QKV_W + CONV_DIM, IN_ALIGNED)
    ba_ref[...] = proj(IN_ALIGNED, IN_ALIGNED + LANES)
    gates_ref[...] = proj(IN_ALIGNED + 2 * DN_HEADS, IN_COLS).astype(gates_ref.dtype)


def _inproj(h, norm_w, w_in_t, l):
    rows = h.shape[0]
    tm = _row_tile(rows)
    return pl.pallas_call(
        _inproj_body,
        grid=(rows // tm,),
        in_specs=[
            pl.BlockSpec((tm, D_MODEL), lambda i: (i, 0)),
            _layer_spec(l, (1, D_MODEL)),
            _layer_spec(l, (IN_COLS, D_MODEL)),
        ],
        out_specs=[pl.BlockSpec((tm, w), lambda i: (i, 0)) for w in IN_SEGS],
        out_shape=[jax.ShapeDtypeStruct((rows, w), dt) for w, dt in zip(IN_SEGS, IN_DTYPES)],
        compiler_params=pltpu.CompilerParams(
            dimension_semantics=("parallel",), vmem_limit_bytes=VMEM_LIMIT),
        name="inproj",
    )(h, norm_w, w_in_t)


def _merge_ffn_body(h_ref, oa_ref, od_ref, gates_ref, wao_ref, wdo_ref, wout_ref,
                    nw_ref, wgu_ref, wd_ref, o_ref, *sample_out):
    br_a = _dot(oa_ref[...], wao_ref[...])
    br_d = _dot(od_ref[...], wdo_ref[...])
    ga = gates_ref[:, :D_MODEL].astype(F32)
    gd = gates_ref[:, D_MODEL:].astype(F32)
    m = jax.nn.sigmoid(ga) * br_a + jax.nn.sigmoid(gd) * br_d
    h = h_ref[...] + _dot(m.astype(BF16), wout_ref[...])
    y = _swiglu_residual(h, nw_ref[...], wgu_ref, wd_ref)
    o_ref[...] = y
    if sample_out:
        (os_ref,) = sample_out

        @pl.when(pl.program_id(0) == pl.num_programs(0) - 1)
        def _():
            os_ref[...] = y[y.shape[0] - os_ref.shape[0]:, :]


def _merge_ffn(h, oa, od, gates, wao, wdo, wout, norm_w, wgu, wd, l, split=None):
    rows = h.shape[0]
    tm = _row_tile(rows)
    out_specs = pl.BlockSpec((tm, D_MODEL), lambda i: (i, 0))
    out_shape = jax.ShapeDtypeStruct((rows, D_MODEL), F32)
    if split is not None:
        prows, dbatch = split
        out_specs = [out_specs, pl.BlockSpec((dbatch, D_MODEL), lambda i: (0, 0))]
        out_shape = [jax.ShapeDtypeStruct((prows, D_MODEL), F32), jax.ShapeDtypeStruct((dbatch, D_MODEL), F32)]
    return pl.pallas_call(
        _merge_ffn_body,
        grid=(rows // tm,),
        in_specs=[
            pl.BlockSpec((tm, D_MODEL), lambda i: (i, 0)),
            pl.BlockSpec((tm, ATTN_Q), lambda i: (i, 0)),
            pl.BlockSpec((tm, DN_V), lambda i: (i, 0)),
            pl.BlockSpec((tm, 2 * D_MODEL), lambda i: (i, 0)),
            _layer_spec(l, (ATTN_Q, D_MODEL)),
            _layer_spec(l, (DN_V, D_MODEL)),
            _layer_spec(l, (D_MODEL, D_MODEL)),
            _layer_spec(l, (1, D_MODEL)),
            _layer_spec(l, (D_MODEL, 2 * D_FF)),
            _layer_spec(l, (D_FF, D_MODEL)),
        ],
        out_specs=out_specs,
        out_shape=out_shape,
        compiler_params=pltpu.CompilerParams(
            dimension_semantics=("arbitrary",), vmem_limit_bytes=VMEM_LIMIT),
        name="merge_ffn",
    )(h, oa, od, gates, wao, wdo, wout, norm_w, wgu, wd)


def _norm_rope(x, ones_bd, nw, cos, sin):
    lane = lax.broadcasted_iota(jnp.int32, (x.shape[0], LANES), 1)
    first_half = (lane & (HEAD_DIM // 2)) == 0
    outs = []
    for g in range(x.shape[1] // LANES):
        xg = x[:, g * LANES:(g + 1) * LANES]
        hi, lo = _split2(xg * xg)
        ssq = _dot(hi, ones_bd) + _dot(lo, ones_bd)
        xn = xg * lax.rsqrt(ssq * (1.0 / HEAD_DIM) + EPS) * nw[:, g * LANES:(g + 1) * LANES]
        partner = jnp.where(first_half,
                            pltpu.roll(xn, LANES - HEAD_DIM // 2, 1),
                            pltpu.roll(xn, HEAD_DIM // 2, 1))
        outs.append(xn * cos + partner * sin)
    return jnp.concatenate(outs, axis=1)


ATTN_STEP_BLOCKS = 2


def _attn_prompt_body(sinks_ref, x_ref, cos_ref, sin_ref, nw_ref, ones_ref, *rest):
    o_ref, kout_ref, vout_ref, kprev, vprev = rest[-5:]
    step = pl.program_id(1)

    @pl.when(step == 0)
    def _():
        kprev[...] = jnp.zeros_like(kprev)
        vprev[...] = jnp.zeros_like(vprev)

    x = x_ref[...]
    qk = _norm_rope(x[:, :QK_W], ones_ref[...], nw_ref[...], cos_ref[...], sin_ref[...])
    k_all = qk[:, ATTN_Q:QK_W]
    v_all = x[:, QK_W:QKV_W]
    k_bf = jnp.concatenate([kprev[...], k_all.astype(BF16)], axis=0)
    v_bf = jnp.concatenate([vprev[...], v_all.astype(BF16)], axis=0)

    r = lax.broadcasted_iota(jnp.int32, (ATTN_BLOCK, 2 * ATTN_BLOCK), 0)
    c = lax.broadcasted_iota(jnp.int32, (ATTN_BLOCK, 2 * ATTN_BLOCK), 1)
    band = (c >= r) & (c <= r + WINDOW)
    first_col = jnp.where(step == 0, ATTN_BLOCK, 0)
    lo = lax.broadcasted_iota(jnp.int32, (ATTN_BLOCK, LANES), 1) < HEAD_DIM
    scale = LOG2E / math.sqrt(HEAD_DIM)
    klane = lax.broadcasted_iota(jnp.int32, k_bf.shape, 1) < HEAD_DIM
    k_half = (jnp.where(klane, k_bf, jnp.zeros_like(k_bf)), jnp.where(klane, jnp.zeros_like(k_bf), k_bf))

    units = [(sb, g, half) for sb in range(ATTN_STEP_BLOCKS) for g in range(GROUP) for half in range(2)]
    q_bf = {(sb, g): (qk[sb * ATTN_BLOCK:(sb + 1) * ATTN_BLOCK, g * LANES:(g + 1) * LANES] * scale).astype(BF16)
            for sb in range(ATTN_STEP_BLOCKS) for g in range(GROUP)}
    s_raw = {}
    for (sb, g, half) in units:
        s_raw[sb, g, half] = _dot_nt(q_bf[sb, g], k_half[half][sb * ATTN_BLOCK:(sb + 2) * ATTN_BLOCK])
    p_den = {}
    for (sb, g, half) in units:
        vis = band & (c >= first_col) if sb == 0 else band
        s = jnp.where(vis, s_raw.pop((sb, g, half)), -jnp.inf)
        sink = sinks_ref[HEAD_PERM[2 * g + half]] * LOG2E
        m = jnp.maximum(jnp.max(s, axis=1, keepdims=True), sink)
        p = jnp.exp2(s - m)
        p_den[sb, g, half] = (p.astype(BF16), jnp.sum(p, axis=1, keepdims=True) + jnp.exp2(sink - m))
    for sb in range(ATTN_STEP_BLOCKS):
        outs = []
        for g in range(GROUP):
            res = []
            for half in range(2):
                p, denom = p_den.pop((sb, g, half))
                res.append(_dot(p, v_bf[sb * ATTN_BLOCK:(sb + 2) * ATTN_BLOCK]) / denom)
            outs.append(jnp.where(lo, res[0], res[1]))
        o_ref[sb * ATTN_BLOCK:(sb + 1) * ATTN_BLOCK, :] = jnp.concatenate(outs, axis=1).astype(o_ref.dtype)

    last = slice((ATTN_STEP_BLOCKS - 1) * ATTN_BLOCK, ATTN_STEP_BLOCKS * ATTN_BLOCK)
    kprev[...] = k_all[last].astype(BF16)
    vprev[...] = v_all[last].astype(BF16)

    @pl.when(step == pl.num_programs(1) - 1)
    def _():
        kout_ref[...] = k_all[last]
        vout_ref[...] = v_all[last]


def _attn_prompt(qkv, sinks, cos, sin, nw, ones_bd, kstack, vstack, l, batch, seq, rows):
    rows_step = ATTN_STEP_BLOCKS * ATTN_BLOCK
    nb = seq // rows_step
    kv_block = (None, WINDOW, ATTN_KV)
    kv_idx = lambda b, j: (b, 0, 0)
    k_spec, k_sds, k_in, k_in_spec = _stacked_out(l, kstack, (batch, WINDOW, ATTN_KV), kv_block, kv_idx)
    v_spec, v_sds, v_in, v_in_spec = _stacked_out(l, vstack, (batch, WINDOW, ATTN_KV), kv_block, kv_idx)
    n_in = 6
    return pl.pallas_call(
        _attn_prompt_body,
        grid=(batch, nb),
        in_specs=[
            pl.BlockSpec(memory_space=pltpu.SMEM),
            pl.BlockSpec((rows_step, QKV_W), lambda b, j: (b * nb + j, 0)),
            pl.BlockSpec((rows_step, LANES), lambda b, j: (j, 0)),
            pl.BlockSpec((rows_step, LANES), lambda b, j: (j, 0)),
            _const_spec((1, QK_W)),
            _const_spec((LANES, LANES)),
        ] + k_in_spec + v_in_spec,
        out_specs=[pl.BlockSpec((rows_step, ATTN_Q), lambda b, j: (b * nb + j, 0)), k_spec, v_spec],
        out_shape=[jax.ShapeDtypeStruct((rows, ATTN_Q), BF16), k_sds, v_sds],
        input_output_aliases={n_in: 1, n_in + 1: 2} if l else {},
        scratch_shapes=[pltpu.VMEM((ATTN_BLOCK, ATTN_KV), BF16),
                        pltpu.VMEM((ATTN_BLOCK, ATTN_KV), BF16)],
        compiler_params=pltpu.CompilerParams(
            dimension_semantics=("parallel", "arbitrary"), vmem_limit_bytes=VMEM_LIMIT),
        name="attn_prompt",
    )(sinks, qkv, cos, sin, nw, ones_bd, *k_in, *v_in)


SAMPLE_BB = 16


def _attn_sample_body(x_ref, kc_ref, vc_ref, cos_ref, sin_ref, nw_ref, ones_ref, sinks_ref, *rest):
    o_ref, kout_ref, vout_ref = rest[-3:]
    x = x_ref[...]
    qk = _norm_rope(x[:, :QK_W], ones_ref[...], nw_ref[...], cos_ref[...], sin_ref[...])
    knew = qk[:, ATTN_Q:QK_W]
    vnew = x[:, QK_W:QKV_W]
    scale = 1.0 / math.sqrt(HEAD_DIM)
    row = lax.broadcasted_iota(jnp.int32, (N_HEADS, LANES), 0)
    lane = lax.broadcasted_iota(jnp.int32, (N_HEADS, LANES), 1)
    own = (lane < HEAD_DIM) == ((row & 1) == 0)
    lo1 = lax.broadcasted_iota(jnp.int32, (1, LANES), 1) < HEAD_DIM
    sinks = sinks_ref[...]
    units = range(SAMPLE_BB)
    qz, s_all, soft = {}, {}, {}
    for b in units:
        qrows = [qk[b:b + 1, (i // 2) * LANES:(i // 2 + 1) * LANES] for i in range(N_HEADS)]
        qz[b] = jnp.where(own, jnp.concatenate(qrows, axis=0) * scale, 0.0)
        s_all[b] = _dot(qz[b].astype(BF16), kc_ref[b].reshape(ATTN_KV, WINDOW).astype(BF16))
    sink = sinks[:, 0:1]
    for b in units:
        s = s_all.pop(b)
        s_new = jnp.sum(qz.pop(b) * knew[b:b + 1, :], axis=1, keepdims=True)
        m = jnp.maximum(jnp.maximum(jnp.max(s, axis=1, keepdims=True), s_new), sink)
        p = jnp.exp(s - m)
        p_new = jnp.exp(s_new - m)
        soft[b] = (p.astype(BF16), p_new, jnp.sum(p, axis=1, keepdims=True) + p_new + jnp.exp(sink - m))
    o_rows = []
    for b in units:
        p, p_new, denom = soft.pop(b)
        pv = (_dot_nt(p, vc_ref[b].reshape(ATTN_KV, WINDOW).astype(BF16)) + p_new * vnew[b:b + 1, :]) / denom
        o_rows.append(jnp.concatenate(
            [jnp.where(lo1, pv[2 * g:2 * g + 1, :], pv[2 * g + 1:2 * g + 2, :]) for g in range(GROUP)],
            axis=1))
    pad = jnp.zeros((LANES - SAMPLE_BB, ATTN_KV), F32)
    knew_t = jnp.concatenate([knew, pad], axis=0).T
    vnew_t = jnp.concatenate([vnew, pad], axis=0).T
    newest = lax.broadcasted_iota(jnp.int32, (ATTN_KV, WINDOW), 1) == WINDOW - 1
    kv_block = (N_KV_HEADS, HEAD_DIM, WINDOW)
    for b in units:
        for src_ref, new_t, dst_ref in ((kc_ref, knew_t, kout_ref), (vc_ref, vnew_t, vout_ref)):
            shifted = pltpu.roll(src_ref[b].reshape(ATTN_KV, WINDOW), WINDOW - 1, 1)
            dst_ref[b] = jnp.where(newest, new_t[:, b:b + 1], shifted).reshape(kv_block)
    o_ref[...] = jnp.concatenate(o_rows, axis=0).astype(o_ref.dtype)


def _attn_sample(qkv, cache_k, cache_v, cos, sin, nw, ones_bd, sinks_b, oa, kstack, vstack,
                 l, row0, dbatch):
    bb = SAMPLE_BB
    r0 = row0 // bb
    kv_block = (bb, N_KV_HEADS, HEAD_DIM, WINDOW)
    kv_idx = lambda i: (i, 0, 0, 0)
    kv_shape = (dbatch, N_KV_HEADS, HEAD_DIM, WINDOW)
    k_spec, k_sds, k_in, k_in_spec = _stacked_out(l, kstack, kv_shape, kv_block, kv_idx)
    v_spec, v_sds, v_in, v_in_spec = _stacked_out(l, vstack, kv_shape, kv_block, kv_idx)
    n_in = 9
    return pl.pallas_call(
        _attn_sample_body,
        grid=(dbatch // bb,),
        in_specs=[
            pl.BlockSpec((bb, QKV_W), lambda i: (r0 + i, 0)),
            pl.BlockSpec((None,) + kv_block, lambda i: (l, i, 0, 0, 0)),
            pl.BlockSpec((None,) + kv_block, lambda i: (l, i, 0, 0, 0)),
            _const_spec((1, LANES)),
            _const_spec((1, LANES)),
            _const_spec((1, QK_W)),
            _const_spec((LANES, LANES)),
            _const_spec((N_HEADS, LANES)),
            pl.BlockSpec(memory_space=pl.ANY),
        ] + k_in_spec + v_in_spec,
        out_specs=[pl.BlockSpec((bb, ATTN_Q), lambda i: (r0 + i, 0)), k_spec, v_spec],
        out_shape=[jax.ShapeDtypeStruct(oa.shape, oa.dtype), k_sds, v_sds],
        input_output_aliases={8: 0, n_in: 1, n_in + 1: 2} if l else {8: 0},
        compiler_params=pltpu.CompilerParams(
            dimension_semantics=("parallel",), vmem_limit_bytes=VMEM_LIMIT),
        name="attn_sample",
    )(qkv, cache_k, cache_v, cos, sin, nw, ones_bd, sinks_b, oa, *k_in, *v_in)


def _softplus(x):
    return jnp.maximum(x, 0.0) + jnp.log(1.0 + jnp.exp(-jnp.abs(x)))


def _silu(x):
    return x * jax.nn.sigmoid(x)


def _l2n(x):
    return x * lax.rsqrt(jnp.sum(x * x, axis=-1, keepdims=True) + EPS)


def _l2n_mxu(x, ones):
    hi, lo = _split2(x * x)
    return x * lax.rsqrt(_dot(hi, ones) + _dot(lo, ones) + EPS)


def _gate_rows(ba, alog_row, dtb_row):
    beta = jax.nn.sigmoid(ba)
    g = -jnp.exp(alog_row) * _softplus(ba + dtb_row)
    return beta, g


GROUP_TILES = 4
GROUP_TOK = GROUP_TILES * TILE
CHUNKS = TILE // DN_CHUNK


def _gdn_prompt_body(raw_ref, z_ref, ba_ref, cw_ref, alog_ref, dtb_ref, onw_ref,
                     ltri_ref, lall_ref, lvl_ref, *rest, groups_per_seq):
    od_ref, sout_ref, cout_ref, xp, s_scr, ub_scr, wq_scr, kq_scr, egl_scr = rest[-9:]
    i = pl.program_id(0)
    n_groups = pl.num_programs(0) - 1
    ga = jnp.minimum(i, n_groups - 1)
    gb = jnp.maximum(i - 1, 0)
    slot_a = i % 2
    slot_b = 1 - slot_a

    @pl.when(i == 0)
    def _():
        ub_scr[1] = jnp.zeros(ub_scr.shape[1:], F32)
        wq_scr[1] = jnp.zeros(wq_scr.shape[1:], BF16)
        kq_scr[1] = jnp.zeros(kq_scr.shape[1:], BF16)
        egl_scr[1] = jnp.zeros(egl_scr.shape[1:], F32)

    @pl.when(ga % groups_per_seq == 0)
    def _():
        xp[0:SUBLANES, :] = jnp.zeros((SUBLANES, CONV_DIM), F32)

    @pl.when(gb % groups_per_seq == 0)
    def _():
        s_scr[...] = jnp.zeros_like(s_scr)

    @pl.when(ga % groups_per_seq == groups_per_seq - 1)
    def _():
        cout_ref[...] = raw_ref[GROUP_TOK - (CONV_W - 1):GROUP_TOK, :]

    ltri = ltri_ref[...]
    lall = lall_ref[...]
    ri = lax.broadcasted_iota(jnp.int32, (TILE, TILE), 0)
    ci = lax.broadcasted_iota(jnp.int32, (TILE, TILE), 1)
    same = (ri // DN_CHUNK) == (ci // DN_CHUNK)
    causal = same & (ci <= ri)
    strict = same & (ci < ri)
    eye_bf = (ri == ci).astype(F32).astype(BF16)
    zeros_half = jnp.zeros((DN_CHUNK, DN_DV), BF16)
    onw = onw_ref[...]
    ones_sq = jnp.ones((DN_DK, LANES), BF16)
    units = [(j, h) for j in range(GROUP_TILES) for h in range(DN_HEADS)]

    st = [s_scr[h] for h in range(DN_HEADS)]
    b_live = {}

    def b_stage1(j, c):
        for h in range(DN_HEADS):
            idx = j * DN_HEADS + h
            r1 = _dot(wq_scr[slot_b, idx, 2 * c * DN_CHUNK:(2 * c + 2) * DN_CHUNK, :], st[h].astype(BF16))
            u = ub_scr[slot_b, idx, c * DN_CHUNK:(c + 1) * DN_CHUNK, :] - r1[:DN_CHUNK]
            parts = [zeros_half] * CHUNKS
            parts[c] = u.astype(BF16)
            b_live[h] = (jnp.concatenate(parts, axis=0), r1[DN_CHUNK:])

    def b_stage2(j, c):
        egl_f = egl_scr[slot_b, j]
        for h in range(DN_HEADS):
            idx = j * DN_HEADS + h
            u_full, oq = b_live[h]
            base = c * (DN_DK + DN_CHUNK)
            r2 = _dot(kq_scr[slot_b, idx, base:base + DN_DK + DN_CHUNK, :], u_full)
            o = oq + r2[DN_DK:]
            zh = z_ref[j * TILE + c * DN_CHUNK:j * TILE + (c + 1) * DN_CHUNK, h * DN_DV:(h + 1) * DN_DV]
            od_ref[j * TILE + c * DN_CHUNK:j * TILE + (c + 1) * DN_CHUNK, h * DN_DV:(h + 1) * DN_DV] = (
                _rms(o, onw) * _silu(zh)).astype(od_ref.dtype)
            st[h] = (st[h] * egl_f[c * DN_CHUNK:c * DN_CHUNK + 1, DN_HEADS + h:DN_HEADS + h + 1]
                     + r2[:DN_DK])

    b_stages = []
    for j in range(GROUP_TILES):
        for c in range(CHUNKS):
            b_stages.append(functools.partial(b_stage1, j, c))
            b_stages.append(functools.partial(b_stage2, j, c))
    b_iter = iter(b_stages)

    def b_step():
        f = next(b_iter, None)
        if f is not None:
            f()

    xp[SUBLANES:SUBLANES + GROUP_TOK, :] = raw_ref[...]
    ys = []
    for j in range(GROUP_TILES):
        r0 = SUBLANES + j * TILE
        y = xp[r0:r0 + TILE, :] * cw_ref[CONV_W - 1:CONV_W, :]
        for s in range(1, CONV_W):
            y = y + xp[r0 - s:r0 - s + TILE, :] * cw_ref[CONV_W - 1 - s:CONV_W - s, :]
        ys.append(_silu(y))
        b_step()
    xp[0:SUBLANES, :] = raw_ref[GROUP_TOK - SUBLANES:GROUP_TOK, :]

    tiles = []
    for j in range(GROUP_TILES):
        beta_f, g_f = _gate_rows(ba_ref[j * TILE:(j + 1) * TILE, :], alog_ref[...], dtb_ref[...])
        g_parts = _split3(g_f)
        gcum = sum(_dot(ltri, part) for part in g_parts)
        glast = sum(_dot(lall, part) for part in g_parts)
        egl_scr[slot_a, j] = jnp.exp(glast)
        tiles.append((beta_f, gcum, glast, gcum.T))

    u = {}
    for (j, h) in units:
        beta_f, gcum, glast, gcum_t = tiles[j]
        y = ys[j]
        k = _l2n_mxu(y[:, DN_QK + h * DN_DK:DN_QK + (h + 1) * DN_DK], ones_sq)
        gc = gcum[:, DN_HEADS + h:DN_HEADS + h + 1]
        gr = gcum_t[DN_HEADS + h:DN_HEADS + h + 1, :]
        u[j, h] = dict(k=k, kb=k.astype(BF16), beta=beta_f[:, h:h + 1], gc=gc,
                       gl=glast[:, DN_HEADS + h:DN_HEADS + h + 1],
                       decay=jnp.exp(jnp.where(causal, gc - gr, -jnp.inf)))
    for un in units:
        d = u[un]
        d["kk"] = _dot_nt(d["kb"], d["kb"])
    b_step()
    for un in units:
        d = u[un]
        a = jnp.where(strict, d["beta"] * d.pop("kk") * d["decay"], 0.0)
        d["ab"] = a.astype(BF16)
        d["tb"] = eye_bf - d["ab"] * lvl_ref[0]
    for lv in range(1, 6):
        for un in units:
            d = u[un]
            d["p"] = _dot(d["tb"], d["ab"] * lvl_ref[lv]).astype(BF16)
        b_step()
        for un in units:
            d = u[un]
            d["tb"] = d["tb"] - _dot(d.pop("p"), d["tb"]).astype(BF16)
        b_step()
    for (j, h) in units:
        d = u[j, h]
        y = ys[j]
        v = y[:, 2 * DN_QK + h * DN_DV:2 * DN_QK + (h + 1) * DN_DV]
        d["eg"] = jnp.exp(d["gc"])
        rhs = jnp.concatenate([v * d["beta"], d["k"] * (d["beta"] * d["eg"])], axis=1).astype(BF16)
        d["sol"] = _dot(d.pop("tb"), rhs)
    b_step()
    for (j, h) in units:
        d = u[j, h]
        y = ys[j]
        q = _l2n_mxu(y[:, h * DN_DK:(h + 1) * DN_DK], ones_sq) * (DN_DK ** -0.5)
        d["qk"] = (_dot_nt(q.astype(BF16), d["kb"]) * d["decay"]).astype(BF16)
        d["qd"] = (q * d["eg"]).astype(BF16)
    b_step()
    for _ in range(len(b_stages)):
        b_step()
    for (j, h) in units:
        d = u[j, h]
        idx = j * DN_HEADS + h
        sol = d["sol"]
        ub_scr[slot_a, idx] = sol[:, :DN_DV]
        w = sol[:, DN_DV:].astype(BF16)
        kd_t = (d["k"] * jnp.exp(d["gl"] - d["gc"])).T.astype(BF16)
        for c in range(CHUNKS):
            rows = slice(c * DN_CHUNK, (c + 1) * DN_CHUNK)
            wq_scr[slot_a, idx, 2 * c * DN_CHUNK:(2 * c + 1) * DN_CHUNK, :] = w[rows]
            wq_scr[slot_a, idx, (2 * c + 1) * DN_CHUNK:(2 * c + 2) * DN_CHUNK, :] = d["qd"][rows]
            base = c * (DN_DK + DN_CHUNK)
            kq_scr[slot_a, idx, base:base + DN_DK, :] = kd_t
            kq_scr[slot_a, idx, base + DN_DK:base + DN_DK + DN_CHUNK, :] = d["qk"][rows]

    for h in range(DN_HEADS):
        s_scr[h] = st[h]

    @pl.when(gb % groups_per_seq == groups_per_seq - 1)
    def _():
        for h in range(DN_HEADS):
            sout_ref[h] = st[h]


def _gdn_prompt(raw, z, ba, cw, alog_row, dtb_row, onw, consts, sstack, cstack, l, batch, seq, rows):
    gps = seq // GROUP_TOK
    ng = batch * gps
    s_spec, s_sds, s_in, s_in_spec = _stacked_out(
        l, sstack, (batch, DN_HEADS, DN_DK, DN_DV), (None, DN_HEADS, DN_DK, DN_DV),
        lambda i: (jnp.maximum(i - 1, 0) // gps, 0, 0, 0))
    c_spec, c_sds, c_in, c_in_spec = _stacked_out(
        l, cstack, (batch, CONV_W - 1, CONV_DIM), (None, CONV_W - 1, CONV_DIM),
        lambda i: (jnp.minimum(i, ng - 1) // gps, 0, 0))
    n_in = 10
    ltri, lall, lvl = consts
    nht = GROUP_TILES * DN_HEADS
    a_idx = lambda i: (jnp.minimum(i, ng - 1), 0)
    b_idx = lambda i: (jnp.maximum(i - 1, 0), 0)
    return pl.pallas_call(
        functools.partial(_gdn_prompt_body, groups_per_seq=gps),
        grid=(ng + 1,),
        in_specs=[
            pl.BlockSpec((GROUP_TOK, CONV_DIM), a_idx),
            pl.BlockSpec((GROUP_TOK, DN_V), b_idx),
            pl.BlockSpec((GROUP_TOK, LANES), a_idx),
            _layer_spec(l, (CONV_W, CONV_DIM)),
            _layer_spec(l, (1, LANES)),
            _layer_spec(l, (1, LANES)),
            _layer_spec(l, (1, DN_DV)),
            _const_spec((TILE, TILE)),
            _const_spec((TILE, TILE)),
            _const_spec((6, TILE, TILE)),
        ] + s_in_spec + c_in_spec,
        out_specs=[pl.BlockSpec((GROUP_TOK, DN_V), b_idx), s_spec, c_spec],
        out_shape=[jax.ShapeDtypeStruct((rows, DN_V), BF16), s_sds, c_sds],
        input_output_aliases={n_in: 1, n_in + 1: 2} if l else {},
        scratch_shapes=[
            pltpu.VMEM((SUBLANES + GROUP_TOK, CONV_DIM), F32),
            pltpu.VMEM((DN_HEADS, DN_DK, DN_DV), F32),
            pltpu.VMEM((2, nht, TILE, DN_DV), F32),
            pltpu.VMEM((2, nht, 2 * TILE, DN_DK), BF16),
            pltpu.VMEM((2, nht, CHUNKS * (DN_DK + DN_CHUNK), TILE), BF16),
            pltpu.VMEM((2, GROUP_TILES, TILE, LANES), F32),
        ],
        compiler_params=pltpu.CompilerParams(
            dimension_semantics=("arbitrary",), vmem_limit_bytes=VMEM_LIMIT),
        name="gdn_prompt",
    )(raw, z, ba, cw, alog_row, dtb_row, onw, ltri, lall, lvl, *s_in, *c_in)


def _gdn_sample_body(raw_ref, z_ref, ba_ref, cs_ref, st_ref, cw_ref, alog_ref, dtb_ref, onw_ref, *rest):
    od_ref, sout_ref, cout_ref = rest[-3:]
    bb = SAMPLE_BB
    raw = raw_ref[...]
    y = raw * cw_ref[CONV_W - 1:CONV_W, :]
    for i in range(CONV_W - 1):
        y = y + cs_ref[i] * cw_ref[i:i + 1, :]
    y = _silu(y)
    for i in range(CONV_W - 2):
        cout_ref[i] = cs_ref[i + 1]
    cout_ref[CONV_W - 2] = raw

    beta_f, g_f = _gate_rows(ba_ref[...], alog_ref[...], dtb_ref[...])
    eg_f = jnp.exp(g_f)
    pad = jnp.zeros((LANES - bb, DN_DK), F32)
    outs = []
    for h in range(DN_HEADS):
        q = _l2n(y[:, h * DN_DK:(h + 1) * DN_DK]) * (DN_DK ** -0.5)
        k = _l2n(y[:, DN_QK + h * DN_DK:DN_QK + (h + 1) * DN_DK])
        v = y[:, 2 * DN_QK + h * DN_DV:2 * DN_QK + (h + 1) * DN_DV]
        k_t = jnp.concatenate([k, pad], axis=0).T
        qk = jnp.sum(q * k, axis=1, keepdims=True)
        o_rows = []
        for b in range(bb):
            s1 = st_ref[b, h] * eg_f[b:b + 1, DN_HEADS + h:DN_HEADS + h + 1]
            kq = jnp.concatenate([k[b:b + 1, :], q[b:b + 1, :]], axis=0).astype(BF16)
            r = _dot(kq, s1.astype(BF16))
            delta = beta_f[b:b + 1, h:h + 1] * (v[b:b + 1, :] - r[0:1, :])
            sout_ref[b, h] = s1 + k_t[:, b:b + 1] * delta
            o_rows.append(r[1:2, :] + qk[b:b + 1, :] * delta)
        o = jnp.concatenate(o_rows, axis=0)
        zh = z_ref[:, h * DN_DV:(h + 1) * DN_DV]
        outs.append(_rms(o, onw_ref[...]) * _silu(zh))
    od_ref[...] = jnp.concatenate(outs, axis=1).astype(od_ref.dtype)


def _gdn_sample(raw, z, ba, conv_state, dn_state, cw, alog_row, dtb_row, onw, od, sstack, cstack,
                l, row0, dbatch):
    bb = SAMPLE_BB
    r0 = row0 // bb
    s_spec, s_sds, s_in, s_in_spec = _stacked_out(
        l, sstack, (dbatch, DN_HEADS, DN_DK, DN_DV), (bb, DN_HEADS, DN_DK, DN_DV), lambda i: (i, 0, 0, 0))
    c_spec, c_sds, c_in, c_in_spec = _stacked_out(
        l, cstack, (CONV_W - 1, dbatch, CONV_DIM), (CONV_W - 1, bb, CONV_DIM), lambda i: (0, i, 0))
    n_in = 10
    return pl.pallas_call(
        _gdn_sample_body,
        grid=(dbatch // bb,),
        in_specs=[
            pl.BlockSpec((bb, CONV_DIM), lambda i: (r0 + i, 0)),
            pl.BlockSpec((bb, DN_V), lambda i: (r0 + i, 0)),
            pl.BlockSpec((bb, LANES), lambda i: (r0 + i, 0)),
            pl.BlockSpec((None, CONV_W - 1, bb, CONV_DIM), lambda i: (l, 0, i, 0)),
            pl.BlockSpec((None, bb, DN_HEADS, DN_DK, DN_DV), lambda i: (l, i, 0, 0, 0)),
            _layer_spec(l, (CONV_W, CONV_DIM)),
            _layer_spec(l, (1, LANES)),
            _layer_spec(l, (1, LANES)),
            _layer_spec(l, (1, DN_DV)),
            pl.BlockSpec(memory_space=pl.ANY),
        ] + s_in_spec + c_in_spec,
        out_specs=[pl.BlockSpec((bb, DN_V), lambda i: (r0 + i, 0)), s_spec, c_spec],
        out_shape=[jax.ShapeDtypeStruct(od.shape, od.dtype), s_sds, c_sds],
        input_output_aliases={9: 0, n_in: 1, n_in + 1: 2} if l else {9: 0},
        compiler_params=pltpu.CompilerParams(
            dimension_semantics=("parallel",), vmem_limit_bytes=VMEM_LIMIT),
        name="gdn_sample",
    )(raw, z, ba, conv_state, dn_state, cw, alog_row, dtb_row, onw, od, *s_in, *c_in)


def _np_consts():
    i = np.arange(TILE)[:, None]
    j = np.arange(TILE)[None, :]
    same = (i // DN_CHUNK) == (j // DN_CHUNK)
    ltri = (same & (j <= i)).astype(np.float32)
    lall = same.astype(np.float32)
    lvls = []
    b = 1
    while b < DN_CHUNK:
        lvls.append(((i // (2 * b)) == (j // (2 * b))) & ((i // b) != (j // b)))
        b *= 2
    lvl = np.stack(lvls).astype(np.float32)
    hi = np.arange(LANES)
    ones_bd = (hi[:, None] // HEAD_DIM == hi[None, :] // HEAD_DIM).astype(np.float32)
    return ltri, lall, lvl, ones_bd


def _rope_tables(pos):
    half = HEAD_DIM // 2
    inv = 1.0 / (ROPE_THETA ** (jnp.arange(half, dtype=F32) / half))
    ang = pos.astype(F32)[:, None] * inv[None, :]
    cos, sin = jnp.cos(ang), jnp.sin(ang)
    cos_t = jnp.concatenate([cos, cos] * (LANES // HEAD_DIM), axis=1)
    sin_t = jnp.concatenate([-sin, sin] * (LANES // HEAD_DIM), axis=1)
    return cos_t, sin_t


def _pad_row(x, offset):
    out = jnp.zeros((x.shape[0], 1, LANES), F32)
    return out.at[:, 0, offset:offset + x.shape[1]].set(x.astype(F32))


def kernel(x_prompt, x_sample, cache_swa_k, cache_swa_v, state_dn, state_conv, ffn1_norm, ffn1_w_gate_up, ffn1_w_down, mix_norm, w_in, q_norm, k_norm, attn_sinks, conv_w, dn_A_log, dn_dt_bias, dn_out_norm, w_attn_o, w_dn_o, w_out, ffn2_norm, ffn2_w_gate_up, ffn2_w_down):
    batch, seq, _ = x_prompt.shape
    dbatch = x_sample.shape[0]
    prows = batch * seq
    rows = prows + dbatch
    perm = np.asarray(HEAD_PERM)

    w_in_t = jnp.transpose(w_in, (0, 2, 1))
    wao = w_attn_o.reshape(DEPTH, N_HEADS, HEAD_DIM, D_MODEL)[:, perm].reshape(DEPTH, ATTN_Q, D_MODEL).astype(BF16)
    wdo = w_dn_o.astype(BF16)
    wout = w_out.astype(BF16)
    wgu1, wd1, wgu2, wd2 = ffn1_w_gate_up, ffn1_w_down, ffn2_w_gate_up, ffn2_w_down
    n1 = ffn1_norm.reshape(DEPTH, 1, D_MODEL)
    n2 = ffn2_norm.reshape(DEPTH, 1, D_MODEL)
    nm = mix_norm.reshape(DEPTH, 1, D_MODEL)
    qk_nw = jnp.concatenate([jnp.tile(q_norm, (1, N_HEADS)), jnp.tile(k_norm, (1, N_KV_HEADS))], axis=1)
    alog_row = _pad_row(dn_A_log, DN_HEADS)
    dtb_row = _pad_row(dn_dt_bias, DN_HEADS)
    onw = dn_out_norm.reshape(DEPTH, 1, DN_DV)
    sinks_perm = attn_sinks[:, perm]
    sinks_b = jnp.broadcast_to(sinks_perm[:, :, None], (DEPTH, N_HEADS, LANES))

    ltri, lall, lvl, ones_bd = _np_consts()
    gdn_consts = (jnp.asarray(ltri, BF16), jnp.asarray(lall, BF16), jnp.asarray(lvl, BF16))
    ones_bd = jnp.asarray(ones_bd, BF16)
    cos_p, sin_p = _rope_tables(jnp.arange(seq))
    cos_s, sin_s = _rope_tables(PAST_LEN + jnp.arange(1))

    ck = jnp.transpose(cache_swa_k, (0, 1, 3, 4, 2))
    cv = jnp.transpose(cache_swa_v, (0, 1, 3, 4, 2))
    cs = jnp.transpose(state_conv, (0, 2, 1, 3))

    split_rows = _can_split_rows(prows, dbatch)
    xp2, xs2 = x_prompt.reshape(prows, D_MODEL), x_sample.reshape(dbatch, D_MODEL)
    kp = vp = sp = cp = ksn = vsn = ssn = csn = None
    for l in range(DEPTH):
        if l > 0:
            h = _ffn(h, n1, wgu1, wd1, l)
        elif split_rows:
            h = _ffn_first(xp2, xs2, n1, wgu1, wd1, l)
        else:
            h = _ffn(jnp.concatenate([xp2, xs2], axis=0), n1, wgu1, wd1, l)
        qkv, raw, z, ba, gates = _inproj(h, nm, w_in_t, l)
        oa, kp, vp = _attn_prompt(qkv, attn_sinks[l], cos_p, sin_p, qk_nw[l:l + 1], ones_bd,
                                  kp, vp, l, batch, seq, rows)
        oa, ksn, vsn = _attn_sample(qkv, ck, cv, cos_s, sin_s, qk_nw[l:l + 1], ones_bd,
                                    sinks_b[l], oa, ksn, vsn, l, prows, dbatch)
        od, sp, cp = _gdn_prompt(raw, z, ba, conv_w, alog_row, dtb_row, onw, gdn_consts,
                                 sp, cp, l, batch, seq, rows)
        od, ssn, csn = _gdn_sample(raw, z, ba, cs, state_dn, conv_w, alog_row, dtb_row, onw,
                                   od, ssn, csn, l, prows, dbatch)
        final_split = (prows, dbatch) if (split_rows and l == DEPTH - 1) else None
        h = _merge_ffn(h, oa, od, gates, wao, wdo, wout, n2, wgu2, wd2, l, split=final_split)

    yp, ys = h if split_rows else (h[:prows], h[prows:])
    kv_shape = (DEPTH, -1, WINDOW, N_KV_HEADS, HEAD_DIM)
    return (yp.reshape(batch, seq, D_MODEL),
            ys.reshape(dbatch, 1, D_MODEL),
            kp.reshape(kv_shape), vp.reshape(kv_shape), sp, cp,
            jnp.transpose(ksn, (0, 1, 4, 2, 3)), jnp.transpose(vsn, (0, 1, 4, 2, 3)), ssn,
            jnp.transpose(csn, (0, 2, 1, 3)))
```

```python
import functools
import math

import numpy as np
import jax
import jax.numpy as jnp
from jax import lax
from jax.experimental import pallas as pl
from jax.experimental.pallas import tpu as pltpu

F32 = jnp.float32
BF16 = jnp.bfloat16

D_MODEL = 1024
DEPTH = 4
PAST_LEN = 8192
N_HEADS = 8
N_KV_HEADS = 2
GROUP = N_HEADS // N_KV_HEADS
HEAD_DIM = 64
WINDOW = 128
ATTN_BLOCK = 128
ROPE_THETA = 10000.0
DN_HEADS = 4
DN_DK = 128
DN_DV = 128
CONV_W = 4
DN_CHUNK = 64
D_FF = 2816
EPS = 1e-6
LOG2E = math.log2(math.e)

ATTN_Q = N_HEADS * HEAD_DIM
ATTN_KV = N_KV_HEADS * HEAD_DIM
DN_QK = DN_HEADS * DN_DK
DN_V = DN_HEADS * DN_DV
CONV_DIM = 2 * DN_QK + DN_V
QKV_W = ATTN_Q + 2 * ATTN_KV
QK_W = ATTN_Q + ATTN_KV
LANES = 128
SUBLANES = 8
TILE = 128
VMEM_LIMIT = 60 * 1024 * 1024

HEAD_PERM = (0, 4, 1, 5, 2, 6, 3, 7)
MXU_COLS = 256
FF_CHUNKS = (3 * MXU_COLS,) * 3 + (2 * MXU_COLS,)


def _row_tile(rows, cap=512):
    best = SUBLANES
    for t in range(SUBLANES, cap + 1, SUBLANES):
        if rows % t == 0:
            best = t
    return best


def _rms(x, w):
    return x * lax.rsqrt(jnp.mean(x * x, axis=-1, keepdims=True) + EPS) * w


def _dot(a, b):
    return jnp.dot(a, b, preferred_element_type=F32)


def _dot_nt(a, b):
    return lax.dot_general(a, b, (((1,), (1,)), ((), ())), preferred_element_type=F32)


def _split2(x):
    hi = x.astype(BF16)
    return hi, (x - hi.astype(F32)).astype(BF16)


def _split3(x):
    hi = x.astype(BF16)
    r = x - hi.astype(F32)
    mid = r.astype(BF16)
    lo = (r - mid.astype(F32)).astype(BF16)
    return hi, mid, lo


def _const_spec(shape):
    nd = len(shape)
    return pl.BlockSpec(shape, lambda *_: (0,) * nd, pipeline_mode=pl.Buffered(1))


def _layer_spec(l, shape):
    nd = len(shape)
    return pl.BlockSpec((None,) + shape, lambda *_: (l,) + (0,) * nd, pipeline_mode=pl.Buffered(1))


def _stacked_out(l, prev, shape, block, index_map):
    spec = pl.BlockSpec((None,) + block, lambda *idx: (l,) + tuple(index_map(*idx)))
    sds = jax.ShapeDtypeStruct((DEPTH,) + shape, F32)
    if l == 0:
        return spec, sds, [], []
    return spec, sds, [prev], [pl.BlockSpec(memory_space=pl.ANY)]


def _swiglu_residual(x, nw, wgu_ref, wd_ref):
    xn = _rms(x, nw).astype(BF16)
    acc = jnp.zeros_like(x)
    c0 = 0
    for tf in FF_CHUNKS:
        g = _dot(xn, wgu_ref[:, c0:c0 + tf].astype(BF16))
        u = _dot(xn, wgu_ref[:, D_FF + c0:D_FF + c0 + tf].astype(BF16))
        a = (g * jax.nn.sigmoid(g) * u).astype(BF16)
        acc = acc + _dot(a, wd_ref[c0:c0 + tf, :].astype(BF16))
        c0 += tf
    return x + 0.5 * acc


def _ffn_body(x_ref, nw_ref, wgu_ref, wd_ref, o_ref):
    o_ref[...] = _swiglu_residual(x_ref[...], nw_ref[...], wgu_ref, wd_ref)


def _ffn_first_body(*refs):
    *piece_refs, xs_ref, nw_ref, wgu_ref, wd_ref, o_ref = refs
    last = pl.program_id(0) == pl.num_programs(0) - 1
    pieces = [r[...] for r in piece_refs]
    pieces[-1] = jnp.where(last, xs_ref[...], pieces[-1])
    o_ref[...] = _swiglu_residual(jnp.concatenate(pieces, axis=0), nw_ref[...], wgu_ref, wd_ref)


def _ffn_first(xp, xs, norm_w, wgu, wd, l):
    prows, dbatch = xp.shape[0], xs.shape[0]
    rows = prows + dbatch
    tm = _row_tile(rows)
    n_piece = tm // dbatch
    last_piece = prows // dbatch - 1
    piece_spec = lambda p: pl.BlockSpec(
        (dbatch, D_MODEL), lambda i: (jnp.minimum(i * n_piece + p, last_piece), 0))
    return pl.pallas_call(
        _ffn_first_body,
        grid=(rows // tm,),
        in_specs=[piece_spec(p) for p in range(n_piece)] + [
            _const_spec((dbatch, D_MODEL)),
            _layer_spec(l, (1, D_MODEL)),
            _layer_spec(l, (D_MODEL, 2 * D_FF)),
            _layer_spec(l, (D_FF, D_MODEL)),
        ],
        out_specs=pl.BlockSpec((tm, D_MODEL), lambda i: (i, 0)),
        out_shape=jax.ShapeDtypeStruct((rows, D_MODEL), F32),
        compiler_params=pltpu.CompilerParams(
            dimension_semantics=("parallel",), vmem_limit_bytes=VMEM_LIMIT),
        name="ffn_first",
    )(*([xp] * n_piece), xs, norm_w, wgu, wd)


def _can_split_rows(prows, dbatch):
    tm = _row_tile(prows + dbatch)
    return tm % dbatch == 0 and prows % dbatch == 0 and dbatch % SUBLANES == 0


def _ffn(x, norm_w, wgu, wd, l):
    rows = x.shape[0]
    tm = _row_tile(rows)
    return pl.pallas_call(
        _ffn_body,
        grid=(rows // tm,),
        in_specs=[
            pl.BlockSpec((tm, D_MODEL), lambda i: (i, 0)),
            _layer_spec(l, (1, D_MODEL)),
            _layer_spec(l, (D_MODEL, 2 * D_FF)),
            _layer_spec(l, (D_FF, D_MODEL)),
        ],
        out_specs=pl.BlockSpec((tm, D_MODEL), lambda i: (i, 0)),
        out_shape=jax.ShapeDtypeStruct((rows, D_MODEL), F32),
        compiler_params=pltpu.CompilerParams(
            dimension_semantics=("parallel",), vmem_limit_bytes=VMEM_LIMIT),
        name="ffn",
    )(x, norm_w, wgu, wd)


INPROJ_ROW_CAP = 768


IN_SEGS = (QKV_W, CONV_DIM, DN_V, LANES, 2 * D_MODEL)
IN_DTYPES = (F32, F32, F32, F32, BF16)
IN_COLS = QKV_W + CONV_DIM + DN_V + 2 * DN_HEADS + 2 * D_MODEL
IN_ALIGNED = QKV_W + CONV_DIM + DN_V


def _inproj_body(h_ref, nw_ref, wt_ref, qkv_ref, raw_ref, z_ref, ba_ref, gates_ref):
    u = _rms(h_ref[...], nw_ref[...]).astype(BF16)

    def proj(r0, r1):
        return _dot_nt(u, wt_ref[r0:r1, :].astype(BF16))

    qkv = proj(0, QKV_W)
    lo = lax.broadcasted_iota(jnp.int32, (qkv.shape[0], LANES), 1) < HEAD_DIM
    nat = [qkv[:, c * LANES:(c + 1) * LANES] for c in range(ATTN_Q // LANES)]
    for g in range(GROUP):
        first, second = nat[g // 2], nat[(GROUP + g) // 2]
        if g % 2 == 0:
            second = pltpu.roll(second, HEAD_DIM, 1)
        else:
            first = pltpu.roll(first, HEAD_DIM, 1)
        qkv_ref[:, g * LANES:(g + 1) * LANES] = jnp.where(lo, first, second)
    qkv_ref[:, ATTN_Q:] = qkv[:, ATTN_Q:]
    raw_ref[...] = proj(QKV_W, QKV_W + CONV_DIM)
    z_ref[...] = proj(QKV_W + CONV_DIM, IN_ALIGNED)
    ba_ref[...] = proj(IN_ALIGNED, IN_ALIGNED + LANES)
    gates_ref[...] = proj(IN_ALIGNED + 2 * DN_HEADS, IN_COLS).astype(gates_ref.dtype)


def _inproj(h, norm_w, w_in_t, l):
    rows = h.shape[0]
    tm = _row_tile(rows, cap=INPROJ_ROW_CAP)
    return pl.pallas_call(
        _inproj_body,
        grid=(rows // tm,),
        in_specs=[
            pl.BlockSpec((tm, D_MODEL), lambda i: (i, 0)),
            _layer_spec(l, (1, D_MODEL)),
            _layer_spec(l, (IN_COLS, D_MODEL)),
        ],
        out_specs=[pl.BlockSpec((tm, w), lambda i: (i, 0)) for w in IN_SEGS],
        out_shape=[jax.ShapeDtypeStruct((rows, w), dt) for w, dt in zip(IN_SEGS, IN_DTYPES)],
        compiler_params=pltpu.CompilerParams(
            dimension_semantics=("parallel",), vmem_limit_bytes=VMEM_LIMIT),
        name="inproj",
    )(h, norm_w, w_in_t)


def _merge_ffn_body(h_ref, oa_ref, od_ref, gates_ref, wao_ref, wdo_ref, wout_ref,
                    nw_ref, wgu_ref, wd_ref, o_ref, *sample_out):
    br_a = _dot(oa_ref[...], wao_ref[...])
    br_d = _dot(od_ref[...], wdo_ref[...])
    ga = gates_ref[:, :D_MODEL].astype(F32)
    gd = gates_ref[:, D_MODEL:].astype(F32)
    m = jax.nn.sigmoid(ga) * br_a + jax.nn.sigmoid(gd) * br_d
    h = h_ref[...] + _dot(m.astype(BF16), wout_ref[...])
    y = _swiglu_residual(h, nw_ref[...], wgu_ref, wd_ref)
    o_ref[...] = y
    if sample_out:
        (os_ref,) = sample_out

        @pl.when(pl.program_id(0) == pl.num_programs(0) - 1)
        def _():
            os_ref[...] = y[y.shape[0] - os_ref.shape[0]:, :]


def _merge_ffn(h, oa, od, gates, wao, wdo, wout, norm_w, wgu, wd, l, split=None):
    rows = h.shape[0]
    tm = _row_tile(rows)
    out_specs = pl.BlockSpec((tm, D_MODEL), lambda i: (i, 0))
    out_shape = jax.ShapeDtypeStruct((rows, D_MODEL), F32)
    if split is not None:
        prows, dbatch = split
        out_specs = [out_specs, pl.BlockSpec((dbatch, D_MODEL), lambda i: (0, 0))]
        out_shape = [jax.ShapeDtypeStruct((prows, D_MODEL), F32), jax.ShapeDtypeStruct((dbatch, D_MODEL), F32)]
    return pl.pallas_call(
        _merge_ffn_body,
        grid=(rows // tm,),
        in_specs=[
            pl.BlockSpec((tm, D_MODEL), lambda i: (i, 0)),
            pl.BlockSpec((tm, ATTN_Q), lambda i: (i, 0)),
            pl.BlockSpec((tm, DN_V), lambda i: (i, 0)),
            pl.BlockSpec((tm, 2 * D_MODEL), lambda i: (i, 0)),
            _layer_spec(l, (ATTN_Q, D_MODEL)),
            _layer_spec(l, (DN_V, D_MODEL)),
            _layer_spec(l, (D_MODEL, D_MODEL)),
            _layer_spec(l, (1, D_MODEL)),
            _layer_spec(l, (D_MODEL, 2 * D_FF)),
            _layer_spec(l, (D_FF, D_MODEL)),
        ],
        out_specs=out_specs,
        out_shape=out_shape,
        compiler_params=pltpu.CompilerParams(
            dimension_semantics=("arbitrary",), vmem_limit_bytes=VMEM_LIMIT),
        name="merge_ffn",
    )(h, oa, od, gates, wao, wdo, wout, norm_w, wgu, wd)


def _norm_rope(x, ones_bd, nw, cos, sin):
    lane = lax.broadcasted_iota(jnp.int32, (x.shape[0], LANES), 1)
    first_half = (lane & (HEAD_DIM // 2)) == 0
    outs = []
    for g in range(x.shape[1] // LANES):
        xg = x[:, g * LANES:(g + 1) * LANES]
        hi, lo = _split2(xg * xg)
        ssq = _dot(hi, ones_bd) + _dot(lo, ones_bd)
        xn = xg * lax.rsqrt(ssq * (1.0 / HEAD_DIM) + EPS) * nw[:, g * LANES:(g + 1) * LANES]
        partner = jnp.where(first_half,
                            pltpu.roll(xn, LANES - HEAD_DIM // 2, 1),
                            pltpu.roll(xn, HEAD_DIM // 2, 1))
        outs.append(xn * cos + partner * sin)
    return jnp.concatenate(outs, axis=1)


ATTN_STEP_BLOCKS = 2


def _attn_prompt_body(sinks_ref, x_ref, cos_ref, sin_ref, nw_ref, ones_ref, *rest):
    o_ref, kout_ref, vout_ref, kprev, vprev = rest[-5:]
    step = pl.program_id(1)

    @pl.when(step == 0)
    def _():
        kprev[...] = jnp.zeros_like(kprev)
        vprev[...] = jnp.zeros_like(vprev)

    x = x_ref[...]
    qk = _norm_rope(x[:, :QK_W], ones_ref[...], nw_ref[...], cos_ref[...], sin_ref[...])
    k_all = qk[:, ATTN_Q:QK_W]
    v_all = x[:, QK_W:QKV_W]
    k_bf = jnp.concatenate([kprev[...], k_all.astype(BF16)], axis=0)
    v_bf = jnp.concatenate([vprev[...], v_all.astype(BF16)], axis=0)

    r = lax.broadcasted_iota(jnp.int32, (ATTN_BLOCK, 2 * ATTN_BLOCK), 0)
    c = lax.broadcasted_iota(jnp.int32, (ATTN_BLOCK, 2 * ATTN_BLOCK), 1)
    band = (c >= r) & (c <= r + WINDOW)
    first_col = jnp.where(step == 0, ATTN_BLOCK, 0)
    lo = lax.broadcasted_iota(jnp.int32, (ATTN_BLOCK, LANES), 1) < HEAD_DIM
    scale = LOG2E / math.sqrt(HEAD_DIM)
    klane = lax.broadcasted_iota(jnp.int32, k_bf.shape, 1) < HEAD_DIM
    k_half = (jnp.where(klane, k_bf, jnp.zeros_like(k_bf)), jnp.where(klane, jnp.zeros_like(k_bf), k_bf))

    units = [(sb, g, half) for sb in range(ATTN_STEP_BLOCKS) for g in range(GROUP) for half in range(2)]
    q_bf = {(sb, g): (qk[sb * ATTN_BLOCK:(sb + 1) * ATTN_BLOCK, g * LANES:(g + 1) * LANES] * scale).astype(BF16)
            for sb in range(ATTN_STEP_BLOCKS) for g in range(GROUP)}
    s_raw = {}
    for (sb, g, half) in units:
        s_raw[sb, g, half] = _dot_nt(q_bf[sb, g], k_half[half][sb * ATTN_BLOCK:(sb + 2) * ATTN_BLOCK])
    p_den = {}
    for (sb, g, half) in units:
        vis = band & (c >= first_col) if sb == 0 else band
        s = jnp.where(vis, s_raw.pop((sb, g, half)), -jnp.inf)
        sink = sinks_ref[HEAD_PERM[2 * g + half]] * LOG2E
        m = jnp.maximum(jnp.max(s, axis=1, keepdims=True), sink)
        p = jnp.exp2(s - m)
        p_den[sb, g, half] = (p.astype(BF16), jnp.sum(p, axis=1, keepdims=True) + jnp.exp2(sink - m))
    for sb in range(ATTN_STEP_BLOCKS):
        outs = []
        for g in range(GROUP):
            res = []
            for half in range(2):
                p, denom = p_den.pop((sb, g, half))
                res.append(_dot(p, v_bf[sb * ATTN_BLOCK:(sb + 2) * ATTN_BLOCK]) / denom)
            outs.append(jnp.where(lo, res[0], res[1]))
        o_ref[sb * ATTN_BLOCK:(sb + 1) * ATTN_BLOCK, :] = jnp.concatenate(outs, axis=1).astype(o_ref.dtype)

    last = slice((ATTN_STEP_BLOCKS - 1) * ATTN_BLOCK, ATTN_STEP_BLOCKS * ATTN_BLOCK)
    kprev[...] = k_all[last].astype(BF16)
    vprev[...] = v_all[last].astype(BF16)

    @pl.when(step == pl.num_programs(1) - 1)
    def _():
        kout_ref[...] = k_all[last]
        vout_ref[...] = v_all[last]


def _attn_prompt(qkv, sinks, cos, sin, nw, ones_bd, kstack, vstack, l, batch, seq, rows):
    rows_step = ATTN_STEP_BLOCKS * ATTN_BLOCK
    nb = seq // rows_step
    kv_block = (None, WINDOW, ATTN_KV)
    kv_idx = lambda b, j: (b, 0, 0)
    k_spec, k_sds, k_in, k_in_spec = _stacked_out(l, kstack, (batch, WINDOW, ATTN_KV), kv_block, kv_idx)
    v_spec, v_sds, v_in, v_in_spec = _stacked_out(l, vstack, (batch, WINDOW, ATTN_KV), kv_block, kv_idx)
    n_in = 6
    return pl.pallas_call(
        _attn_prompt_body,
        grid=(batch, nb),
        in_specs=[
            pl.BlockSpec(memory_space=pltpu.SMEM),
            pl.BlockSpec((rows_step, QKV_W), lambda b, j: (b * nb + j, 0)),
            pl.BlockSpec((rows_step, LANES), lambda b, j: (j, 0)),
            pl.BlockSpec((rows_step, LANES), lambda b, j: (j, 0)),
            _const_spec((1, QK_W)),
            _const_spec((LANES, LANES)),
        ] + k_in_spec + v_in_spec,
        out_specs=[pl.BlockSpec((rows_step, ATTN_Q), lambda b, j: (b * nb + j, 0)), k_spec, v_spec],
        out_shape=[jax.ShapeDtypeStruct((rows, ATTN_Q), BF16), k_sds, v_sds],
        input_output_aliases={n_in: 1, n_in + 1: 2} if l else {},
        scratch_shapes=[pltpu.VMEM((ATTN_BLOCK, ATTN_KV), BF16),
                        pltpu.VMEM((ATTN_BLOCK, ATTN_KV), BF16)],
        compiler_params=pltpu.CompilerParams(
            dimension_semantics=("parallel", "arbitrary"), vmem_limit_bytes=VMEM_LIMIT),
        name="attn_prompt",
    )(sinks, qkv, cos, sin, nw, ones_bd, *k_in, *v_in)


SAMPLE_BB = 16


def _attn_sample_body(x_ref, kc_ref, vc_ref, cos_ref, sin_ref, nw_ref, ones_ref, sinks_ref, *rest):
    o_ref, kout_ref, vout_ref = rest[-3:]
    x = x_ref[...]
    qk = _norm_rope(x[:, :QK_W], ones_ref[...], nw_ref[...], cos_ref[...], sin_ref[...])
    knew = qk[:, ATTN_Q:QK_W]
    vnew = x[:, QK_W:QKV_W]
    scale = 1.0 / math.sqrt(HEAD_DIM)
    row = lax.broadcasted_iota(jnp.int32, (N_HEADS, LANES), 0)
    lane = lax.broadcasted_iota(jnp.int32, (N_HEADS, LANES), 1)
    own = (lane < HEAD_DIM) == ((row & 1) == 0)
    lo1 = lax.broadcasted_iota(jnp.int32, (1, LANES), 1) < HEAD_DIM
    sinks = sinks_ref[...]
    units = range(SAMPLE_BB)
    qz, s_all, soft = {}, {}, {}
    for b in units:
        qrows = [qk[b:b + 1, (i // 2) * LANES:(i // 2 + 1) * LANES] for i in range(N_HEADS)]
        qz[b] = jnp.where(own, jnp.concatenate(qrows, axis=0) * scale, 0.0)
        s_all[b] = _dot(qz[b].astype(BF16), kc_ref[b].reshape(ATTN_KV, WINDOW).astype(BF16))
    sink = sinks[:, 0:1]
    for b in units:
        s = s_all.pop(b)
        s_new = jnp.sum(qz.pop(b) * knew[b:b + 1, :], axis=1, keepdims=True)
        m = jnp.maximum(jnp.maximum(jnp.max(s, axis=1, keepdims=True), s_new), sink)
        p = jnp.exp(s - m)
        p_new = jnp.exp(s_new - m)
        soft[b] = (p.astype(BF16), p_new, jnp.sum(p, axis=1, keepdims=True) + p_new + jnp.exp(sink - m))
    o_rows = []
    for b in units:
        p, p_new, denom = soft.pop(b)
        pv = (_dot_nt(p, vc_ref[b].reshape(ATTN_KV, WINDOW).astype(BF16)) + p_new * vnew[b:b + 1, :]) / denom
        o_rows.append(jnp.concatenate(
            [jnp.where(lo1, pv[2 * g:2 * g + 1, :], pv[2 * g + 1:2 * g + 2, :]) for g in range(GROUP)],
            axis=1))
    pad = jnp.zeros((LANES - SAMPLE_BB, ATTN_KV), F32)
    knew_t = jnp.concatenate([knew, pad], axis=0).T
    vnew_t = jnp.concatenate([vnew, pad], axis=0).T
    newest = lax.broadcasted_iota(jnp.int32, (ATTN_KV, WINDOW), 1) == WINDOW - 1
    kv_block = (N_KV_HEADS, HEAD_DIM, WINDOW)
    for b in units:
        for src_ref, new_t, dst_ref in ((kc_ref, knew_t, kout_ref), (vc_ref, vnew_t, vout_ref)):
            shifted = pltpu.roll(src_ref[b].reshape(ATTN_KV, WINDOW), WINDOW - 1, 1)
            dst_ref[b] = jnp.where(newest, new_t[:, b:b + 1], shifted).reshape(kv_block)
    o_ref[...] = jnp.concatenate(o_rows, axis=0).astype(o_ref.dtype)


def _attn_sample(qkv, cache_k, cache_v, cos, sin, nw, ones_bd, sinks_b, oa, kstack, vstack,
                 l, row0, dbatch):
    bb = SAMPLE_BB
    r0 = row0 // bb
    kv_block = (bb, N_KV_HEADS, HEAD_DIM, WINDOW)
    kv_idx = lambda i: (i, 0, 0, 0)
    kv_shape = (dbatch, N_KV_HEADS, HEAD_DIM, WINDOW)
    k_spec, k_sds, k_in, k_in_spec = _stacked_out(l, kstack, kv_shape, kv_block, kv_idx)
    v_spec, v_sds, v_in, v_in_spec = _stacked_out(l, vstack, kv_shape, kv_block, kv_idx)
    n_in = 9
    return pl.pallas_call(
        _attn_sample_body,
        grid=(dbatch // bb,),
        in_specs=[
            pl.BlockSpec((bb, QKV_W), lambda i: (r0 + i, 0)),
            pl.BlockSpec((None,) + kv_block, lambda i: (l, i, 0, 0, 0)),
            pl.BlockSpec((None,) + kv_block, lambda i: (l, i, 0, 0, 0)),
            _const_spec((1, LANES)),
            _const_spec((1, LANES)),
            _const_spec((1, QK_W)),
            _const_spec((LANES, LANES)),
            _const_spec((N_HEADS, LANES)),
            pl.BlockSpec(memory_space=pl.ANY),
        ] + k_in_spec + v_in_spec,
        out_specs=[pl.BlockSpec((bb, ATTN_Q), lambda i: (r0 + i, 0)), k_spec, v_spec],
        out_shape=[jax.ShapeDtypeStruct(oa.shape, oa.dtype), k_sds, v_sds],
        input_output_aliases={8: 0, n_in: 1, n_in + 1: 2} if l else {8: 0},
        compiler_params=pltpu.CompilerParams(
            dimension_semantics=("parallel",), vmem_limit_bytes=VMEM_LIMIT),
        name="attn_sample",
    )(qkv, cache_k, cache_v, cos, sin, nw, ones_bd, sinks_b, oa, *k_in, *v_in)


def _softplus(x):
    return jnp.maximum(x, 0.0) + jnp.log(1.0 + jnp.exp(-jnp.abs(x)))


def _silu(x):
    return x * jax.nn.sigmoid(x)


def _l2n(x):
    return x * lax.rsqrt(jnp.sum(x * x, axis=-1, keepdims=True) + EPS)


def _l2n_mxu(x, ones):
    hi, lo = _split2(x * x)
    return x * lax.rsqrt(_dot(hi, ones) + _dot(lo, ones) + EPS)


def _gate_rows(ba, alog_row, dtb_row):
    beta = jax.nn.sigmoid(ba)
    g = -jnp.exp(alog_row) * _softplus(ba + dtb_row)
    return beta, g


GROUP_TILES = 4
GROUP_TOK = GROUP_TILES * TILE
CHUNKS = TILE // DN_CHUNK


def _gdn_prompt_body(raw_ref, z_ref, ba_ref, cw_ref, alog_ref, dtb_ref, onw_ref,
                     ltri_ref, lall_ref, lvl_ref, *rest, groups_per_seq):
    od_ref, sout_ref, cout_ref, xp, s_scr, ub_scr, wq_scr, kq_scr, egl_scr = rest[-9:]
    i = pl.program_id(0)
    n_groups = pl.num_programs(0) - 1
    ga = jnp.minimum(i, n_groups - 1)
    gb = jnp.maximum(i - 1, 0)
    slot_a = i % 2
    slot_b = 1 - slot_a

    @pl.when(i == 0)
    def _():
        ub_scr[1] = jnp.zeros(ub_scr.shape[1:], F32)
        wq_scr[1] = jnp.zeros(wq_scr.shape[1:], BF16)
        kq_scr[1] = jnp.zeros(kq_scr.shape[1:], BF16)
        egl_scr[1] = jnp.zeros(egl_scr.shape[1:], F32)

    @pl.when(ga % groups_per_seq == 0)
    def _():
        xp[0:SUBLANES, :] = jnp.zeros((SUBLANES, CONV_DIM), F32)

    @pl.when(gb % groups_per_seq == 0)
    def _():
        s_scr[...] = jnp.zeros_like(s_scr)

    @pl.when(ga % groups_per_seq == groups_per_seq - 1)
    def _():
        cout_ref[...] = raw_ref[GROUP_TOK - (CONV_W - 1):GROUP_TOK, :]

    ltri = ltri_ref[...]
    lall = lall_ref[...]
    ri = lax.broadcasted_iota(jnp.int32, (TILE, TILE), 0)
    ci = lax.broadcasted_iota(jnp.int32, (TILE, TILE), 1)
    same = (ri // DN_CHUNK) == (ci // DN_CHUNK)
    causal = same & (ci <= ri)
    strict = same & (ci < ri)
    eye_bf = (ri == ci).astype(F32).astype(BF16)
    zeros_half = jnp.zeros((DN_CHUNK, DN_DV), BF16)
    onw = onw_ref[...]
    ones_sq = jnp.ones((DN_DK, LANES), BF16)
    units = [(j, h) for j in range(GROUP_TILES) for h in range(DN_HEADS)]

    st = [s_scr[h] for h in range(DN_HEADS)]
    b_live = {}

    def b_stage1(j, c):
        for h in range(DN_HEADS):
            idx = j * DN_HEADS + h
            r1 = _dot(wq_scr[slot_b, idx, 2 * c * DN_CHUNK:(2 * c + 2) * DN_CHUNK, :], st[h].astype(BF16))
            u = ub_scr[slot_b, idx, c * DN_CHUNK:(c + 1) * DN_CHUNK, :] - r1[:DN_CHUNK]
            parts = [zeros_half] * CHUNKS
            parts[c] = u.astype(BF16)
            b_live[h] = (jnp.concatenate(parts, axis=0), r1[DN_CHUNK:])

    def b_stage2(j, c):
        egl_f = egl_scr[slot_b, j]
        for h in range(DN_HEADS):
            idx = j * DN_HEADS + h
            u_full, oq = b_live[h]
            base = c * (DN_DK + DN_CHUNK)
            r2 = _dot(kq_scr[slot_b, idx, base:base + DN_DK + DN_CHUNK, :], u_full)
            o = oq + r2[DN_DK:]
            zh = z_ref[j * TILE + c * DN_CHUNK:j * TILE + (c + 1) * DN_CHUNK, h * DN_DV:(h + 1) * DN_DV]
            od_ref[j * TILE + c * DN_CHUNK:j * TILE + (c + 1) * DN_CHUNK, h * DN_DV:(h + 1) * DN_DV] = (
                _rms(o, onw) * _silu(zh)).astype(od_ref.dtype)
            st[h] = (st[h] * egl_f[c * DN_CHUNK:c * DN_CHUNK + 1, DN_HEADS + h:DN_HEADS + h + 1]
                     + r2[:DN_DK])

    b_stages = []
    for j in range(GROUP_TILES):
        for c in range(CHUNKS):
            b_stages.append(functools.partial(b_stage1, j, c))
            b_stages.append(functools.partial(b_stage2, j, c))
    b_iter = iter(b_stages)

    def b_step():
        f = next(b_iter, None)
        if f is not None:
            f()

    xp[SUBLANES:SUBLANES + GROUP_TOK, :] = raw_ref[...]
    ys = []
    for j in range(GROUP_TILES):
        r0 = SUBLANES + j * TILE
        y = xp[r0:r0 + TILE, :] * cw_ref[CONV_W - 1:CONV_W, :]
        for s in range(1, CONV_W):
            y = y + xp[r0 - s:r0 - s + TILE, :] * cw_ref[CONV_W - 1 - s:CONV_W - s, :]
        ys.append(_silu(y))
        b_step()
    xp[0:SUBLANES, :] = raw_ref[GROUP_TOK - SUBLANES:GROUP_TOK, :]

    tiles = []
    for j in range(GROUP_TILES):
        beta_f, g_f = _gate_rows(ba_ref[j * TILE:(j + 1) * TILE, :], alog_ref[...], dtb_ref[...])
        g_parts = _split3(g_f)
        gcum = sum(_dot(ltri, part) for part in g_parts)
        glast = sum(_dot(lall, part) for part in g_parts)
        egl_scr[slot_a, j] = jnp.exp(glast)
        tiles.append((beta_f, gcum, glast, gcum.T))

    u = {}
    for (j, h) in units:
        beta_f, gcum, glast, gcum_t = tiles[j]
        y = ys[j]
        k = _l2n_mxu(y[:, DN_QK + h * DN_DK:DN_QK + (h + 1) * DN_DK], ones_sq)
        gc = gcum[:, DN_HEADS + h:DN_HEADS + h + 1]
        gr = gcum_t[DN_HEADS + h:DN_HEADS + h + 1, :]
        u[j, h] = dict(k=k, kb=k.astype(BF16), beta=beta_f[:, h:h + 1], gc=gc,
                       gl=glast[:, DN_HEADS + h:DN_HEADS + h + 1],
                       decay=jnp.exp(jnp.where(causal, gc - gr, -jnp.inf)))
    for un in units:
        d = u[un]
        d["kk"] = _dot_nt(d["kb"], d["kb"])
    b_step()
    for un in units:
        d = u[un]
        a = jnp.where(strict, d["beta"] * d.pop("kk") * d["decay"], 0.0)
        d["ab"] = a.astype(BF16)
        d["tb"] = eye_bf - d["ab"] * lvl_ref[0]
    for lv in range(1, 6):
        for un in units:
            d = u[un]
            d["p"] = _dot(d["tb"], d["ab"] * lvl_ref[lv]).astype(BF16)
        b_step()
        for un in units:
            d = u[un]
            d["tb"] = d["tb"] - _dot(d.pop("p"), d["tb"]).astype(BF16)
        b_step()
    for (j, h) in units:
        d = u[j, h]
        y = ys[j]
        v = y[:, 2 * DN_QK + h * DN_DV:2 * DN_QK + (h + 1) * DN_DV]
        d["eg"] = jnp.exp(d["gc"])
        rhs = jnp.concatenate([v * d["beta"], d["k"] * (d["beta"] * d["eg"])], axis=1).astype(BF16)
        d["sol"] = _dot(d.pop("tb"), rhs)
    b_step()
    for (j, h) in units:
        d = u[j, h]
        y = ys[j]
        q = _l2n_mxu(y[:, h * DN_DK:(h + 1) * DN_DK], ones_sq) * (DN_DK ** -0.5)
        d["qk"] = (_dot_nt(q.astype(BF16), d["kb"]) * d["decay"]).astype(BF16)
        d["qd"] = (q * d["eg"]).astype(BF16)
    b_step()
    for _ in range(len(b_stages)):
        b_step()
    for (j, h) in units:
        d = u[j, h]
        idx = j * DN_HEADS + h
        sol = d["sol"]
        ub_scr[slot_a, idx] = sol[:, :DN_DV]
        w = sol[:, DN_DV:].astype(BF16)
        kd_t = (d["k"] * jnp.exp(d["gl"] - d["gc"])).T.astype(BF16)
        for c in range(CHUNKS):
            rows = slice(c * DN_CHUNK, (c + 1) * DN_CHUNK)
            wq_scr[slot_a, idx, 2 * c * DN_CHUNK:(2 * c + 1) * DN_CHUNK, :] = w[rows]
            wq_scr[slot_a, idx, (2 * c + 1) * DN_CHUNK:(2 * c + 2) * DN_CHUNK, :] = d["qd"][rows]
            base = c * (DN_DK + DN_CHUNK)
            kq_scr[slot_a, idx, base:base + DN_DK, :] = kd_t
            kq_scr[slot_a, idx, base + DN_DK:base + DN_DK + DN_CHUNK, :] = d["qk"][rows]

    for h in range(DN_HEADS):
        s_scr[h] = st[h]

    @pl.when(gb % groups_per_seq == groups_per_seq - 1)
    def _():
        for h in range(DN_HEADS):
            sout_ref[h] = st[h]


def _gdn_prompt(raw, z, ba, cw, alog_row, dtb_row, onw, consts, sstack, cstack, l, batch, seq, rows):
    gps = seq // GROUP_TOK
    ng = batch * gps
    s_spec, s_sds, s_in, s_in_spec = _stacked_out(
        l, sstack, (batch, DN_HEADS, DN_DK, DN_DV), (None, DN_HEADS, DN_DK, DN_DV),
        lambda i: (jnp.maximum(i - 1, 0) // gps, 0, 0, 0))
    c_spec, c_sds, c_in, c_in_spec = _stacked_out(
        l, cstack, (batch, CONV_W - 1, CONV_DIM), (None, CONV_W - 1, CONV_DIM),
        lambda i: (jnp.minimum(i, ng - 1) // gps, 0, 0))
    n_in = 10
    ltri, lall, lvl = consts
    nht = GROUP_TILES * DN_HEADS
    a_idx = lambda i: (jnp.minimum(i, ng - 1), 0)
    b_idx = lambda i: (jnp.maximum(i - 1, 0), 0)
    return pl.pallas_call(
        functools.partial(_gdn_prompt_body, groups_per_seq=gps),
        grid=(ng + 1,),
        in_specs=[
            pl.BlockSpec((GROUP_TOK, CONV_DIM), a_idx),
            pl.BlockSpec((GROUP_TOK, DN_V), b_idx),
            pl.BlockSpec((GROUP_TOK, LANES), a_idx),
            _layer_spec(l, (CONV_W, CONV_DIM)),
            _layer_spec(l, (1, LANES)),
            _layer_spec(l, (1, LANES)),
            _layer_spec(l, (1, DN_DV)),
            _const_spec((TILE, TILE)),
            _const_spec((TILE, TILE)),
            _const_spec((6, TILE, TILE)),
        ] + s_in_spec + c_in_spec,
        out_specs=[pl.BlockSpec((GROUP_TOK, DN_V), b_idx), s_spec, c_spec],
        out_shape=[jax.ShapeDtypeStruct((rows, DN_V), BF16), s_sds, c_sds],
        input_output_aliases={n_in: 1, n_in + 1: 2} if l else {},
        scratch_shapes=[
            pltpu.VMEM((SUBLANES + GROUP_TOK, CONV_DIM), F32),
            pltpu.VMEM((DN_HEADS, DN_DK, DN_DV), F32),
            pltpu.VMEM((2, nht, TILE, DN_DV), F32),
            pltpu.VMEM((2, nht, 2 * TILE, DN_DK), BF16),
            pltpu.VMEM((2, nht, CHUNKS * (DN_DK + DN_CHUNK), TILE), BF16),
            pltpu.VMEM((2, GROUP_TILES, TILE, LANES), F32),
        ],
        compiler_params=pltpu.CompilerParams(
            dimension_semantics=("arbitrary",), vmem_limit_bytes=VMEM_LIMIT),
        name="gdn_prompt",
    )(raw, z, ba, cw, alog_row, dtb_row, onw, ltri, lall, lvl, *s_in, *c_in)


def _gdn_sample_body(raw_ref, z_ref, ba_ref, cs_ref, st_ref, cw_ref, alog_ref, dtb_ref, onw_ref, *rest):
    od_ref, sout_ref, cout_ref = rest[-3:]
    bb = SAMPLE_BB
    raw = raw_ref[...]
    y = raw * cw_ref[CONV_W - 1:CONV_W, :]
    for i in range(CONV_W - 1):
        y = y + cs_ref[i] * cw_ref[i:i + 1, :]
    y = _silu(y)
    for i in range(CONV_W - 2):
        cout_ref[i] = cs_ref[i + 1]
    cout_ref[CONV_W - 2] = raw

    beta_f, g_f = _gate_rows(ba_ref[...], alog_ref[...], dtb_ref[...])
    eg_f = jnp.exp(g_f)
    pad = jnp.zeros((LANES - bb, DN_DK), F32)
    outs = []
    for h in range(DN_HEADS):
        q = _l2n(y[:, h * DN_DK:(h + 1) * DN_DK]) * (DN_DK ** -0.5)
        k = _l2n(y[:, DN_QK + h * DN_DK:DN_QK + (h + 1) * DN_DK])
        v = y[:, 2 * DN_QK + h * DN_DV:2 * DN_QK + (h + 1) * DN_DV]
        k_t = jnp.concatenate([k, pad], axis=0).T
        qk = jnp.sum(q * k, axis=1, keepdims=True)
        o_rows = []
        for b in range(bb):
            s1 = st_ref[b, h] * eg_f[b:b + 1, DN_HEADS + h:DN_HEADS + h + 1]
            kq = jnp.concatenate([k[b:b + 1, :], q[b:b + 1, :]], axis=0).astype(BF16)
            r = _dot(kq, s1.astype(BF16))
            delta = beta_f[b:b + 1, h:h + 1] * (v[b:b + 1, :] - r[0:1, :])
            sout_ref[b, h] = s1 + k_t[:, b:b + 1] * delta
            o_rows.append(r[1:2, :] + qk[b:b + 1, :] * delta)
        o = jnp.concatenate(o_rows, axis=0)
        zh = z_ref[:, h * DN_DV:(h + 1) * DN_DV]
        outs.append(_rms(o, onw_ref[...]) * _silu(zh))
    od_ref[...] = jnp.concatenate(outs, axis=1).astype(od_ref.dtype)


def _gdn_sample(raw, z, ba, conv_state, dn_state, cw, alog_row, dtb_row, onw, od, sstack, cstack,
                l, row0, dbatch):
    bb = SAMPLE_BB
    r0 = row0 // bb
    s_spec, s_sds, s_in, s_in_spec = _stacked_out(
        l, sstack, (dbatch, DN_HEADS, DN_DK, DN_DV), (bb, DN_HEADS, DN_DK, DN_DV), lambda i: (i, 0, 0, 0))
    c_spec, c_sds, c_in, c_in_spec = _stacked_out(
        l, cstack, (CONV_W - 1, dbatch, CONV_DIM), (CONV_W - 1, bb, CONV_DIM), lambda i: (0, i, 0))
    n_in = 10
    return pl.pallas_call(
        _gdn_sample_body,
        grid=(dbatch // bb,),
        in_specs=[
            pl.BlockSpec((bb, CONV_DIM), lambda i: (r0 + i, 0)),
            pl.BlockSpec((bb, DN_V), lambda i: (r0 + i, 0)),
            pl.BlockSpec((bb, LANES), lambda i: (r0 + i, 0)),
            pl.BlockSpec((None, CONV_W - 1, bb, CONV_DIM), lambda i: (l, 0, i, 0)),
            pl.BlockSpec((None, bb, DN_HEADS, DN_DK, DN_DV), lambda i: (l, i, 0, 0, 0)),
            _layer_spec(l, (CONV_W, CONV_DIM)),
            _layer_spec(l, (1, LANES)),
            _layer_spec(l, (1, LANES)),
            _layer_spec(l, (1, DN_DV)),
            pl.BlockSpec(memory_space=pl.ANY),
        ] + s_in_spec + c_in_spec,
        out_specs=[pl.BlockSpec((bb, DN_V), lambda i: (r0 + i, 0)), s_spec, c_spec],
        out_shape=[jax.ShapeDtypeStruct(od.shape, od.dtype), s_sds, c_sds],
        input_output_aliases={9: 0, n_in: 1, n_in + 1: 2} if l else {9: 0},
        compiler_params=pltpu.CompilerParams(
            dimension_semantics=("parallel",), vmem_limit_bytes=VMEM_LIMIT),
        name="gdn_sample",
    )(raw, z, ba, conv_state, dn_state, cw, alog_row, dtb_row, onw, od, *s_in, *c_in)


def _np_consts():
    i = np.arange(TILE)[:, None]
    j = np.arange(TILE)[None, :]
    same = (i // DN_CHUNK) == (j // DN_CHUNK)
    ltri = (same & (j <= i)).astype(np.float32)
    lall = same.astype(np.float32)
    lvls = []
    b = 1
    while b < DN_CHUNK:
        lvls.append(((i // (2 * b)) == (j // (2 * b))) & ((i // b) != (j // b)))
        b *= 2
    lvl = np.stack(lvls).astype(np.float32)
    hi = np.arange(LANES)
    ones_bd = (hi[:, None] // HEAD_DIM == hi[None, :] // HEAD_DIM).astype(np.float32)
    return ltri, lall, lvl, ones_bd


def _rope_tables(pos):
    half = HEAD_DIM // 2
    inv = 1.0 / (ROPE_THETA ** (jnp.arange(half, dtype=F32) / half))
    ang = pos.astype(F32)[:, None] * inv[None, :]
    cos, sin = jnp.cos(ang), jnp.sin(ang)
    cos_t = jnp.concatenate([cos, cos] * (LANES // HEAD_DIM), axis=1)
    sin_t = jnp.concatenate([-sin, sin] * (LANES // HEAD_DIM), axis=1)
    return cos_t, sin_t


def _pad_row(x, offset):
    out = jnp.zeros((x.shape[0], 1, LANES), F32)
    return out.at[:, 0, offset:offset + x.shape[1]].set(x.astype(F32))


def kernel(x_prompt, x_sample, cache_swa_k, cache_swa_v, state_dn, state_conv, ffn1_norm, ffn1_w_gate_up, ffn1_w_down, mix_norm, w_in, q_norm, k_norm, attn_sinks, conv_w, dn_A_log, dn_dt_bias, dn_out_norm, w_attn_o, w_dn_o, w_out, ffn2_norm, ffn2_w_gate_up, ffn2_w_down):
    batch, seq, _ = x_prompt.shape
    dbatch = x_sample.shape[0]
    prows = batch * seq
    rows = prows + dbatch
    perm = np.asarray(HEAD_PERM)

    w_in_t = jnp.transpose(w_in, (0, 2, 1))
    wao = w_attn_o.reshape(DEPTH, N_HEADS, HEAD_DIM, D_MODEL)[:, perm].reshape(DEPTH, ATTN_Q, D_MODEL).astype(BF16)
    wdo = w_dn_o.astype(BF16)
    wout = w_out.astype(BF16)
    wgu1, wd1, wgu2, wd2 = ffn1_w_gate_up, ffn1_w_down, ffn2_w_gate_up, ffn2_w_down
    n1 = ffn1_norm.reshape(DEPTH, 1, D_MODEL)
    n2 = ffn2_norm.reshape(DEPTH, 1, D_MODEL)
    nm = mix_norm.reshape(DEPTH, 1, D_MODEL)
    qk_nw = jnp.concatenate([jnp.tile(q_norm, (1, N_HEADS)), jnp.tile(k_norm, (1, N_KV_HEADS))], axis=1)
    alog_row = _pad_row(dn_A_log, DN_HEADS)
    dtb_row = _pad_row(dn_dt_bias, DN_HEADS)
    onw = dn_out_norm.reshape(DEPTH, 1, DN_DV)
    sinks_perm = attn_sinks[:, perm]
    sinks_b = jnp.broadcast_to(sinks_perm[:, :, None], (DEPTH, N_HEADS, LANES))

    ltri, lall, lvl, ones_bd = _np_consts()
    gdn_consts = (jnp.asarray(ltri, BF16), jnp.asarray(lall, BF16), jnp.asarray(lvl, BF16))
    ones_bd = jnp.asarray(ones_bd, BF16)
    cos_p, sin_p = _rope_tables(jnp.arange(seq))
    cos_s, sin_s = _rope_tables(PAST_LEN + jnp.arange(1))

    ck = jnp.transpose(cache_swa_k, (0, 1, 3, 4, 2))
    cv = jnp.transpose(cache_swa_v, (0, 1, 3, 4, 2))
    cs = jnp.transpose(state_conv, (0, 2, 1, 3))

    split_rows = _can_split_rows(prows, dbatch)
    xp2, xs2 = x_prompt.reshape(prows, D_MODEL), x_sample.reshape(dbatch, D_MODEL)
    kp = vp = sp = cp = ksn = vsn = ssn = csn = None
    for l in range(DEPTH):
        if l > 0:
            h = _ffn(h, n1, wgu1, wd1, l)
        elif split_rows:
            h = _ffn_first(xp2, xs2, n1, wgu1, wd1, l)
        else:
            h = _ffn(jnp.concatenate([xp2, xs2], axis=0), n1, wgu1, wd1, l)
        qkv, raw, z, ba, gates = _inproj(h, nm, w_in_t, l)
        oa, kp, vp = _attn_prompt(qkv, attn_sinks[l], cos_p, sin_p, qk_nw[l:l + 1], ones_bd,
                                  kp, vp, l, batch, seq, rows)
        oa, ksn, vsn = _attn_sample(qkv, ck, cv, cos_s, sin_s, qk_nw[l:l + 1], ones_bd,
                                    sinks_b[l], oa, ksn, vsn, l, prows, dbatch)
        od, sp, cp = _gdn_prompt(raw, z, ba, conv_w, alog_row, dtb_row, onw, gdn_consts,
                                 sp, cp, l, batch, seq, rows)
        od, ssn, csn = _gdn_sample(raw, z, ba, cs, state_dn, conv_w, alog_row, dtb_row, onw,
                                   od, ssn, csn, l, prows, dbatch)
        final_split = (prows, dbatch) if (split_rows and l == DEPTH - 1) else None
        h = _merge_ffn(h, oa, od, gates, wao, wdo, wout, n2, wgu2, wd2, l, split=final_split)

    yp, ys = h if split_rows else (h[:prows], h[prows:])
    kv_shape = (DEPTH, -1, WINDOW, N_KV_HEADS, HEAD_DIM)
    return (yp.reshape(batch, seq, D_MODEL),
            ys.reshape(dbatch, 1, D_MODEL),
            kp.reshape(kv_shape), vp.reshape(kv_shape), sp, cp,
            jnp.transpose(ksn, (0, 1, 4, 2, 3)), jnp.transpose(vsn, (0, 1, 4, 2, 3)), ssn,
            jnp.transpose(csn, (0, 2, 1, 3)))
```

```python
import functools
import math

import numpy as np
import jax
import jax.numpy as jnp
from jax import lax
from jax.experimental import pallas as pl
from jax.experimental.pallas import tpu as pltpu

F32 = jnp.float32
BF16 = jnp.bfloat16

D_MODEL = 1024
DEPTH = 4
PAST_LEN = 8192
N_HEADS = 8
N_KV_HEADS = 2
GROUP = N_HEADS // N_KV_HEADS
HEAD_DIM = 64
WINDOW = 128
ATTN_BLOCK = 128
ROPE_THETA = 10000.0
DN_HEADS = 4
DN_DK = 128
DN_DV = 128
CONV_W = 4
DN_CHUNK = 128
D_FF = 2816
EPS = 1e-6
LOG2E = math.log2(math.e)

ATTN_Q = N_HEADS * HEAD_DIM
ATTN_KV = N_KV_HEADS * HEAD_DIM
DN_QK = DN_HEADS * DN_DK
DN_V = DN_HEADS * DN_DV
CONV_DIM = 2 * DN_QK + DN_V
QKV_W = ATTN_Q + 2 * ATTN_KV
QK_W = ATTN_Q + ATTN_KV
LANES = 128
SUBLANES = 8
TILE = 128
VMEM_LIMIT = 60 * 1024 * 1024

HEAD_PERM = (0, 4, 1, 5, 2, 6, 3, 7)
MXU_COLS = 256
FF_CHUNKS = (3 * MXU_COLS,) * 3 + (2 * MXU_COLS,)
WIDE_ROW_CAP = 768


def _row_tile(rows, cap=512):
    best = SUBLANES
    for t in range(SUBLANES, cap + 1, SUBLANES):
        if rows % t == 0:
            best = t
    return best


def _rms(x, w):
    return x * lax.rsqrt(jnp.mean(x * x, axis=-1, keepdims=True) + EPS) * w


def _dot(a, b):
    return jnp.dot(a, b, preferred_element_type=F32)


def _dot_nt(a, b):
    return lax.dot_general(a, b, (((1,), (1,)), ((), ())), preferred_element_type=F32)


def _split2(x):
    hi = x.astype(BF16)
    return hi, (x - hi.astype(F32)).astype(BF16)


def _split3(x):
    hi = x.astype(BF16)
    r = x - hi.astype(F32)
    mid = r.astype(BF16)
    lo = (r - mid.astype(F32)).astype(BF16)
    return hi, mid, lo


def _const_spec(shape):
    nd = len(shape)
    return pl.BlockSpec(shape, lambda *_: (0,) * nd, pipeline_mode=pl.Buffered(1))


def _layer_spec(l, shape):
    nd = len(shape)
    return pl.BlockSpec((None,) + shape, lambda *_: (l,) + (0,) * nd, pipeline_mode=pl.Buffered(1))


def _stacked_out(l, prev, shape, block, index_map):
    spec = pl.BlockSpec((None,) + block, lambda *idx: (l,) + tuple(index_map(*idx)))
    sds = jax.ShapeDtypeStruct((DEPTH,) + shape, F32)
    if l == 0:
        return spec, sds, [], []
    return spec, sds, [prev], [pl.BlockSpec(memory_space=pl.ANY)]


def _swiglu_residual(x, nw, wgu_ref, wd_ref):
    xn = _rms(x, nw).astype(BF16)
    acc = jnp.zeros_like(x)
    c0 = 0
    for tf in FF_CHUNKS:
        g = _dot(xn, wgu_ref[:, c0:c0 + tf].astype(BF16))
        u = _dot(xn, wgu_ref[:, D_FF + c0:D_FF + c0 + tf].astype(BF16))
        a = (g * jax.nn.sigmoid(g) * u).astype(BF16)
        acc = acc + _dot(a, wd_ref[c0:c0 + tf, :].astype(BF16))
        c0 += tf
    return x + 0.5 * acc


def _ffn_body(x_ref, nw_ref, wgu_ref, wd_ref, o_ref):
    o_ref[...] = _swiglu_residual(x_ref[...], nw_ref[...], wgu_ref, wd_ref)


def _ffn_first_body(*refs):
    *piece_refs, xs_ref, nw_ref, wgu_ref, wd_ref, o_ref = refs
    last = pl.program_id(0) == pl.num_programs(0) - 1
    pieces = [r[...] for r in piece_refs]
    pieces[-1] = jnp.where(last, xs_ref[...], pieces[-1])
    o_ref[...] = _swiglu_residual(jnp.concatenate(pieces, axis=0), nw_ref[...], wgu_ref, wd_ref)


def _ffn_first(xp, xs, norm_w, wgu, wd, l):
    prows, dbatch = xp.shape[0], xs.shape[0]
    rows = prows + dbatch
    tm = _row_tile(rows)
    n_piece = tm // dbatch
    last_piece = prows // dbatch - 1
    piece_spec = lambda p: pl.BlockSpec(
        (dbatch, D_MODEL), lambda i: (jnp.minimum(i * n_piece + p, last_piece), 0))
    return pl.pallas_call(
        _ffn_first_body,
        grid=(rows // tm,),
        in_specs=[piece_spec(p) for p in range(n_piece)] + [
            _const_spec((dbatch, D_MODEL)),
            _layer_spec(l, (1, D_MODEL)),
            _layer_spec(l, (D_MODEL, 2 * D_FF)),
            _layer_spec(l, (D_FF, D_MODEL)),
        ],
        out_specs=pl.BlockSpec((tm, D_MODEL), lambda i: (i, 0)),
        out_shape=jax.ShapeDtypeStruct((rows, D_MODEL), F32),
        compiler_params=pltpu.CompilerParams(
            dimension_semantics=("parallel",), vmem_limit_bytes=VMEM_LIMIT),
        name="ffn_first",
    )(*([xp] * n_piece), xs, norm_w, wgu, wd)


def _can_split_rows(prows, dbatch):
    tm = _row_tile(prows + dbatch)
    return tm % dbatch == 0 and prows % dbatch == 0 and dbatch % SUBLANES == 0


def _ffn(x, norm_w, wgu, wd, l):
    rows = x.shape[0]
    tm = _row_tile(rows, cap=WIDE_ROW_CAP)
    return pl.pallas_call(
        _ffn_body,
        grid=(rows // tm,),
        in_specs=[
            pl.BlockSpec((tm, D_MODEL), lambda i: (i, 0)),
            _layer_spec(l, (1, D_MODEL)),
            _layer_spec(l, (D_MODEL, 2 * D_FF)),
            _layer_spec(l, (D_FF, D_MODEL)),
        ],
        out_specs=pl.BlockSpec((tm, D_MODEL), lambda i: (i, 0)),
        out_shape=jax.ShapeDtypeStruct((rows, D_MODEL), F32),
        compiler_params=pltpu.CompilerParams(
            dimension_semantics=("parallel",), vmem_limit_bytes=VMEM_LIMIT),
        name="ffn",
    )(x, norm_w, wgu, wd)


IN_SEGS = (QKV_W, CONV_DIM, DN_V, LANES, 2 * D_MODEL)
IN_DTYPES = (F32, F32, F32, F32, BF16)
IN_COLS = QKV_W + CONV_DIM + DN_V + 2 * DN_HEADS + 2 * D_MODEL
IN_ALIGNED = QKV_W + CONV_DIM + DN_V


def _inproj_body(h_ref, nw_ref, wt_ref, qkv_ref, raw_ref, z_ref, ba_ref, gates_ref):
    u = _rms(h_ref[...], nw_ref[...]).astype(BF16)

    def proj(r0, r1):
        return _dot_nt(u, wt_ref[r0:r1, :].astype(BF16))

    qkv = proj(0, QKV_W)
    lo = lax.broadcasted_iota(jnp.int32, (qkv.shape[0], LANES), 1) < HEAD_DIM
    nat = [qkv[:, c * LANES:(c + 1) * LANES] for c in range(ATTN_Q // LANES)]
    for g in range(GROUP):
        first, second = nat[g // 2], nat[(GROUP + g) // 2]
        if g % 2 == 0:
            second = pltpu.roll(second, HEAD_DIM, 1)
        else:
            first = pltpu.roll(first, HEAD_DIM, 1)
        qkv_ref[:, g * LANES:(g + 1) * LANES] = jnp.where(lo, first, second)
    qkv_ref[:, ATTN_Q:] = qkv[:, ATTN_Q:]
    raw_ref[...] = proj(QKV_W, QKV_W + CONV_DIM)
    z_ref[...] = proj(QKV_W + CONV_DIM, IN_ALIGNED)
    ba_ref[...] = proj(IN_ALIGNED, IN_ALIGNED + LANES)
    gates_ref[...] = proj(IN_ALIGNED + 2 * DN_HEADS, IN_COLS).astype(gates_ref.dtype)


def _inproj(h, norm_w, w_in_t, l):
    rows = h.shape[0]
    tm = _row_tile(rows, cap=WIDE_ROW_CAP)
    return pl.pallas_call(
        _inproj_body,
        grid=(rows // tm,),
        in_specs=[
            pl.BlockSpec((tm, D_MODEL), lambda i: (i, 0)),
            _layer_spec(l, (1, D_MODEL)),
            _layer_spec(l, (IN_COLS, D_MODEL)),
        ],
        out_specs=[pl.BlockSpec((tm, w), lambda i: (i, 0)) for w in IN_SEGS],
        out_shape=[jax.ShapeDtypeStruct((rows, w), dt) for w, dt in zip(IN_SEGS, IN_DTYPES)],
        compiler_params=pltpu.CompilerParams(
            dimension_semantics=("parallel",), vmem_limit_bytes=VMEM_LIMIT),
        name="inproj",
    )(h, norm_w, w_in_t)


def _merge_ffn_body(h_ref, oa_ref, od_ref, gates_ref, wao_ref, wdo_ref, wout_ref,
                    nw_ref, wgu_ref, wd_ref, o_ref, *sample_out):
    br_a = _dot(oa_ref[...], wao_ref[...])
    br_d = _dot(od_ref[...], wdo_ref[...])
    ga = gates_ref[:, :D_MODEL].astype(F32)
    gd = gates_ref[:, D_MODEL:].astype(F32)
    m = jax.nn.sigmoid(ga) * br_a + jax.nn.sigmoid(gd) * br_d
    h = h_ref[...] + _dot(m.astype(BF16), wout_ref[...])
    y = _swiglu_residual(h, nw_ref[...], wgu_ref, wd_ref)
    o_ref[...] = y
    if sample_out:
        (os_ref,) = sample_out

        @pl.when(pl.program_id(0) == pl.num_programs(0) - 1)
        def _():
            os_ref[...] = y[y.shape[0] - os_ref.shape[0]:, :]


def _merge_ffn(h, oa, od, gates, wao, wdo, wout, norm_w, wgu, wd, l, split=None):
    rows = h.shape[0]
    tm = _row_tile(rows)
    out_specs = pl.BlockSpec((tm, D_MODEL), lambda i: (i, 0))
    out_shape = jax.ShapeDtypeStruct((rows, D_MODEL), F32)
    if split is not None:
        prows, dbatch = split
        out_specs = [out_specs, pl.BlockSpec((dbatch, D_MODEL), lambda i: (0, 0))]
        out_shape = [jax.ShapeDtypeStruct((prows, D_MODEL), F32), jax.ShapeDtypeStruct((dbatch, D_MODEL), F32)]
    return pl.pallas_call(
        _merge_ffn_body,
        grid=(rows // tm,),
        in_specs=[
            pl.BlockSpec((tm, D_MODEL), lambda i: (i, 0)),
            pl.BlockSpec((tm, ATTN_Q), lambda i: (i, 0)),
            pl.BlockSpec((tm, DN_V), lambda i: (i, 0)),
            pl.BlockSpec((tm, 2 * D_MODEL), lambda i: (i, 0)),
            _layer_spec(l, (ATTN_Q, D_MODEL)),
            _layer_spec(l, (DN_V, D_MODEL)),
            _layer_spec(l, (D_MODEL, D_MODEL)),
            _layer_spec(l, (1, D_MODEL)),
            _layer_spec(l, (D_MODEL, 2 * D_FF)),
            _layer_spec(l, (D_FF, D_MODEL)),
        ],
        out_specs=out_specs,
        out_shape=out_shape,
        compiler_params=pltpu.CompilerParams(
            dimension_semantics=("arbitrary",), vmem_limit_bytes=VMEM_LIMIT),
        name="merge_ffn",
    )(h, oa, od, gates, wao, wdo, wout, norm_w, wgu, wd)


def _norm_rope(x, ones_bd, nw, cos, sin):
    lane = lax.broadcasted_iota(jnp.int32, (x.shape[0], LANES), 1)
    first_half = (lane & (HEAD_DIM // 2)) == 0
    outs = []
    for g in range(x.shape[1] // LANES):
        xg = x[:, g * LANES:(g + 1) * LANES]
        hi, lo = _split2(xg * xg)
        ssq = _dot(hi, ones_bd) + _dot(lo, ones_bd)
        xn = xg * lax.rsqrt(ssq * (1.0 / HEAD_DIM) + EPS) * nw[:, g * LANES:(g + 1) * LANES]
        partner = jnp.where(first_half,
                            pltpu.roll(xn, LANES - HEAD_DIM // 2, 1),
                            pltpu.roll(xn, HEAD_DIM // 2, 1))
        outs.append(xn * cos + partner * sin)
    return jnp.concatenate(outs, axis=1)


ATTN_STEP_BLOCKS = 2


def _attn_prompt_body(sinks_ref, x_ref, cos_ref, sin_ref, nw_ref, ones_ref, *rest):
    o_ref, kout_ref, vout_ref, kprev, vprev = rest[-5:]
    step = pl.program_id(1)

    @pl.when(step == 0)
    def _():
        kprev[...] = jnp.zeros_like(kprev)
        vprev[...] = jnp.zeros_like(vprev)

    x = x_ref[...]
    qk = _norm_rope(x[:, :QK_W], ones_ref[...], nw_ref[...], cos_ref[...], sin_ref[...])
    k_all = qk[:, ATTN_Q:QK_W]
    v_all = x[:, QK_W:QKV_W]
    k_bf = jnp.concatenate([kprev[...], k_all.astype(BF16)], axis=0)
    v_bf = jnp.concatenate([vprev[...], v_all.astype(BF16)], axis=0)

    r = lax.broadcasted_iota(jnp.int32, (ATTN_BLOCK, 2 * ATTN_BLOCK), 0)
    c = lax.broadcasted_iota(jnp.int32, (ATTN_BLOCK, 2 * ATTN_BLOCK), 1)
    band = (c >= r) & (c <= r + WINDOW)
    first_col = jnp.where(step == 0, ATTN_BLOCK, 0)
    lo = lax.broadcasted_iota(jnp.int32, (ATTN_BLOCK, LANES), 1) < HEAD_DIM
    scale = LOG2E / math.sqrt(HEAD_DIM)
    klane = lax.broadcasted_iota(jnp.int32, k_bf.shape, 1) < HEAD_DIM
    k_half = (jnp.where(klane, k_bf, jnp.zeros_like(k_bf)), jnp.where(klane, jnp.zeros_like(k_bf), k_bf))

    units = [(sb, g, half) for sb in range(ATTN_STEP_BLOCKS) for g in range(GROUP) for half in range(2)]
    q_bf = {(sb, g): (qk[sb * ATTN_BLOCK:(sb + 1) * ATTN_BLOCK, g * LANES:(g + 1) * LANES] * scale).astype(BF16)
            for sb in range(ATTN_STEP_BLOCKS) for g in range(GROUP)}
    s_raw = {}
    for (sb, g, half) in units:
        s_raw[sb, g, half] = _dot_nt(q_bf[sb, g], k_half[half][sb * ATTN_BLOCK:(sb + 2) * ATTN_BLOCK])
    p_den = {}
    for (sb, g, half) in units:
        vis = band & (c >= first_col) if sb == 0 else band
        s = jnp.where(vis, s_raw.pop((sb, g, half)), -jnp.inf)
        sink = sinks_ref[HEAD_PERM[2 * g + half]] * LOG2E
        m = jnp.maximum(jnp.max(s, axis=1, keepdims=True), sink)
        p = jnp.exp2(s - m)
        p_den[sb, g, half] = (p.astype(BF16), jnp.sum(p, axis=1, keepdims=True) + jnp.exp2(sink - m))
    for sb in range(ATTN_STEP_BLOCKS):
        outs = []
        for g in range(GROUP):
            res = []
            for half in range(2):
                p, denom = p_den.pop((sb, g, half))
                res.append(_dot(p, v_bf[sb * ATTN_BLOCK:(sb + 2) * ATTN_BLOCK]) / denom)
            outs.append(jnp.where(lo, res[0], res[1]))
        o_ref[sb * ATTN_BLOCK:(sb + 1) * ATTN_BLOCK, :] = jnp.concatenate(outs, axis=1).astype(o_ref.dtype)

    last = slice((ATTN_STEP_BLOCKS - 1) * ATTN_BLOCK, ATTN_STEP_BLOCKS * ATTN_BLOCK)
    kprev[...] = k_all[last].astype(BF16)
    vprev[...] = v_all[last].astype(BF16)

    @pl.when(step == pl.num_programs(1) - 1)
    def _():
        kout_ref[...] = k_all[last]
        vout_ref[...] = v_all[last]


def _attn_prompt(qkv, sinks, cos, sin, nw, ones_bd, kstack, vstack, l, batch, seq, rows):
    rows_step = ATTN_STEP_BLOCKS * ATTN_BLOCK
    nb = seq // rows_step
    kv_block = (None, WINDOW, ATTN_KV)
    kv_idx = lambda b, j: (b, 0, 0)
    k_spec, k_sds, k_in, k_in_spec = _stacked_out(l, kstack, (batch, WINDOW, ATTN_KV), kv_block, kv_idx)
    v_spec, v_sds, v_in, v_in_spec = _stacked_out(l, vstack, (batch, WINDOW, ATTN_KV), kv_block, kv_idx)
    n_in = 6
    return pl.pallas_call(
        _attn_prompt_body,
        grid=(batch, nb),
        in_specs=[
            pl.BlockSpec(memory_space=pltpu.SMEM),
            pl.BlockSpec((rows_step, QKV_W), lambda b, j: (b * nb + j, 0)),
            pl.BlockSpec((rows_step, LANES), lambda b, j: (j, 0)),
            pl.BlockSpec((rows_step, LANES), lambda b, j: (j, 0)),
            _const_spec((1, QK_W)),
            _const_spec((LANES, LANES)),
        ] + k_in_spec + v_in_spec,
        out_specs=[pl.BlockSpec((rows_step, ATTN_Q), lambda b, j: (b * nb + j, 0)), k_spec, v_spec],
        out_shape=[jax.ShapeDtypeStruct((rows, ATTN_Q), BF16), k_sds, v_sds],
        input_output_aliases={n_in: 1, n_in + 1: 2} if l else {},
        scratch_shapes=[pltpu.VMEM((ATTN_BLOCK, ATTN_KV), BF16),
                        pltpu.VMEM((ATTN_BLOCK, ATTN_KV), BF16)],
        compiler_params=pltpu.CompilerParams(
            dimension_semantics=("parallel", "arbitrary"), vmem_limit_bytes=VMEM_LIMIT),
        name="attn_prompt",
    )(sinks, qkv, cos, sin, nw, ones_bd, *k_in, *v_in)


SAMPLE_BB = 16


def _attn_sample_body(x_ref, kc_ref, vc_ref, cos_ref, sin_ref, nw_ref, ones_ref, sinks_ref, *rest):
    o_ref, kout_ref, vout_ref = rest[-3:]
    x = x_ref[...]
    qk = _norm_rope(x[:, :QK_W], ones_ref[...], nw_ref[...], cos_ref[...], sin_ref[...])
    knew = qk[:, ATTN_Q:QK_W]
    vnew = x[:, QK_W:QKV_W]
    scale = 1.0 / math.sqrt(HEAD_DIM)
    row = lax.broadcasted_iota(jnp.int32, (N_HEADS, LANES), 0)
    lane = lax.broadcasted_iota(jnp.int32, (N_HEADS, LANES), 1)
    own = (lane < HEAD_DIM) == ((row & 1) == 0)
    lo1 = lax.broadcasted_iota(jnp.int32, (1, LANES), 1) < HEAD_DIM
    sinks = sinks_ref[...]
    units = range(SAMPLE_BB)
    qz, s_all, soft = {}, {}, {}
    for b in units:
        qrows = [qk[b:b + 1, (i // 2) * LANES:(i // 2 + 1) * LANES] for i in range(N_HEADS)]
        qz[b] = jnp.where(own, jnp.concatenate(qrows, axis=0) * scale, 0.0)
        s_all[b] = _dot(qz[b].astype(BF16), kc_ref[b].reshape(ATTN_KV, WINDOW).astype(BF16))
    sink = sinks[:, 0:1]
    for b in units:
        s = s_all.pop(b)
        s_new = jnp.sum(qz.pop(b) * knew[b:b + 1, :], axis=1, keepdims=True)
        m = jnp.maximum(jnp.maximum(jnp.max(s, axis=1, keepdims=True), s_new), sink)
        p = jnp.exp(s - m)
        p_new = jnp.exp(s_new - m)
        soft[b] = (p.astype(BF16), p_new, jnp.sum(p, axis=1, keepdims=True) + p_new + jnp.exp(sink - m))
    o_rows = []
    for b in units:
        p, p_new, denom = soft.pop(b)
        pv = (_dot_nt(p, vc_ref[b].reshape(ATTN_KV, WINDOW).astype(BF16)) + p_new * vnew[b:b + 1, :]) / denom
        o_rows.append(jnp.concatenate(
            [jnp.where(lo1, pv[2 * g:2 * g + 1, :], pv[2 * g + 1:2 * g + 2, :]) for g in range(GROUP)],
            axis=1))
    pad = jnp.zeros((LANES - SAMPLE_BB, ATTN_KV), F32)
    knew_t = jnp.concatenate([knew, pad], axis=0).T
    vnew_t = jnp.concatenate([vnew, pad], axis=0).T
    newest = lax.broadcasted_iota(jnp.int32, (ATTN_KV, WINDOW), 1) == WINDOW - 1
    kv_block = (N_KV_HEADS, HEAD_DIM, WINDOW)
    for b in units:
        for src_ref, new_t, dst_ref in ((kc_ref, knew_t, kout_ref), (vc_ref, vnew_t, vout_ref)):
            shifted = pltpu.roll(src_ref[b].reshape(ATTN_KV, WINDOW), WINDOW - 1, 1)
            dst_ref[b] = jnp.where(newest, new_t[:, b:b + 1], shifted).reshape(kv_block)
    o_ref[...] = jnp.concatenate(o_rows, axis=0).astype(o_ref.dtype)


def _attn_sample(qkv, cache_k, cache_v, cos, sin, nw, ones_bd, sinks_b, oa, kstack, vstack,
                 l, row0, dbatch):
    bb = SAMPLE_BB
    r0 = row0 // bb
    kv_block = (bb, N_KV_HEADS, HEAD_DIM, WINDOW)
    kv_idx = lambda i: (i, 0, 0, 0)
    kv_shape = (dbatch, N_KV_HEADS, HEAD_DIM, WINDOW)
    k_spec, k_sds, k_in, k_in_spec = _stacked_out(l, kstack, kv_shape, kv_block, kv_idx)
    v_spec, v_sds, v_in, v_in_spec = _stacked_out(l, vstack, kv_shape, kv_block, kv_idx)
    n_in = 9
    return pl.pallas_call(
        _attn_sample_body,
        grid=(dbatch // bb,),
        in_specs=[
            pl.BlockSpec((bb, QKV_W), lambda i: (r0 + i, 0)),
            pl.BlockSpec((None,) + kv_block, lambda i: (l, i, 0, 0, 0)),
            pl.BlockSpec((None,) + kv_block, lambda i: (l, i, 0, 0, 0)),
            _const_spec((1, LANES)),
            _const_spec((1, LANES)),
            _const_spec((1, QK_W)),
            _const_spec((LANES, LANES)),
            _const_spec((N_HEADS, LANES)),
            pl.BlockSpec(memory_space=pl.ANY),
        ] + k_in_spec + v_in_spec,
        out_specs=[pl.BlockSpec((bb, ATTN_Q), lambda i: (r0 + i, 0)), k_spec, v_spec],
        out_shape=[jax.ShapeDtypeStruct(oa.shape, oa.dtype), k_sds, v_sds],
        input_output_aliases={8: 0, n_in: 1, n_in + 1: 2} if l else {8: 0},
        compiler_params=pltpu.CompilerParams(
            dimension_semantics=("parallel",), vmem_limit_bytes=VMEM_LIMIT),
        name="attn_sample",
    )(qkv, cache_k, cache_v, cos, sin, nw, ones_bd, sinks_b, oa, *k_in, *v_in)


def _softplus(x):
    return jnp.maximum(x, 0.0) + jnp.log(1.0 + jnp.exp(-jnp.abs(x)))


def _silu(x):
    return x * jax.nn.sigmoid(x)


def _l2n(x):
    return x * lax.rsqrt(jnp.sum(x * x, axis=-1, keepdims=True) + EPS)


def _l2n_mxu(x, ones):
    hi, lo = _split2(x * x)
    return x * lax.rsqrt(_dot(hi, ones) + _dot(lo, ones) + EPS)


def _gate_rows(ba, alog_row, dtb_row):
    beta = jax.nn.sigmoid(ba)
    g = -jnp.exp(alog_row) * _softplus(ba + dtb_row)
    return beta, g


GROUP_TILES = 4
GROUP_TOK = GROUP_TILES * TILE
CHUNKS = TILE // DN_CHUNK
INV_LEVELS = DN_CHUNK.bit_length() - 1


def _gdn_prompt_body(raw_ref, z_ref, ba_ref, cw_ref, alog_ref, dtb_ref, onw_ref,
                     ltri_ref, lall_ref, lvl_ref, *rest, groups_per_seq):
    od_ref, sout_ref, cout_ref, xp, s_scr, ub_scr, wq_scr, kq_scr, egl_scr = rest[-9:]
    i = pl.program_id(0)
    n_groups = pl.num_programs(0) - 1
    ga = jnp.minimum(i, n_groups - 1)
    gb = jnp.maximum(i - 1, 0)
    slot_a = i % 2
    slot_b = 1 - slot_a

    @pl.when(i == 0)
    def _():
        ub_scr[1] = jnp.zeros(ub_scr.shape[1:], F32)
        wq_scr[1] = jnp.zeros(wq_scr.shape[1:], BF16)
        kq_scr[1] = jnp.zeros(kq_scr.shape[1:], BF16)
        egl_scr[1] = jnp.zeros(egl_scr.shape[1:], F32)

    @pl.when(ga % groups_per_seq == 0)
    def _():
        xp[0:SUBLANES, :] = jnp.zeros((SUBLANES, CONV_DIM), F32)

    @pl.when(gb % groups_per_seq == 0)
    def _():
        s_scr[...] = jnp.zeros_like(s_scr)

    @pl.when(ga % groups_per_seq == groups_per_seq - 1)
    def _():
        cout_ref[...] = raw_ref[GROUP_TOK - (CONV_W - 1):GROUP_TOK, :]

    ltri = ltri_ref[...]
    lall = lall_ref[...]
    ri = lax.broadcasted_iota(jnp.int32, (TILE, TILE), 0)
    ci = lax.broadcasted_iota(jnp.int32, (TILE, TILE), 1)
    same = (ri // DN_CHUNK) == (ci // DN_CHUNK)
    causal = same & (ci <= ri)
    strict = same & (ci < ri)
    eye_bf = (ri == ci).astype(F32).astype(BF16)
    zeros_half = jnp.zeros((DN_CHUNK, DN_DV), BF16)
    onw = onw_ref[...]
    ones_sq = jnp.ones((DN_DK, LANES), BF16)
    units = [(j, h) for j in range(GROUP_TILES) for h in range(DN_HEADS)]

    st = [s_scr[h] for h in range(DN_HEADS)]
    b_live = {}

    def b_stage1(j, c):
        for h in range(DN_HEADS):
            idx = j * DN_HEADS + h
            r1 = _dot(wq_scr[slot_b, idx, 2 * c * DN_CHUNK:(2 * c + 2) * DN_CHUNK, :], st[h].astype(BF16))
            u = ub_scr[slot_b, idx, c * DN_CHUNK:(c + 1) * DN_CHUNK, :] - r1[:DN_CHUNK]
            parts = [zeros_half] * CHUNKS
            parts[c] = u.astype(BF16)
            b_live[h] = (jnp.concatenate(parts, axis=0), r1[DN_CHUNK:])

    def b_stage2(j, c):
        egl_f = egl_scr[slot_b, j]
        for h in range(DN_HEADS):
            idx = j * DN_HEADS + h
            u_full, oq = b_live[h]
            base = c * (DN_DK + DN_CHUNK)
            r2 = _dot(kq_scr[slot_b, idx, base:base + DN_DK + DN_CHUNK, :], u_full)
            o = oq + r2[DN_DK:]
            zh = z_ref[j * TILE + c * DN_CHUNK:j * TILE + (c + 1) * DN_CHUNK, h * DN_DV:(h + 1) * DN_DV]
            od_ref[j * TILE + c * DN_CHUNK:j * TILE + (c + 1) * DN_CHUNK, h * DN_DV:(h + 1) * DN_DV] = (
                _rms(o, onw) * _silu(zh)).astype(od_ref.dtype)
            st[h] = (st[h] * egl_f[c * DN_CHUNK:c * DN_CHUNK + 1, DN_HEADS + h:DN_HEADS + h + 1]
                     + r2[:DN_DK])

    b_stages = []
    for j in range(GROUP_TILES):
        for c in range(CHUNKS):
            b_stages.append(functools.partial(b_stage1, j, c))
            b_stages.append(functools.partial(b_stage2, j, c))
    b_iter = iter(b_stages)

    def b_step():
        f = next(b_iter, None)
        if f is not None:
            f()

    xp[SUBLANES:SUBLANES + GROUP_TOK, :] = raw_ref[...]
    ys = []
    for j in range(GROUP_TILES):
        r0 = SUBLANES + j * TILE
        y = xp[r0:r0 + TILE, :] * cw_ref[CONV_W - 1:CONV_W, :]
        for s in range(1, CONV_W):
            y = y + xp[r0 - s:r0 - s + TILE, :] * cw_ref[CONV_W - 1 - s:CONV_W - s, :]
        ys.append(_silu(y))
        b_step()
    xp[0:SUBLANES, :] = raw_ref[GROUP_TOK - SUBLANES:GROUP_TOK, :]

    tiles = []
    for j in range(GROUP_TILES):
        beta_f, g_f = _gate_rows(ba_ref[j * TILE:(j + 1) * TILE, :], alog_ref[...], dtb_ref[...])
        g_parts = _split3(g_f)
        gcum = sum(_dot(ltri, part) for part in g_parts)
        glast = sum(_dot(lall, part) for part in g_parts)
        egl_scr[slot_a, j] = jnp.exp(glast)
        tiles.append((beta_f, gcum, glast, gcum.T))

    u = {}
    for (j, h) in units:
        beta_f, gcum, glast, gcum_t = tiles[j]
        y = ys[j]
        k = _l2n_mxu(y[:, DN_QK + h * DN_DK:DN_QK + (h + 1) * DN_DK], ones_sq)
        gc = gcum[:, DN_HEADS + h:DN_HEADS + h + 1]
        gr = gcum_t[DN_HEADS + h:DN_HEADS + h + 1, :]
        u[j, h] = dict(k=k, kb=k.astype(BF16), beta=beta_f[:, h:h + 1], gc=gc,
                       gl=glast[:, DN_HEADS + h:DN_HEADS + h + 1],
                       decay=jnp.exp(jnp.where(causal, gc - gr, -jnp.inf)))
    for un in units:
        d = u[un]
        d["kk"] = _dot_nt(d["kb"], d["kb"])
    b_step()
    for un in units:
        d = u[un]
        a = jnp.where(strict, d["beta"] * d.pop("kk") * d["decay"], 0.0)
        d["ab"] = a.astype(BF16)
        d["tb"] = eye_bf - d["ab"] * lvl_ref[0]
    for lv in range(1, INV_LEVELS):
        for un in units:
            d = u[un]
            d["p"] = _dot(d["tb"], d["ab"] * lvl_ref[lv]).astype(BF16)
        b_step()
        for un in units:
            d = u[un]
            d["tb"] = d["tb"] - _dot(d.pop("p"), d["tb"]).astype(BF16)
        b_step()
    for (j, h) in units:
        d = u[j, h]
        y = ys[j]
        v = y[:, 2 * DN_QK + h * DN_DV:2 * DN_QK + (h + 1) * DN_DV]
        d["eg"] = jnp.exp(d["gc"])
        rhs = jnp.concatenate([v * d["beta"], d["k"] * (d["beta"] * d["eg"])], axis=1).astype(BF16)
        d["sol"] = _dot(d.pop("tb"), rhs)
    b_step()
    for (j, h) in units:
        d = u[j, h]
        y = ys[j]
        q = _l2n_mxu(y[:, h * DN_DK:(h + 1) * DN_DK], ones_sq) * (DN_DK ** -0.5)
        d["qk"] = (_dot_nt(q.astype(BF16), d["kb"]) * d["decay"]).astype(BF16)
        d["qd"] = (q * d["eg"]).astype(BF16)
    b_step()
    for _ in range(len(b_stages)):
        b_step()
    for (j, h) in units:
        d = u[j, h]
        idx = j * DN_HEADS + h
        sol = d["sol"]
        ub_scr[slot_a, idx] = sol[:, :DN_DV]
        w = sol[:, DN_DV:].astype(BF16)
        kd_t = (d["k"] * jnp.exp(d["gl"] - d["gc"])).T.astype(BF16)
        for c in range(CHUNKS):
            rows = slice(c * DN_CHUNK, (c + 1) * DN_CHUNK)
            wq_scr[slot_a, idx, 2 * c * DN_CHUNK:(2 * c + 1) * DN_CHUNK, :] = w[rows]
            wq_scr[slot_a, idx, (2 * c + 1) * DN_CHUNK:(2 * c + 2) * DN_CHUNK, :] = d["qd"][rows]
            base = c * (DN_DK + DN_CHUNK)
            kq_scr[slot_a, idx, base:base + DN_DK, :] = kd_t
            kq_scr[slot_a, idx, base + DN_DK:base + DN_DK + DN_CHUNK, :] = d["qk"][rows]

    for h in range(DN_HEADS):
        s_scr[h] = st[h]

    @pl.when(gb % groups_per_seq == groups_per_seq - 1)
    def _():
        for h in range(DN_HEADS):
            sout_ref[h] = st[h]


def _gdn_prompt(raw, z, ba, cw, alog_row, dtb_row, onw, consts, sstack, cstack, l, batch, seq, rows):
    gps = seq // GROUP_TOK
    ng = batch * gps
    s_spec, s_sds, s_in, s_in_spec = _stacked_out(
        l, sstack, (batch, DN_HEADS, DN_DK, DN_DV), (None, DN_HEADS, DN_DK, DN_DV),
        lambda i: (jnp.maximum(i - 1, 0) // gps, 0, 0, 0))
    c_spec, c_sds, c_in, c_in_spec = _stacked_out(
        l, cstack, (batch, CONV_W - 1, CONV_DIM), (None, CONV_W - 1, CONV_DIM),
        lambda i: (jnp.minimum(i, ng - 1) // gps, 0, 0))
    n_in = 10
    ltri, lall, lvl = consts
    nht = GROUP_TILES * DN_HEADS
    a_idx = lambda i: (jnp.minimum(i, ng - 1), 0)
    b_idx = lambda i: (jnp.maximum(i - 1, 0), 0)
    return pl.pallas_call(
        functools.partial(_gdn_prompt_body, groups_per_seq=gps),
        grid=(ng + 1,),
        in_specs=[
            pl.BlockSpec((GROUP_TOK, CONV_DIM), a_idx),
            pl.BlockSpec((GROUP_TOK, DN_V), b_idx),
            pl.BlockSpec((GROUP_TOK, LANES), a_idx),
            _layer_spec(l, (CONV_W, CONV_DIM)),
            _layer_spec(l, (1, LANES)),
            _layer_spec(l, (1, LANES)),
            _layer_spec(l, (1, DN_DV)),
            _const_spec((TILE, TILE)),
            _const_spec((TILE, TILE)),
            _const_spec((INV_LEVELS, TILE, TILE)),
        ] + s_in_spec + c_in_spec,
        out_specs=[pl.BlockSpec((GROUP_TOK, DN_V), b_idx), s_spec, c_spec],
        out_shape=[jax.ShapeDtypeStruct((rows, DN_V), BF16), s_sds, c_sds],
        input_output_aliases={n_in: 1, n_in + 1: 2} if l else {},
        scratch_shapes=[
            pltpu.VMEM((SUBLANES + GROUP_TOK, CONV_DIM), F32),
            pltpu.VMEM((DN_HEADS, DN_DK, DN_DV), F32),
            pltpu.VMEM((2, nht, TILE, DN_DV), F32),
            pltpu.VMEM((2, nht, 2 * TILE, DN_DK), BF16),
            pltpu.VMEM((2, nht, CHUNKS * (DN_DK + DN_CHUNK), TILE), BF16),
            pltpu.VMEM((2, GROUP_TILES, TILE, LANES), F32),
        ],
        compiler_params=pltpu.CompilerParams(
            dimension_semantics=("arbitrary",), vmem_limit_bytes=VMEM_LIMIT),
        name="gdn_prompt",
    )(raw, z, ba, cw, alog_row, dtb_row, onw, ltri, lall, lvl, *s_in, *c_in)


def _gdn_sample_body(raw_ref, z_ref, ba_ref, cs_ref, st_ref, cw_ref, alog_ref, dtb_ref, onw_ref, *rest):
    od_ref, sout_ref, cout_ref = rest[-3:]
    bb = SAMPLE_BB
    raw = raw_ref[...]
    y = raw * cw_ref[CONV_W - 1:CONV_W, :]
    for i in range(CONV_W - 1):
        y = y + cs_ref[i] * cw_ref[i:i + 1, :]
    y = _silu(y)
    for i in range(CONV_W - 2):
        cout_ref[i] = cs_ref[i + 1]
    cout_ref[CONV_W - 2] = raw

    beta_f, g_f = _gate_rows(ba_ref[...], alog_ref[...], dtb_ref[...])
    eg_f = jnp.exp(g_f)
    pad = jnp.zeros((LANES - bb, DN_DK), F32)
    outs = []
    for h in range(DN_HEADS):
        q = _l2n(y[:, h * DN_DK:(h + 1) * DN_DK]) * (DN_DK ** -0.5)
        k = _l2n(y[:, DN_QK + h * DN_DK:DN_QK + (h + 1) * DN_DK])
        v = y[:, 2 * DN_QK + h * DN_DV:2 * DN_QK + (h + 1) * DN_DV]
        k_t = jnp.concatenate([k, pad], axis=0).T
        qk = jnp.sum(q * k, axis=1, keepdims=True)
        o_rows = []
        for b in range(bb):
            s1 = st_ref[b, h] * eg_f[b:b + 1, DN_HEADS + h:DN_HEADS + h + 1]
            kq = jnp.concatenate([k[b:b + 1, :], q[b:b + 1, :]], axis=0).astype(BF16)
            r = _dot(kq, s1.astype(BF16))
            delta = beta_f[b:b + 1, h:h + 1] * (v[b:b + 1, :] - r[0:1, :])
            sout_ref[b, h] = s1 + k_t[:, b:b + 1] * delta
            o_rows.append(r[1:2, :] + qk[b:b + 1, :] * delta)
        o = jnp.concatenate(o_rows, axis=0)
        zh = z_ref[:, h * DN_DV:(h + 1) * DN_DV]
        outs.append(_rms(o, onw_ref[...]) * _silu(zh))
    od_ref[...] = jnp.concatenate(outs, axis=1).astype(od_ref.dtype)


def _gdn_sample(raw, z, ba, conv_state, dn_state, cw, alog_row, dtb_row, onw, od, sstack, cstack,
                l, row0, dbatch):
    bb = SAMPLE_BB
    r0 = row0 // bb
    s_spec, s_sds, s_in, s_in_spec = _stacked_out(
        l, sstack, (dbatch, DN_HEADS, DN_DK, DN_DV), (bb, DN_HEADS, DN_DK, DN_DV), lambda i: (i, 0, 0, 0))
    c_spec, c_sds, c_in, c_in_spec = _stacked_out(
        l, cstack, (CONV_W - 1, dbatch, CONV_DIM), (CONV_W - 1, bb, CONV_DIM), lambda i: (0, i, 0))
    n_in = 10
    return pl.pallas_call(
        _gdn_sample_body,
        grid=(dbatch // bb,),
        in_specs=[
            pl.BlockSpec((bb, CONV_DIM), lambda i: (r0 + i, 0)),
            pl.BlockSpec((bb, DN_V), lambda i: (r0 + i, 0)),
            pl.BlockSpec((bb, LANES), lambda i: (r0 + i, 0)),
            pl.BlockSpec((None, CONV_W - 1, bb, CONV_DIM), lambda i: (l, 0, i, 0)),
            pl.BlockSpec((None, bb, DN_HEADS, DN_DK, DN_DV), lambda i: (l, i, 0, 0, 0)),
            _layer_spec(l, (CONV_W, CONV_DIM)),
            _layer_spec(l, (1, LANES)),
            _layer_spec(l, (1, LANES)),
            _layer_spec(l, (1, DN_DV)),
            pl.BlockSpec(memory_space=pl.ANY),
        ] + s_in_spec + c_in_spec,
        out_specs=[pl.BlockSpec((bb, DN_V), lambda i: (r0 + i, 0)), s_spec, c_spec],
        out_shape=[jax.ShapeDtypeStruct(od.shape, od.dtype), s_sds, c_sds],
        input_output_aliases={9: 0, n_in: 1, n_in + 1: 2} if l else {9: 0},
        compiler_params=pltpu.CompilerParams(
            dimension_semantics=("parallel",), vmem_limit_bytes=VMEM_LIMIT),
        name="gdn_sample",
    )(raw, z, ba, conv_state, dn_state, cw, alog_row, dtb_row, onw, od, *s_in, *c_in)


def _np_consts():
    i = np.arange(TILE)[:, None]
    j = np.arange(TILE)[None, :]
    same = (i // DN_CHUNK) == (j // DN_CHUNK)
    ltri = (same & (j <= i)).astype(np.float32)
    lall = same.astype(np.float32)
    lvls = []
    b = 1
    while b < DN_CHUNK:
        lvls.append(((i // (2 * b)) == (j // (2 * b))) & ((i // b) != (j // b)))
        b *= 2
    lvl = np.stack(lvls).astype(np.float32)
    hi = np.arange(LANES)
    ones_bd = (hi[:, None] // HEAD_DIM == hi[None, :] // HEAD_DIM).astype(np.float32)
    return ltri, lall, lvl, ones_bd


def _rope_tables(pos):
    half = HEAD_DIM // 2
    inv = 1.0 / (ROPE_THETA ** (jnp.arange(half, dtype=F32) / half))
    ang = pos.astype(F32)[:, None] * inv[None, :]
    cos, sin = jnp.cos(ang), jnp.sin(ang)
    cos_t = jnp.concatenate([cos, cos] * (LANES // HEAD_DIM), axis=1)
    sin_t = jnp.concatenate([-sin, sin] * (LANES // HEAD_DIM), axis=1)
    return cos_t, sin_t


def _pad_row(x, offset):
    out = jnp.zeros((x.shape[0], 1, LANES), F32)
    return out.at[:, 0, offset:offset + x.shape[1]].set(x.astype(F32))


def kernel(x_prompt, x_sample, cache_swa_k, cache_swa_v, state_dn, state_conv, ffn1_norm, ffn1_w_gate_up, ffn1_w_down, mix_norm, w_in, q_norm, k_norm, attn_sinks, conv_w, dn_A_log, dn_dt_bias, dn_out_norm, w_attn_o, w_dn_o, w_out, ffn2_norm, ffn2_w_gate_up, ffn2_w_down):
    batch, seq, _ = x_prompt.shape
    dbatch = x_sample.shape[0]
    prows = batch * seq
    rows = prows + dbatch
    perm = np.asarray(HEAD_PERM)

    w_in_t = jnp.transpose(w_in, (0, 2, 1))
    wao = w_attn_o.reshape(DEPTH, N_HEADS, HEAD_DIM, D_MODEL)[:, perm].reshape(DEPTH, ATTN_Q, D_MODEL).astype(BF16)
    wdo = w_dn_o.astype(BF16)
    wout = w_out.astype(BF16)
    wgu1, wd1, wgu2, wd2 = ffn1_w_gate_up, ffn1_w_down, ffn2_w_gate_up, ffn2_w_down
    n1 = ffn1_norm.reshape(DEPTH, 1, D_MODEL)
    n2 = ffn2_norm.reshape(DEPTH, 1, D_MODEL)
    nm = mix_norm.reshape(DEPTH, 1, D_MODEL)
    qk_nw = jnp.concatenate([jnp.tile(q_norm, (1, N_HEADS)), jnp.tile(k_norm, (1, N_KV_HEADS))], axis=1)
    alog_row = _pad_row(dn_A_log, DN_HEADS)
    dtb_row = _pad_row(dn_dt_bias, DN_HEADS)
    onw = dn_out_norm.reshape(DEPTH, 1, DN_DV)
    sinks_perm = attn_sinks[:, perm]
    sinks_b = jnp.broadcast_to(sinks_perm[:, :, None], (DEPTH, N_HEADS, LANES))

    ltri, lall, lvl, ones_bd = _np_consts()
    gdn_consts = (jnp.asarray(ltri, BF16), jnp.asarray(lall, BF16), jnp.asarray(lvl, BF16))
    ones_bd = jnp.asarray(ones_bd, BF16)
    cos_p, sin_p = _rope_tables(jnp.arange(seq))
    cos_s, sin_s = _rope_tables(PAST_LEN + jnp.arange(1))

    ck = jnp.transpose(cache_swa_k, (0, 1, 3, 4, 2))
    cv = jnp.transpose(cache_swa_v, (0, 1, 3, 4, 2))
    cs = jnp.transpose(state_conv, (0, 2, 1, 3))

    split_rows = _can_split_rows(prows, dbatch)
    xp2, xs2 = x_prompt.reshape(prows, D_MODEL), x_sample.reshape(dbatch, D_MODEL)
    kp = vp = sp = cp = ksn = vsn = ssn = csn = None
    for l in range(DEPTH):
        if l > 0:
            h = _ffn(h, n1, wgu1, wd1, l)
        elif split_rows:
            h = _ffn_first(xp2, xs2, n1, wgu1, wd1, l)
        else:
            h = _ffn(jnp.concatenate([xp2, xs2], axis=0), n1, wgu1, wd1, l)
        qkv, raw, z, ba, gates = _inproj(h, nm, w_in_t, l)
        oa, kp, vp = _attn_prompt(qkv, attn_sinks[l], cos_p, sin_p, qk_nw[l:l + 1], ones_bd,
                                  kp, vp, l, batch, seq, rows)
        oa, ksn, vsn = _attn_sample(qkv, ck, cv, cos_s, sin_s, qk_nw[l:l + 1], ones_bd,
                                    sinks_b[l], oa, ksn, vsn, l, prows, dbatch)
        od, sp, cp = _gdn_prompt(raw, z, ba, conv_w, alog_row, dtb_row, onw, gdn_consts,
                                 sp, cp, l, batch, seq, rows)
        od, ssn, csn = _gdn_sample(raw, z, ba, cs, state_dn, conv_w, alog_row, dtb_row, onw,
                                   od, ssn, csn, l, prows, dbatch)
        final_split = (prows, dbatch) if (split_rows and l == DEPTH - 1) else None
        h = _merge_ffn(h, oa, od, gates, wao, wdo, wout, n2, wgu2, wd2, l, split=final_split)

    yp, ys = h if split_rows else (h[:prows], h[prows:])
    kv_shape = (DEPTH, -1, WINDOW, N_KV_HEADS, HEAD_DIM)
    return (yp.reshape(batch, seq, D_MODEL),
            ys.reshape(dbatch, 1, D_MODEL),
            kp.reshape(kv_shape), vp.reshape(kv_shape), sp, cp,
            jnp.transpose(ksn, (0, 1, 4, 2, 3)), jnp.transpose(vsn, (0, 1, 4, 2, 3)), ssn,
            jnp.transpose(csn, (0, 2, 1, 3)))
```

```python
import functools
import math

import numpy as np
import jax
import jax.numpy as jnp
from jax import lax
from jax.experimental import pallas as pl
from jax.experimental.pallas import tpu as pltpu

F32 = jnp.float32
BF16 = jnp.bfloat16

D_MODEL = 1024
DEPTH = 4
PAST_LEN = 8192
N_HEADS = 8
N_KV_HEADS = 2
GROUP = N_HEADS // N_KV_HEADS
HEAD_DIM = 64
WINDOW = 128
ATTN_BLOCK = 128
ROPE_THETA = 10000.0
DN_HEADS = 4
DN_DK = 128
DN_DV = 128
CONV_W = 4
DN_CHUNK = 128
D_FF = 2816
EPS = 1e-6
LOG2E = math.log2(math.e)

ATTN_Q = N_HEADS * HEAD_DIM
ATTN_KV = N_KV_HEADS * HEAD_DIM
DN_QK = DN_HEADS * DN_DK
DN_V = DN_HEADS * DN_DV
CONV_DIM = 2 * DN_QK + DN_V
QKV_W = ATTN_Q + 2 * ATTN_KV
QK_W = ATTN_Q + ATTN_KV
LANES = 128
SUBLANES = 8
TILE = 128
VMEM_LIMIT = 60 * 1024 * 1024

HEAD_PERM = (0, 4, 1, 5, 2, 6, 3, 7)
MXU_COLS = 256
FF_CHUNKS = (3 * MXU_COLS,) * 3 + (2 * MXU_COLS,)
WIDE_ROW_CAP = 768


def _row_tile(rows, cap=512):
    best = SUBLANES
    for t in range(SUBLANES, cap + 1, SUBLANES):
        if rows % t == 0:
            best = t
    return best


def _rms(x, w):
    return x * lax.rsqrt(jnp.mean(x * x, axis=-1, keepdims=True) + EPS) * w


def _dot(a, b):
    return jnp.dot(a, b, preferred_element_type=F32)


def _dot_nt(a, b):
    return lax.dot_general(a, b, (((1,), (1,)), ((), ())), preferred_element_type=F32)


def _split2(x):
    hi = x.astype(BF16)
    return hi, (x - hi.astype(F32)).astype(BF16)


def _split3(x):
    hi = x.astype(BF16)
    r = x - hi.astype(F32)
    mid = r.astype(BF16)
    lo = (r - mid.astype(F32)).astype(BF16)
    return hi, mid, lo


def _const_spec(shape):
    nd = len(shape)
    return pl.BlockSpec(shape, lambda *_: (0,) * nd, pipeline_mode=pl.Buffered(1))


def _layer_spec(l, shape):
    nd = len(shape)
    return pl.BlockSpec((None,) + shape, lambda *_: (l,) + (0,) * nd, pipeline_mode=pl.Buffered(1))


def _stacked_out(l, prev, shape, block, index_map):
    spec = pl.BlockSpec((None,) + block, lambda *idx: (l,) + tuple(index_map(*idx)))
    sds = jax.ShapeDtypeStruct((DEPTH,) + shape, F32)
    if l == 0:
        return spec, sds, [], []
    return spec, sds, [prev], [pl.BlockSpec(memory_space=pl.ANY)]


def _swiglu_residual(x, nw, wgu_ref, wd_ref):
    xn = _rms(x, nw).astype(BF16)
    acc = jnp.zeros_like(x)
    c0 = 0
    for tf in FF_CHUNKS:
        g = _dot(xn, wgu_ref[:, c0:c0 + tf].astype(BF16))
        u = _dot(xn, wgu_ref[:, D_FF + c0:D_FF + c0 + tf].astype(BF16))
        a = (g * jax.nn.sigmoid(g) * u).astype(BF16)
        acc = acc + _dot(a, wd_ref[c0:c0 + tf, :].astype(BF16))
        c0 += tf
    return x + 0.5 * acc


def _ffn_body(x_ref, nw_ref, wgu_ref, wd_ref, o_ref):
    o_ref[...] = _swiglu_residual(x_ref[...], nw_ref[...], wgu_ref, wd_ref)


def _ffn_first_body(*refs):
    *piece_refs, xs_ref, nw_ref, wgu_ref, wd_ref, o_ref = refs
    last = pl.program_id(0) == pl.num_programs(0) - 1
    pieces = [r[...] for r in piece_refs]
    pieces[-1] = jnp.where(last, xs_ref[...], pieces[-1])
    o_ref[...] = _swiglu_residual(jnp.concatenate(pieces, axis=0), nw_ref[...], wgu_ref, wd_ref)


def _ffn_first(xp, xs, norm_w, wgu, wd, l):
    prows, dbatch = xp.shape[0], xs.shape[0]
    rows = prows + dbatch
    tm = _row_tile(rows)
    n_piece = tm // dbatch
    last_piece = prows // dbatch - 1
    piece_spec = lambda p: pl.BlockSpec(
        (dbatch, D_MODEL), lambda i: (jnp.minimum(i * n_piece + p, last_piece), 0))
    return pl.pallas_call(
        _ffn_first_body,
        grid=(rows // tm,),
        in_specs=[piece_spec(p) for p in range(n_piece)] + [
            _const_spec((dbatch, D_MODEL)),
            _layer_spec(l, (1, D_MODEL)),
            _layer_spec(l, (D_MODEL, 2 * D_FF)),
            _layer_spec(l, (D_FF, D_MODEL)),
        ],
        out_specs=pl.BlockSpec((tm, D_MODEL), lambda i: (i, 0)),
        out_shape=jax.ShapeDtypeStruct((rows, D_MODEL), F32),
        compiler_params=pltpu.CompilerParams(
            dimension_semantics=("parallel",), vmem_limit_bytes=VMEM_LIMIT),
        name="ffn_first",
    )(*([xp] * n_piece), xs, norm_w, wgu, wd)


def _can_split_rows(prows, dbatch):
    tm = _row_tile(prows + dbatch)
    return tm % dbatch == 0 and prows % dbatch == 0 and dbatch % SUBLANES == 0


def _ffn(x, norm_w, wgu, wd, l):
    rows = x.shape[0]
    tm = _row_tile(rows, cap=WIDE_ROW_CAP)
    return pl.pallas_call(
        _ffn_body,
        grid=(rows // tm,),
        in_specs=[
            pl.BlockSpec((tm, D_MODEL), lambda i: (i, 0)),
            _layer_spec(l, (1, D_MODEL)),
            _layer_spec(l, (D_MODEL, 2 * D_FF)),
            _layer_spec(l, (D_FF, D_MODEL)),
        ],
        out_specs=pl.BlockSpec((tm, D_MODEL), lambda i: (i, 0)),
        out_shape=jax.ShapeDtypeStruct((rows, D_MODEL), F32),
        compiler_params=pltpu.CompilerParams(
            dimension_semantics=("parallel",), vmem_limit_bytes=VMEM_LIMIT),
        name="ffn",
    )(x, norm_w, wgu, wd)


IN_SEGS = (QKV_W, CONV_DIM, DN_V, LANES, 2 * D_MODEL)
IN_DTYPES = (F32, F32, F32, F32, BF16)
IN_COLS = QKV_W + CONV_DIM + DN_V + 2 * DN_HEADS + 2 * D_MODEL
IN_ALIGNED = QKV_W + CONV_DIM + DN_V


def _inproj_body(h_ref, nw_ref, wt_ref, qkv_ref, raw_ref, z_ref, ba_ref, gates_ref):
    u = _rms(h_ref[...], nw_ref[...]).astype(BF16)

    def proj(r0, r1):
        return _dot_nt(u, wt_ref[r0:r1, :].astype(BF16))

    qkv = proj(0, QKV_W)
    lo = lax.broadcasted_iota(jnp.int32, (qkv.shape[0], LANES), 1) < HEAD_DIM
    nat = [qkv[:, c * LANES:(c + 1) * LANES] for c in range(ATTN_Q // LANES)]
    for g in range(GROUP):
        first, second = nat[g // 2], nat[(GROUP + g) // 2]
        if g % 2 == 0:
            second = pltpu.roll(second, HEAD_DIM, 1)
        else:
            first = pltpu.roll(first, HEAD_DIM, 1)
        qkv_ref[:, g * LANES:(g + 1) * LANES] = jnp.where(lo, first, second)
    qkv_ref[:, ATTN_Q:] = qkv[:, ATTN_Q:]
    raw_ref[...] = proj(QKV_W, QKV_W + CONV_DIM)
    z_ref[...] = proj(QKV_W + CONV_DIM, IN_ALIGNED)
    ba_ref[...] = proj(IN_ALIGNED, IN_ALIGNED + LANES)
    gates_ref[...] = proj(IN_ALIGNED + 2 * DN_HEADS, IN_COLS).astype(gates_ref.dtype)


def _inproj(h, norm_w, w_in_t, l):
    rows = h.shape[0]
    tm = _row_tile(rows, cap=WIDE_ROW_CAP)
    return pl.pallas_call(
        _inproj_body,
        grid=(rows // tm,),
        in_specs=[
            pl.BlockSpec((tm, D_MODEL), lambda i: (i, 0)),
            _layer_spec(l, (1, D_MODEL)),
            _layer_spec(l, (IN_COLS, D_MODEL)),
        ],
        out_specs=[pl.BlockSpec((tm, w), lambda i: (i, 0)) for w in IN_SEGS],
        out_shape=[jax.ShapeDtypeStruct((rows, w), dt) for w, dt in zip(IN_SEGS, IN_DTYPES)],
        compiler_params=pltpu.CompilerParams(
            dimension_semantics=("parallel",), vmem_limit_bytes=VMEM_LIMIT),
        name="inproj",
    )(h, norm_w, w_in_t)


def _merge_ffn_body(h_ref, oa_ref, od_ref, gates_ref, wao_ref, wdo_ref, wout_ref,
                    nw_ref, wgu_ref, wd_ref, o_ref, *sample_out):
    br_a = _dot(oa_ref[...], wao_ref[...])
    br_d = _dot(od_ref[...], wdo_ref[...])
    ga = gates_ref[:, :D_MODEL].astype(F32)
    gd = gates_ref[:, D_MODEL:].astype(F32)
    m = jax.nn.sigmoid(ga) * br_a + jax.nn.sigmoid(gd) * br_d
    h = h_ref[...] + _dot(m.astype(BF16), wout_ref[...])
    y = _swiglu_residual(h, nw_ref[...], wgu_ref, wd_ref)
    o_ref[...] = y
    if sample_out:
        (os_ref,) = sample_out

        @pl.when(pl.program_id(0) == pl.num_programs(0) - 1)
        def _():
            os_ref[...] = y[y.shape[0] - os_ref.shape[0]:, :]


def _merge_ffn(h, oa, od, gates, wao, wdo, wout, norm_w, wgu, wd, l, split=None):
    rows = h.shape[0]
    tm = _row_tile(rows)
    out_specs = pl.BlockSpec((tm, D_MODEL), lambda i: (i, 0))
    out_shape = jax.ShapeDtypeStruct((rows, D_MODEL), F32)
    if split is not None:
        prows, dbatch = split
        out_specs = [out_specs, pl.BlockSpec((dbatch, D_MODEL), lambda i: (0, 0))]
        out_shape = [jax.ShapeDtypeStruct((prows, D_MODEL), F32), jax.ShapeDtypeStruct((dbatch, D_MODEL), F32)]
    return pl.pallas_call(
        _merge_ffn_body,
        grid=(rows // tm,),
        in_specs=[
            pl.BlockSpec((tm, D_MODEL), lambda i: (i, 0)),
            pl.BlockSpec((tm, ATTN_Q), lambda i: (i, 0)),
            pl.BlockSpec((tm, DN_V), lambda i: (i, 0)),
            pl.BlockSpec((tm, 2 * D_MODEL), lambda i: (i, 0)),
            _layer_spec(l, (ATTN_Q, D_MODEL)),
            _layer_spec(l, (DN_V, D_MODEL)),
            _layer_spec(l, (D_MODEL, D_MODEL)),
            _layer_spec(l, (1, D_MODEL)),
            _layer_spec(l, (D_MODEL, 2 * D_FF)),
            _layer_spec(l, (D_FF, D_MODEL)),
        ],
        out_specs=out_specs,
        out_shape=out_shape,
        compiler_params=pltpu.CompilerParams(
            dimension_semantics=("arbitrary",), vmem_limit_bytes=VMEM_LIMIT),
        name="merge_ffn",
    )(h, oa, od, gates, wao, wdo, wout, norm_w, wgu, wd)


def _norm_rope(x, ones_bd, nw, cos, sin):
    lane = lax.broadcasted_iota(jnp.int32, (x.shape[0], LANES), 1)
    first_half = (lane & (HEAD_DIM // 2)) == 0
    outs = []
    for g in range(x.shape[1] // LANES):
        xg = x[:, g * LANES:(g + 1) * LANES]
        hi, lo = _split2(xg * xg)
        ssq = _dot(hi, ones_bd) + _dot(lo, ones_bd)
        xn = xg * lax.rsqrt(ssq * (1.0 / HEAD_DIM) + EPS) * nw[:, g * LANES:(g + 1) * LANES]
        partner = jnp.where(first_half,
                            pltpu.roll(xn, LANES - HEAD_DIM // 2, 1),
                            pltpu.roll(xn, HEAD_DIM // 2, 1))
        outs.append(xn * cos + partner * sin)
    return jnp.concatenate(outs, axis=1)


ATTN_STEP_BLOCKS = 2


def _attn_prompt_body(sinks_ref, x_ref, cos_ref, sin_ref, nw_ref, ones_ref, *rest):
    o_ref, kout_ref, vout_ref, kprev, vprev = rest[-5:]
    step = pl.program_id(1)

    @pl.when(step == 0)
    def _():
        kprev[...] = jnp.zeros_like(kprev)
        vprev[...] = jnp.zeros_like(vprev)

    x = x_ref[...]
    qk = _norm_rope(x[:, :QK_W], ones_ref[...], nw_ref[...], cos_ref[...], sin_ref[...])
    k_all = qk[:, ATTN_Q:QK_W]
    v_all = x[:, QK_W:QKV_W]
    k_bf = jnp.concatenate([kprev[...], k_all.astype(BF16)], axis=0)
    v_bf = jnp.concatenate([vprev[...], v_all.astype(BF16)], axis=0)

    r = lax.broadcasted_iota(jnp.int32, (ATTN_BLOCK, 2 * ATTN_BLOCK), 0)
    c = lax.broadcasted_iota(jnp.int32, (ATTN_BLOCK, 2 * ATTN_BLOCK), 1)
    band = (c >= r) & (c <= r + WINDOW)
    first_col = jnp.where(step == 0, ATTN_BLOCK, 0)
    lo = lax.broadcasted_iota(jnp.int32, (ATTN_BLOCK, LANES), 1) < HEAD_DIM
    scale = LOG2E / math.sqrt(HEAD_DIM)
    klane = lax.broadcasted_iota(jnp.int32, k_bf.shape, 1) < HEAD_DIM
    k_half = (jnp.where(klane, k_bf, jnp.zeros_like(k_bf)), jnp.where(klane, jnp.zeros_like(k_bf), k_bf))

    units = [(sb, g, half) for sb in range(ATTN_STEP_BLOCKS) for g in range(GROUP) for half in range(2)]
    q_bf = {(sb, g): (qk[sb * ATTN_BLOCK:(sb + 1) * ATTN_BLOCK, g * LANES:(g + 1) * LANES] * scale).astype(BF16)
            for sb in range(ATTN_STEP_BLOCKS) for g in range(GROUP)}
    s_raw = {}
    for (sb, g, half) in units:
        s_raw[sb, g, half] = _dot_nt(q_bf[sb, g], k_half[half][sb * ATTN_BLOCK:(sb + 2) * ATTN_BLOCK])
    p_den = {}
    for (sb, g, half) in units:
        vis = band & (c >= first_col) if sb == 0 else band
        s = jnp.where(vis, s_raw.pop((sb, g, half)), -jnp.inf)
        sink = sinks_ref[HEAD_PERM[2 * g + half]] * LOG2E
        m = jnp.maximum(jnp.max(s, axis=1, keepdims=True), sink)
        p = jnp.exp2(s - m)
        p_den[sb, g, half] = (p.astype(BF16), jnp.sum(p, axis=1, keepdims=True) + jnp.exp2(sink - m))
    for sb in range(ATTN_STEP_BLOCKS):
        outs = []
        for g in range(GROUP):
            res = []
            for half in range(2):
                p, denom = p_den.pop((sb, g, half))
                res.append(_dot(p, v_bf[sb * ATTN_BLOCK:(sb + 2) * ATTN_BLOCK]) / denom)
            outs.append(jnp.where(lo, res[0], res[1]))
        o_ref[sb * ATTN_BLOCK:(sb + 1) * ATTN_BLOCK, :] = jnp.concatenate(outs, axis=1).astype(o_ref.dtype)

    last = slice((ATTN_STEP_BLOCKS - 1) * ATTN_BLOCK, ATTN_STEP_BLOCKS * ATTN_BLOCK)
    kprev[...] = k_all[last].astype(BF16)
    vprev[...] = v_all[last].astype(BF16)

    @pl.when(step == pl.num_programs(1) - 1)
    def _():
        kout_ref[...] = k_all[last]
        vout_ref[...] = v_all[last]


def _attn_prompt(qkv, sinks, cos, sin, nw, ones_bd, kstack, vstack, l, batch, seq, rows):
    rows_step = ATTN_STEP_BLOCKS * ATTN_BLOCK
    nb = seq // rows_step
    kv_block = (None, WINDOW, ATTN_KV)
    kv_idx = lambda b, j: (b, 0, 0)
    k_spec, k_sds, k_in, k_in_spec = _stacked_out(l, kstack, (batch, WINDOW, ATTN_KV), kv_block, kv_idx)
    v_spec, v_sds, v_in, v_in_spec = _stacked_out(l, vstack, (batch, WINDOW, ATTN_KV), kv_block, kv_idx)
    n_in = 6
    return pl.pallas_call(
        _attn_prompt_body,
        grid=(batch, nb),
        in_specs=[
            pl.BlockSpec(memory_space=pltpu.SMEM),
            pl.BlockSpec((rows_step, QKV_W), lambda b, j: (b * nb + j, 0)),
            pl.BlockSpec((rows_step, LANES), lambda b, j: (j, 0)),
            pl.BlockSpec((rows_step, LANES), lambda b, j: (j, 0)),
            _const_spec((1, QK_W)),
            _const_spec((LANES, LANES)),
        ] + k_in_spec + v_in_spec,
        out_specs=[pl.BlockSpec((rows_step, ATTN_Q), lambda b, j: (b * nb + j, 0)), k_spec, v_spec],
        out_shape=[jax.ShapeDtypeStruct((rows, ATTN_Q), BF16), k_sds, v_sds],
        input_output_aliases={n_in: 1, n_in + 1: 2} if l else {},
        scratch_shapes=[pltpu.VMEM((ATTN_BLOCK, ATTN_KV), BF16),
                        pltpu.VMEM((ATTN_BLOCK, ATTN_KV), BF16)],
        compiler_params=pltpu.CompilerParams(
            dimension_semantics=("parallel", "arbitrary"), vmem_limit_bytes=VMEM_LIMIT),
        name="attn_prompt",
    )(sinks, qkv, cos, sin, nw, ones_bd, *k_in, *v_in)


SAMPLE_BB = 16


def _attn_sample_body(x_ref, kc_ref, vc_ref, cos_ref, sin_ref, nw_ref, ones_ref, sinks_ref, *rest):
    o_ref, kout_ref, vout_ref = rest[-3:]
    x = x_ref[...]
    qk = _norm_rope(x[:, :QK_W], ones_ref[...], nw_ref[...], cos_ref[...], sin_ref[...])
    knew = qk[:, ATTN_Q:QK_W]
    vnew = x[:, QK_W:QKV_W]
    scale = 1.0 / math.sqrt(HEAD_DIM)
    row = lax.broadcasted_iota(jnp.int32, (N_HEADS, LANES), 0)
    lane = lax.broadcasted_iota(jnp.int32, (N_HEADS, LANES), 1)
    own = (lane < HEAD_DIM) == ((row & 1) == 0)
    lo1 = lax.broadcasted_iota(jnp.int32, (1, LANES), 1) < HEAD_DIM
    sinks = sinks_ref[...]
    units = range(SAMPLE_BB)
    qz, s_all, soft = {}, {}, {}
    for b in units:
        qrows = [qk[b:b + 1, (i // 2) * LANES:(i // 2 + 1) * LANES] for i in range(N_HEADS)]
        qz[b] = jnp.where(own, jnp.concatenate(qrows, axis=0) * scale, 0.0)
        s_all[b] = _dot(qz[b].astype(BF16), kc_ref[b].reshape(ATTN_KV, WINDOW).astype(BF16))
    sink = sinks[:, 0:1]
    for b in units:
        s = s_all.pop(b)
        s_new = jnp.sum(qz.pop(b) * knew[b:b + 1, :], axis=1, keepdims=True)
        m = jnp.maximum(jnp.maximum(jnp.max(s, axis=1, keepdims=True), s_new), sink)
        p = jnp.exp(s - m)
        p_new = jnp.exp(s_new - m)
        soft[b] = (p.astype(BF16), p_new, jnp.sum(p, axis=1, keepdims=True) + p_new + jnp.exp(sink - m))
    o_rows = []
    for b in units:
        p, p_new, denom = soft.pop(b)
        pv = (_dot_nt(p, vc_ref[b].reshape(ATTN_KV, WINDOW).astype(BF16)) + p_new * vnew[b:b + 1, :]) / denom
        o_rows.append(jnp.concatenate(
            [jnp.where(lo1, pv[2 * g:2 * g + 1, :], pv[2 * g + 1:2 * g + 2, :]) for g in range(GROUP)],
            axis=1))
    pad = jnp.zeros((LANES - SAMPLE_BB, ATTN_KV), F32)
    knew_t = jnp.concatenate([knew, pad], axis=0).T
    vnew_t = jnp.concatenate([vnew, pad], axis=0).T
    newest = lax.broadcasted_iota(jnp.int32, (ATTN_KV, WINDOW), 1) == WINDOW - 1
    kv_block = (N_KV_HEADS, HEAD_DIM, WINDOW)
    for b in units:
        for src_ref, new_t, dst_ref in ((kc_ref, knew_t, kout_ref), (vc_ref, vnew_t, vout_ref)):
            shifted = pltpu.roll(src_ref[b].reshape(ATTN_KV, WINDOW), WINDOW - 1, 1)
            dst_ref[b] = jnp.where(newest, new_t[:, b:b + 1], shifted).reshape(kv_block)
    o_ref[...] = jnp.concatenate(o_rows, axis=0).astype(o_ref.dtype)


def _attn_sample(qkv, cache_k, cache_v, cos, sin, nw, ones_bd, sinks_b, oa, kstack, vstack,
                 l, row0, dbatch):
    bb = SAMPLE_BB
    r0 = row0 // bb
    kv_block = (bb, N_KV_HEADS, HEAD_DIM, WINDOW)
    kv_idx = lambda i: (i, 0, 0, 0)
    kv_shape = (dbatch, N_KV_HEADS, HEAD_DIM, WINDOW)
    k_spec, k_sds, k_in, k_in_spec = _stacked_out(l, kstack, kv_shape, kv_block, kv_idx)
    v_spec, v_sds, v_in, v_in_spec = _stacked_out(l, vstack, kv_shape, kv_block, kv_idx)
    n_in = 9
    return pl.pallas_call(
        _attn_sample_body,
        grid=(dbatch // bb,),
        in_specs=[
            pl.BlockSpec((bb, QKV_W), lambda i: (r0 + i, 0)),
            pl.BlockSpec((None,) + kv_block, lambda i: (l, i, 0, 0, 0)),
            pl.BlockSpec((None,) + kv_block, lambda i: (l, i, 0, 0, 0)),
            _const_spec((1, LANES)),
            _const_spec((1, LANES)),
            _const_spec((1, QK_W)),
            _const_spec((LANES, LANES)),
            _const_spec((N_HEADS, LANES)),
            pl.BlockSpec(memory_space=pl.ANY),
        ] + k_in_spec + v_in_spec,
        out_specs=[pl.BlockSpec((bb, ATTN_Q), lambda i: (r0 + i, 0)), k_spec, v_spec],
        out_shape=[jax.ShapeDtypeStruct(oa.shape, oa.dtype), k_sds, v_sds],
        input_output_aliases={8: 0, n_in: 1, n_in + 1: 2} if l else {8: 0},
        compiler_params=pltpu.CompilerParams(
            dimension_semantics=("parallel",), vmem_limit_bytes=VMEM_LIMIT),
        name="attn_sample",
    )(qkv, cache_k, cache_v, cos, sin, nw, ones_bd, sinks_b, oa, *k_in, *v_in)


def _softplus(x):
    return jnp.maximum(x, 0.0) + jnp.log(1.0 + jnp.exp(-jnp.abs(x)))


def _silu(x):
    return x * jax.nn.sigmoid(x)


def _l2n(x):
    return x * lax.rsqrt(jnp.sum(x * x, axis=-1, keepdims=True) + EPS)


def _l2n_mxu(x, ones):
    hi, lo = _split2(x * x)
    return x * lax.rsqrt(_dot(hi, ones) + _dot(lo, ones) + EPS)


def _gate_rows(ba, alog_row, dtb_row):
    beta = jax.nn.sigmoid(ba)
    g = -jnp.exp(alog_row) * _softplus(ba + dtb_row)
    return beta, g


GROUP_TILES = 4
GROUP_TOK = GROUP_TILES * TILE
CHUNKS = TILE // DN_CHUNK
INV_LEVELS = DN_CHUNK.bit_length() - 1


def _gdn_prompt_body(raw_ref, z_ref, ba_ref, cw_ref, alog_ref, dtb_ref, onw_ref,
                     ltri_ref, lall_ref, lvl_ref, *rest, groups_per_seq):
    od_ref, sout_ref, cout_ref, xp, s_scr, ub_scr, wq_scr, kq_scr, egl_scr = rest[-9:]
    i = pl.program_id(0)
    n_groups = pl.num_programs(0) - 1
    ga = jnp.minimum(i, n_groups - 1)
    gb = jnp.maximum(i - 1, 0)
    slot_a = i % 2
    slot_b = 1 - slot_a

    @pl.when(i == 0)
    def _():
        ub_scr[1] = jnp.zeros(ub_scr.shape[1:], F32)
        wq_scr[1] = jnp.zeros(wq_scr.shape[1:], BF16)
        kq_scr[1] = jnp.zeros(kq_scr.shape[1:], BF16)
        egl_scr[1] = jnp.zeros(egl_scr.shape[1:], F32)

    @pl.when(ga % groups_per_seq == 0)
    def _():
        xp[0:SUBLANES, :] = jnp.zeros((SUBLANES, CONV_DIM), F32)

    @pl.when(gb % groups_per_seq == 0)
    def _():
        s_scr[...] = jnp.zeros_like(s_scr)

    @pl.when(ga % groups_per_seq == groups_per_seq - 1)
    def _():
        cout_ref[...] = raw_ref[GROUP_TOK - (CONV_W - 1):GROUP_TOK, :]

    ltri = ltri_ref[...]
    lall = lall_ref[...]
    ri = lax.broadcasted_iota(jnp.int32, (TILE, TILE), 0)
    ci = lax.broadcasted_iota(jnp.int32, (TILE, TILE), 1)
    same = (ri // DN_CHUNK) == (ci // DN_CHUNK)
    causal = same & (ci <= ri)
    strict = same & (ci < ri)
    eye_bf = (ri == ci).astype(F32).astype(BF16)
    zeros_half = jnp.zeros((DN_CHUNK, DN_DV), BF16)
    onw = onw_ref[...]
    ones_sq = jnp.ones((DN_DK, LANES), BF16)
    units = [(j, h) for j in range(GROUP_TILES) for h in range(DN_HEADS)]

    st = [s_scr[h] for h in range(DN_HEADS)]
    b_live = {}

    def b_stage1(j, c):
        for h in range(DN_HEADS):
            idx = j * DN_HEADS + h
            r1 = _dot(wq_scr[slot_b, idx, 2 * c * DN_CHUNK:(2 * c + 2) * DN_CHUNK, :], st[h].astype(BF16))
            u = ub_scr[slot_b, idx, c * DN_CHUNK:(c + 1) * DN_CHUNK, :] - r1[:DN_CHUNK]
            parts = [zeros_half] * CHUNKS
            parts[c] = u.astype(BF16)
            b_live[h] = (jnp.concatenate(parts, axis=0), r1[DN_CHUNK:])

    def b_stage2(j, c):
        egl_f = egl_scr[slot_b, j]
        for h in range(DN_HEADS):
            idx = j * DN_HEADS + h
            u_full, oq = b_live[h]
            base = c * (DN_DK + DN_CHUNK)
            r2 = _dot(kq_scr[slot_b, idx, base:base + DN_DK + DN_CHUNK, :], u_full)
            o = oq + r2[DN_DK:]
            zh = z_ref[j * TILE + c * DN_CHUNK:j * TILE + (c + 1) * DN_CHUNK, h * DN_DV:(h + 1) * DN_DV]
            od_ref[j * TILE + c * DN_CHUNK:j * TILE + (c + 1) * DN_CHUNK, h * DN_DV:(h + 1) * DN_DV] = (
                _rms(o, onw) * _silu(zh)).astype(od_ref.dtype)
            st[h] = (st[h] * egl_f[c * DN_CHUNK:c * DN_CHUNK + 1, DN_HEADS + h:DN_HEADS + h + 1]
                     + r2[:DN_DK])

    b_stages = []
    for j in range(GROUP_TILES):
        for c in range(CHUNKS):
            b_stages.append(functools.partial(b_stage1, j, c))
            b_stages.append(functools.partial(b_stage2, j, c))
    b_iter = iter(b_stages)

    def b_step():
        f = next(b_iter, None)
        if f is not None:
            f()

    xp[SUBLANES:SUBLANES + GROUP_TOK, :] = raw_ref[...]
    ys = []
    for j in range(GROUP_TILES):
        r0 = SUBLANES + j * TILE
        x0 = xp[r0 - SUBLANES:r0 + TILE, :]
        x1 = xp[r0 - SUBLANES - 1:r0 + TILE - 1, :] if j else jnp.concatenate(
            [jnp.zeros((1, CONV_DIM), F32), x0[:-1]], axis=0)
        near = x0 * cw_ref[3:4, :] + x1 * cw_ref[2:3, :]
        far = x0 * cw_ref[1:2, :] + x1 * cw_ref[0:1, :]
        ys.append(_silu(near[SUBLANES:] + far[SUBLANES - 2:TILE + SUBLANES - 2]))
        b_step()
    xp[0:SUBLANES, :] = raw_ref[GROUP_TOK - SUBLANES:GROUP_TOK, :]

    tiles = []
    for j in range(GROUP_TILES):
        beta_f, g_f = _gate_rows(ba_ref[j * TILE:(j + 1) * TILE, :], alog_ref[...], dtb_ref[...])
        g_parts = _split3(g_f)
        gcum = sum(_dot(ltri, part) for part in g_parts)
        glast = sum(_dot(lall, part) for part in g_parts)
        egl_scr[slot_a, j] = jnp.exp(glast)
        tiles.append((beta_f, gcum, glast, gcum.T))

    u = {}
    for (j, h) in units:
        beta_f, gcum, glast, gcum_t = tiles[j]
        y = ys[j]
        k = _l2n_mxu(y[:, DN_QK + h * DN_DK:DN_QK + (h + 1) * DN_DK], ones_sq)
        gc = gcum[:, DN_HEADS + h:DN_HEADS + h + 1]
        gr = gcum_t[DN_HEADS + h:DN_HEADS + h + 1, :]
        u[j, h] = dict(k=k, kb=k.astype(BF16), beta=beta_f[:, h:h + 1], gc=gc,
                       gl=glast[:, DN_HEADS + h:DN_HEADS + h + 1],
                       decay=jnp.exp(jnp.where(causal, gc - gr, -jnp.inf)))
    for un in units:
        d = u[un]
        d["kk"] = _dot_nt(d["kb"], d["kb"])
    b_step()
    for un in units:
        d = u[un]
        a = jnp.where(strict, d["beta"] * d.pop("kk") * d["decay"], 0.0)
        d["ab"] = a.astype(BF16)
        d["tb"] = eye_bf - d["ab"] * lvl_ref[0]
    for lv in range(1, INV_LEVELS):
        for un in units:
            d = u[un]
            d["p"] = _dot(d["tb"], d["ab"] * lvl_ref[lv]).astype(BF16)
        b_step()
        for un in units:
            d = u[un]
            d["tb"] = d["tb"] - _dot(d.pop("p"), d["tb"]).astype(BF16)
        b_step()
    for (j, h) in units:
        d = u[j, h]
        y = ys[j]
        v = y[:, 2 * DN_QK + h * DN_DV:2 * DN_QK + (h + 1) * DN_DV]
        d["eg"] = jnp.exp(d["gc"])
        rhs = jnp.concatenate([v * d["beta"], d["k"] * (d["beta"] * d["eg"])], axis=1).astype(BF16)
        d["sol"] = _dot(d.pop("tb"), rhs)
    b_step()
    for (j, h) in units:
        d = u[j, h]
        y = ys[j]
        q = _l2n_mxu(y[:, h * DN_DK:(h + 1) * DN_DK], ones_sq) * (DN_DK ** -0.5)
        d["qk"] = (_dot_nt(q.astype(BF16), d["kb"]) * d["decay"]).astype(BF16)
        d["qd"] = (q * d["eg"]).astype(BF16)
    b_step()
    for _ in range(len(b_stages)):
        b_step()
    for (j, h) in units:
        d = u[j, h]
        idx = j * DN_HEADS + h
        sol = d["sol"]
        ub_scr[slot_a, idx] = sol[:, :DN_DV]
        w = sol[:, DN_DV:].astype(BF16)
        kd_t = (d["k"] * jnp.exp(d["gl"] - d["gc"])).T.astype(BF16)
        for c in range(CHUNKS):
            rows = slice(c * DN_CHUNK, (c + 1) * DN_CHUNK)
            wq_scr[slot_a, idx, 2 * c * DN_CHUNK:(2 * c + 1) * DN_CHUNK, :] = w[rows]
            wq_scr[slot_a, idx, (2 * c + 1) * DN_CHUNK:(2 * c + 2) * DN_CHUNK, :] = d["qd"][rows]
            base = c * (DN_DK + DN_CHUNK)
            kq_scr[slot_a, idx, base:base + DN_DK, :] = kd_t
            kq_scr[slot_a, idx, base + DN_DK:base + DN_DK + DN_CHUNK, :] = d["qk"][rows]

    for h in range(DN_HEADS):
        s_scr[h] = st[h]

    @pl.when(gb % groups_per_seq == groups_per_seq - 1)
    def _():
        for h in range(DN_HEADS):
            sout_ref[h] = st[h]


def _gdn_prompt(raw, z, ba, cw, alog_row, dtb_row, onw, consts, sstack, cstack, l, batch, seq, rows):
    gps = seq // GROUP_TOK
    ng = batch * gps
    s_spec, s_sds, s_in, s_in_spec = _stacked_out(
        l, sstack, (batch, DN_HEADS, DN_DK, DN_DV), (None, DN_HEADS, DN_DK, DN_DV),
        lambda i: (jnp.maximum(i - 1, 0) // gps, 0, 0, 0))
    c_spec, c_sds, c_in, c_in_spec = _stacked_out(
        l, cstack, (batch, CONV_W - 1, CONV_DIM), (None, CONV_W - 1, CONV_DIM),
        lambda i: (jnp.minimum(i, ng - 1) // gps, 0, 0))
    n_in = 10
    ltri, lall, lvl = consts
    nht = GROUP_TILES * DN_HEADS
    a_idx = lambda i: (jnp.minimum(i, ng - 1), 0)
    b_idx = lambda i: (jnp.maximum(i - 1, 0), 0)
    return pl.pallas_call(
        functools.partial(_gdn_prompt_body, groups_per_seq=gps),
        grid=(ng + 1,),
        in_specs=[
            pl.BlockSpec((GROUP_TOK, CONV_DIM), a_idx),
            pl.BlockSpec((GROUP_TOK, DN_V), b_idx),
            pl.BlockSpec((GROUP_TOK, LANES), a_idx),
            _layer_spec(l, (CONV_W, CONV_DIM)),
            _layer_spec(l, (1, LANES)),
            _layer_spec(l, (1, LANES)),
            _layer_spec(l, (1, DN_DV)),
            _const_spec((TILE, TILE)),
            _const_spec((TILE, TILE)),
            _const_spec((INV_LEVELS, TILE, TILE)),
        ] + s_in_spec + c_in_spec,
        out_specs=[pl.BlockSpec((GROUP_TOK, DN_V), b_idx), s_spec, c_spec],
        out_shape=[jax.ShapeDtypeStruct((rows, DN_V), BF16), s_sds, c_sds],
        input_output_aliases={n_in: 1, n_in + 1: 2} if l else {},
        scratch_shapes=[
            pltpu.VMEM((SUBLANES + GROUP_TOK, CONV_DIM), F32),
            pltpu.VMEM((DN_HEADS, DN_DK, DN_DV), F32),
            pltpu.VMEM((2, nht, TILE, DN_DV), F32),
            pltpu.VMEM((2, nht, 2 * TILE, DN_DK), BF16),
            pltpu.VMEM((2, nht, CHUNKS * (DN_DK + DN_CHUNK), TILE), BF16),
            pltpu.VMEM((2, GROUP_TILES, TILE, LANES), F32),
        ],
        compiler_params=pltpu.CompilerParams(
            dimension_semantics=("arbitrary",), vmem_limit_bytes=VMEM_LIMIT),
        name="gdn_prompt",
    )(raw, z, ba, cw, alog_row, dtb_row, onw, ltri, lall, lvl, *s_in, *c_in)


def _gdn_sample_body(raw_ref, z_ref, ba_ref, cs_ref, st_ref, cw_ref, alog_ref, dtb_ref, onw_ref, *rest):
    od_ref, sout_ref, cout_ref = rest[-3:]
    bb = SAMPLE_BB
    raw = raw_ref[...]
    y = raw * cw_ref[CONV_W - 1:CONV_W, :]
    for i in range(CONV_W - 1):
        y = y + cs_ref[i] * cw_ref[i:i + 1, :]
    y = _silu(y)
    for i in range(CONV_W - 2):
        cout_ref[i] = cs_ref[i + 1]
    cout_ref[CONV_W - 2] = raw

    beta_f, g_f = _gate_rows(ba_ref[...], alog_ref[...], dtb_ref[...])
    eg_f = jnp.exp(g_f)
    pad = jnp.zeros((LANES - bb, DN_DK), F32)
    outs = []
    for h in range(DN_HEADS):
        q = _l2n(y[:, h * DN_DK:(h + 1) * DN_DK]) * (DN_DK ** -0.5)
        k = _l2n(y[:, DN_QK + h * DN_DK:DN_QK + (h + 1) * DN_DK])
        v = y[:, 2 * DN_QK + h * DN_DV:2 * DN_QK + (h + 1) * DN_DV]
        k_t = jnp.concatenate([k, pad], axis=0).T
        qk = jnp.sum(q * k, axis=1, keepdims=True)
        o_rows = []
        for b in range(bb):
            s1 = st_ref[b, h] * eg_f[b:b + 1, DN_HEADS + h:DN_HEADS + h + 1]
            kq = jnp.concatenate([k[b:b + 1, :], q[b:b + 1, :]], axis=0).astype(BF16)
            r = _dot(kq, s1.astype(BF16))
            delta = beta_f[b:b + 1, h:h + 1] * (v[b:b + 1, :] - r[0:1, :])
            sout_ref[b, h] = s1 + k_t[:, b:b + 1] * delta
            o_rows.append(r[1:2, :] + qk[b:b + 1, :] * delta)
        o = jnp.concatenate(o_rows, axis=0)
        zh = z_ref[:, h * DN_DV:(h + 1) * DN_DV]
        outs.append(_rms(o, onw_ref[...]) * _silu(zh))
    od_ref[...] = jnp.concatenate(outs, axis=1).astype(od_ref.dtype)


def _gdn_sample(raw, z, ba, conv_state, dn_state, cw, alog_row, dtb_row, onw, od, sstack, cstack,
                l, row0, dbatch):
    bb = SAMPLE_BB
    r0 = row0 // bb
    s_spec, s_sds, s_in, s_in_spec = _stacked_out(
        l, sstack, (dbatch, DN_HEADS, DN_DK, DN_DV), (bb, DN_HEADS, DN_DK, DN_DV), lambda i: (i, 0, 0, 0))
    c_spec, c_sds, c_in, c_in_spec = _stacked_out(
        l, cstack, (CONV_W - 1, dbatch, CONV_DIM), (CONV_W - 1, bb, CONV_DIM), lambda i: (0, i, 0))
    n_in = 10
    return pl.pallas_call(
        _gdn_sample_body,
        grid=(dbatch // bb,),
        in_specs=[
            pl.BlockSpec((bb, CONV_DIM), lambda i: (r0 + i, 0)),
            pl.BlockSpec((bb, DN_V), lambda i: (r0 + i, 0)),
            pl.BlockSpec((bb, LANES), lambda i: (r0 + i, 0)),
            pl.BlockSpec((None, CONV_W - 1, bb, CONV_DIM), lambda i: (l, 0, i, 0)),
            pl.BlockSpec((None, bb, DN_HEADS, DN_DK, DN_DV), lambda i: (l, i, 0, 0, 0)),
            _layer_spec(l, (CONV_W, CONV_DIM)),
            _layer_spec(l, (1, LANES)),
            _layer_spec(l, (1, LANES)),
            _layer_spec(l, (1, DN_DV)),
            pl.BlockSpec(memory_space=pl.ANY),
        ] + s_in_spec + c_in_spec,
        out_specs=[pl.BlockSpec((bb, DN_V), lambda i: (r0 + i, 0)), s_spec, c_spec],
        out_shape=[jax.ShapeDtypeStruct(od.shape, od.dtype), s_sds, c_sds],
        input_output_aliases={9: 0, n_in: 1, n_in + 1: 2} if l else {9: 0},
        compiler_params=pltpu.CompilerParams(
            dimension_semantics=("parallel",), vmem_limit_bytes=VMEM_LIMIT),
        name="gdn_sample",
    )(raw, z, ba, conv_state, dn_state, cw, alog_row, dtb_row, onw, od, *s_in, *c_in)


def _np_consts():
    i = np.arange(TILE)[:, None]
    j = np.arange(TILE)[None, :]
    same = (i // DN_CHUNK) == (j // DN_CHUNK)
    ltri = (same & (j <= i)).astype(np.float32)
    lall = same.astype(np.float32)
    lvls = []
    b = 1
    while b < DN_CHUNK:
        lvls.append(((i // (2 * b)) == (j // (2 * b))) & ((i // b) != (j // b)))
        b *= 2
    lvl = np.stack(lvls).astype(np.float32)
    hi = np.arange(LANES)
    ones_bd = (hi[:, None] // HEAD_DIM == hi[None, :] // HEAD_DIM).astype(np.float32)
    return ltri, lall, lvl, ones_bd


def _rope_tables(pos):
    half = HEAD_DIM // 2
    inv = 1.0 / (ROPE_THETA ** (jnp.arange(half, dtype=F32) / half))
    ang = pos.astype(F32)[:, None] * inv[None, :]
    cos, sin = jnp.cos(ang), jnp.sin(ang)
    cos_t = jnp.concatenate([cos, cos] * (LANES // HEAD_DIM), axis=1)
    sin_t = jnp.concatenate([-sin, sin] * (LANES // HEAD_DIM), axis=1)
    return cos_t, sin_t


def _pad_row(x, offset):
    out = jnp.zeros((x.shape[0], 1, LANES), F32)
    return out.at[:, 0, offset:offset + x.shape[1]].set(x.astype(F32))


def kernel(x_prompt, x_sample, cache_swa_k, cache_swa_v, state_dn, state_conv, ffn1_norm, ffn1_w_gate_up, ffn1_w_down, mix_norm, w_in, q_norm, k_norm, attn_sinks, conv_w, dn_A_log, dn_dt_bias, dn_out_norm, w_attn_o, w_dn_o, w_out, ffn2_norm, ffn2_w_gate_up, ffn2_w_down):
    batch, seq, _ = x_prompt.shape
    dbatch = x_sample.shape[0]
    prows = batch * seq
    rows = prows + dbatch
    perm = np.asarray(HEAD_PERM)

    w_in_t = jnp.transpose(w_in, (0, 2, 1))
    wao = w_attn_o.reshape(DEPTH, N_HEADS, HEAD_DIM, D_MODEL)[:, perm].reshape(DEPTH, ATTN_Q, D_MODEL).astype(BF16)
    wdo = w_dn_o.astype(BF16)
    wout = w_out.astype(BF16)
    wgu1, wd1, wgu2, wd2 = ffn1_w_gate_up, ffn1_w_down, ffn2_w_gate_up, ffn2_w_down
    n1 = ffn1_norm.reshape(DEPTH, 1, D_MODEL)
    n2 = ffn2_norm.reshape(DEPTH, 1, D_MODEL)
    nm = mix_norm.reshape(DEPTH, 1, D_MODEL)
    qk_nw = jnp.concatenate([jnp.tile(q_norm, (1, N_HEADS)), jnp.tile(k_norm, (1, N_KV_HEADS))], axis=1)
    alog_row = _pad_row(dn_A_log, DN_HEADS)
    dtb_row = _pad_row(dn_dt_bias, DN_HEADS)
    onw = dn_out_norm.reshape(DEPTH, 1, DN_DV)
    sinks_perm = attn_sinks[:, perm]
    sinks_b = jnp.broadcast_to(sinks_perm[:, :, None], (DEPTH, N_HEADS, LANES))

    ltri, lall, lvl, ones_bd = _np_consts()
    gdn_consts = (jnp.asarray(ltri, BF16), jnp.asarray(lall, BF16), jnp.asarray(lvl, BF16))
    ones_bd = jnp.asarray(ones_bd, BF16)
    cos_p, sin_p = _rope_tables(jnp.arange(seq))
    cos_s, sin_s = _rope_tables(PAST_LEN + jnp.arange(1))

    ck = jnp.transpose(cache_swa_k, (0, 1, 3, 4, 2))
    cv = jnp.transpose(cache_swa_v, (0, 1, 3, 4, 2))
    cs = jnp.transpose(state_conv, (0, 2, 1, 3))

    split_rows = _can_split_rows(prows, dbatch)
    xp2, xs2 = x_prompt.reshape(prows, D_MODEL), x_sample.reshape(dbatch, D_MODEL)
    kp = vp = sp = cp = ksn = vsn = ssn = csn = None
    for l in range(DEPTH):
        if l > 0:
            h = _ffn(h, n1, wgu1, wd1, l)
        elif split_rows:
            h = _ffn_first(xp2, xs2, n1, wgu1, wd1, l)
        else:
            h = _ffn(jnp.concatenate([xp2, xs2], axis=0), n1, wgu1, wd1, l)
        qkv, raw, z, ba, gates = _inproj(h, nm, w_in_t, l)
        oa, kp, vp = _attn_prompt(qkv, attn_sinks[l], cos_p, sin_p, qk_nw[l:l + 1], ones_bd,
                                  kp, vp, l, batch, seq, rows)
        oa, ksn, vsn = _attn_sample(qkv, ck, cv, cos_s, sin_s, qk_nw[l:l + 1], ones_bd,
                                    sinks_b[l], oa, ksn, vsn, l, prows, dbatch)
        od, sp, cp = _gdn_prompt(raw, z, ba, conv_w, alog_row, dtb_row, onw, gdn_consts,
                                 sp, cp, l, batch, seq, rows)
        od, ssn, csn = _gdn_sample(raw, z, ba, cs, state_dn, conv_w, alog_row, dtb_row, onw,
                                   od, ssn, csn, l, prows, dbatch)
        final_split = (prows, dbatch) if (split_rows and l == DEPTH - 1) else None
        h = _merge_ffn(h, oa, od, gates, wao, wdo, wout, n2, wgu2, wd2, l, split=final_split)

    yp, ys = h if split_rows else (h[:prows], h[prows:])
    kv_shape = (DEPTH, -1, WINDOW, N_KV_HEADS, HEAD_DIM)
    return (yp.reshape(batch, seq, D_MODEL),
            ys.reshape(dbatch, 1, D_MODEL),
            kp.reshape(kv_shape), vp.reshape(kv_shape), sp, cp,
            jnp.transpose(ksn, (0, 1, 4, 2, 3)), jnp.transpose(vsn, (0, 1, 4, 2, 3)), ssn,
            jnp.transpose(csn, (0, 2, 1, 3)))
```
